```python
import math
import jax, jax.numpy as jnp
from jax import lax
import numpy as np

D_MODEL = 1024
BATCH = 8
SEQ = 4096
DEPTH = 1

CHUNK = 64
SSM_WIDTH = D_MODEL // 2
SSM_GROUP = 16
SSM_GROUPS = SSM_WIDTH // SSM_GROUP
SSM_STATE = 64
CONV_WIDTH = D_MODEL // 2
CONV_K = 3
FFN_HIDDEN = ((8 * D_MODEL // 3 + 255) // 256) * 256
IN_COLS = SSM_WIDTH + 3 * CONV_WIDTH + 2 * D_MODEL
ALPHA = (2.0 * DEPTH) ** 0.25
BETA = (8.0 * DEPTH) ** -0.25
DT_MIN = 0.001
DT_MAX = 0.1
LN_EPS = 1e-5

kernel_name = "hybrid_s5_shortconv_gated_deepnorm_block"


def layer_norm(x, g, b):
    xf = x.astype(jnp.float32)
    mu = jnp.mean(xf, axis=-1, keepdims=True)
    var = jnp.mean(jnp.square(xf - mu), axis=-1, keepdims=True)
    y = (xf - mu) * lax.rsqrt(var + LN_EPS) * g.astype(jnp.float32) + b.astype(jnp.float32)
    return y.astype(x.dtype)


def _complex_affine_combine(earlier, later):
    ar_i, ai_i, br_i, bi_i = earlier
    ar_j, ai_j, br_j, bi_j = later
    ar = ar_j * ar_i - ai_j * ai_i
    ai = ar_j * ai_i + ai_j * ar_i
    br = ar_j * br_i - ai_j * bi_i + br_j
    bi = ar_j * bi_i + ai_j * br_i + bi_j
    return (ar, ai, br, bi)


def s5_ssm(u, lam_re, lam_im, log_dt, b_re, b_im, c_re, c_im, d_skip):
    f32 = jnp.float32
    bsz, slen, _ = u.shape
    uf = u.astype(f32).reshape(bsz, slen, SSM_GROUPS, SSM_GROUP).transpose(1, 0, 2, 3)
    lr = lam_re.astype(f32)
    li = lam_im.astype(f32)
    dt = jnp.exp(log_dt.astype(f32))[:, None]
    mag = jnp.exp(lr * dt)
    lb_re = mag * jnp.cos(li * dt)
    lb_im = mag * jnp.sin(li * dt)
    den = lr * lr + li * li
    num_re = lb_re - 1.0
    fr = (num_re * lr + lb_im * li) / den
    fi = (lb_im * lr - num_re * li) / den
    br = b_re.astype(f32)
    bi = b_im.astype(f32)
    bb_re = fr[..., None] * br - fi[..., None] * bi
    bb_im = fr[..., None] * bi + fi[..., None] * br
    bu_re = jnp.einsum('sbgc,gpc->sbgp', uf, bb_re)
    bu_im = jnp.einsum('sbgc,gpc->sbgp', uf, bb_im)
    a_re = jnp.broadcast_to(lb_re[None, None], (slen, 1, SSM_GROUPS, SSM_STATE))
    a_im = jnp.broadcast_to(lb_im[None, None], (slen, 1, SSM_GROUPS, SSM_STATE))
    _, _, xs_re, xs_im = lax.associative_scan(
        _complex_affine_combine, (a_re, a_im, bu_re, bu_im), axis=0)
    y = (jnp.einsum('sbgp,gcp->sbgc', xs_re, c_re.astype(f32))
         - jnp.einsum('sbgp,gcp->sbgc', xs_im, c_im.astype(f32))
         + d_skip.astype(f32).reshape(SSM_GROUPS, SSM_GROUP) * uf)
    y = y.transpose(1, 0, 2, 3).reshape(bsz, slen, SSM_WIDTH)
    return y.astype(u.dtype)


def causal_depthwise_conv(z, w):
    return lax.conv_general_dilated(
        z, w[:, None, :].astype(z.dtype), window_strides=(1,), padding=[(CONV_K - 1, 0)],
        dimension_numbers=('NWC', 'WIO', 'NWC'), feature_group_count=CONV_WIDTH)


def hybrid_layer(x, w_in, b_in, ssm_lambda_re, ssm_lambda_im, ssm_log_dt, ssm_b_re, ssm_b_im,
                 ssm_c_re, ssm_c_im, ssm_d, glu_w, glu_b, w_ssm_out, conv_w, w_conv_out, w_o,
                 ln1_g, ln1_b, w_gate, w_up, w_down, ln2_g, ln2_b):
    proj = jnp.einsum('bsd,dn->bsn', x, w_in) + b_in
    o1 = SSM_WIDTH
    o2 = o1 + CONV_WIDTH
    o3 = o2 + CONV_WIDTH
    o4 = o3 + CONV_WIDTH
    o5 = o4 + D_MODEL
    u, h, c_gate, b_gate, gate_a, gate_b = jnp.split(proj, [o1, o2, o3, o4, o5], axis=-1)

    y_a = s5_ssm(u, ssm_lambda_re, ssm_lambda_im, ssm_log_dt, ssm_b_re, ssm_b_im,
                 ssm_c_re, ssm_c_im, ssm_d)
    g = jax.nn.gelu(y_a)
    y_a = g * jax.nn.sigmoid(jnp.einsum('bsc,ce->bse', g, glu_w) + glu_b)
    y_a = jnp.einsum('bsc,cd->bsd', y_a, w_ssm_out)

    z = causal_depthwise_conv(c_gate * h, conv_w)
    y_b = jnp.einsum('bsc,cd->bsd', b_gate * z, w_conv_out)

    merged = jax.nn.sigmoid(gate_a) * y_a + jax.nn.sigmoid(gate_b) * y_b
    mix = jnp.einsum('bsd,de->bse', merged, w_o)
    x = layer_norm(ALPHA * x + mix, ln1_g, ln1_b)

    hid = jax.nn.silu(jnp.einsum('bsd,df->bsf', x, w_gate)) * jnp.einsum('bsd,df->bsf', x, w_up)
    ffn = jnp.einsum('bsf,fd->bsd', hid, w_down)
    x = layer_norm(ALPHA * x + ffn, ln2_g, ln2_b)
    return x


def setup_inputs(seed: int = 0) -> dict:
    key = jax.random.key(seed)
    ks = jax.random.split(key, 24)
    L = DEPTH
    f32 = jnp.float32
    nrm = lambda k, shape, s: jax.random.normal(k, shape, f32) * s
    x = jax.random.normal(ks[0], (BATCH, SEQ, D_MODEL), f32)
    w_in = nrm(ks[1], (L, D_MODEL, IN_COLS), D_MODEL ** -0.5)
    b_in = nrm(ks[2], (L, IN_COLS), 0.01)
    n_idx = jnp.arange(SSM_STATE, dtype=f32)
    ssm_lambda_re = -0.5 + nrm(ks[3], (L, SSM_GROUPS, SSM_STATE), 0.01)
    ssm_lambda_im = math.pi * n_idx[None, None, :] + nrm(ks[4], (L, SSM_GROUPS, SSM_STATE), 0.01)
    ssm_log_dt = jax.random.uniform(ks[5], (L, SSM_GROUPS), f32,
                                    minval=math.log(DT_MIN), maxval=math.log(DT_MAX))
    ssm_b_re = nrm(ks[6], (L, SSM_GROUPS, SSM_STATE, SSM_GROUP), (2.0 * SSM_GROUP) ** -0.5)
    ssm_b_im = nrm(ks[7], (L, SSM_GROUPS, SSM_STATE, SSM_GROUP), (2.0 * SSM_GROUP) ** -0.5)
    ssm_c_re = nrm(ks[8], (L, SSM_GROUPS, SSM_GROUP, SSM_STATE), SSM_STATE ** -0.5)
    ssm_c_im = nrm(ks[9], (L, SSM_GROUPS, SSM_GROUP, SSM_STATE), SSM_STATE ** -0.5)
    ssm_d = nrm(ks[10], (L, SSM_WIDTH), 1.0)
    glu_w = nrm(ks[11], (L, SSM_WIDTH, SSM_WIDTH), SSM_WIDTH ** -0.5)
    glu_b = nrm(ks[12], (L, SSM_WIDTH), 0.01)
    w_ssm_out = nrm(ks[13], (L, SSM_WIDTH, D_MODEL), BETA * SSM_WIDTH ** -0.5)
    conv_w = nrm(ks[14], (L, CONV_K, CONV_WIDTH), CONV_K ** -0.5)
    w_conv_out = nrm(ks[15], (L, CONV_WIDTH, D_MODEL), BETA * CONV_WIDTH ** -0.5)
    w_o = nrm(ks[16], (L, D_MODEL, D_MODEL), BETA * D_MODEL ** -0.5)
    ln1_g = 1.0 + nrm(ks[17], (L, D_MODEL), 0.01)
    ln1_b = nrm(ks[18], (L, D_MODEL), 0.01)
    w_gate = nrm(ks[19], (L, D_MODEL, FFN_HIDDEN), D_MODEL ** -0.5)
    w_up = nrm(ks[20], (L, D_MODEL, FFN_HIDDEN), D_MODEL ** -0.5)
    w_down = nrm(ks[21], (L, FFN_HIDDEN, D_MODEL), BETA * FFN_HIDDEN ** -0.5)
    ln2_g = 1.0 + nrm(ks[22], (L, D_MODEL), 0.01)
    ln2_b = nrm(ks[23], (L, D_MODEL), 0.01)
    return {"x": x, "w_in": w_in, "b_in": b_in,
            "ssm_lambda_re": ssm_lambda_re, "ssm_lambda_im": ssm_lambda_im,
            "ssm_log_dt": ssm_log_dt, "ssm_b_re": ssm_b_re, "ssm_b_im": ssm_b_im,
            "ssm_c_re": ssm_c_re, "ssm_c_im": ssm_c_im, "ssm_d": ssm_d,
            "glu_w": glu_w, "glu_b": glu_b, "w_ssm_out": w_ssm_out,
            "conv_w": conv_w, "w_conv_out": w_conv_out, "w_o": w_o,
            "ln1_g": ln1_g, "ln1_b": ln1_b, "w_gate": w_gate, "w_up": w_up,
            "w_down": w_down, "ln2_g": ln2_g, "ln2_b": ln2_b}


def reference(x, w_in, b_in, ssm_lambda_re, ssm_lambda_im, ssm_log_dt, ssm_b_re, ssm_b_im,
              ssm_c_re, ssm_c_im, ssm_d, glu_w, glu_b, w_ssm_out, conv_w, w_conv_out, w_o,
              ln1_g, ln1_b, w_gate, w_up, w_down, ln2_g, ln2_b):
    for l in range(DEPTH):
        x = hybrid_layer(x, w_in[l], b_in[l], ssm_lambda_re[l], ssm_lambda_im[l], ssm_log_dt[l],
                         ssm_b_re[l], ssm_b_im[l], ssm_c_re[l], ssm_c_im[l], ssm_d[l],
                         glu_w[l], glu_b[l], w_ssm_out[l], conv_w[l], w_conv_out[l], w_o[l],
                         ln1_g[l], ln1_b[l], w_gate[l], w_up[l], w_down[l], ln2_g[l], ln2_b[l])
    return x
```

```python
import functools
import math

import jax
import jax.numpy as jnp
from jax import lax
from jax.experimental import pallas as pl
from jax.experimental.pallas import tpu as pltpu

D_MODEL = 1024
BATCH = 8
SEQ = 4096
SSM_WIDTH = D_MODEL // 2
SSM_GROUP = 16
SSM_GROUPS = SSM_WIDTH // SSM_GROUP
SSM_STATE = 64
CONV_WIDTH = D_MODEL // 2
CONV_K = 3
FFN_HIDDEN = 2816
IN_COLS = SSM_WIDTH + 3 * CONV_WIDTH + 2 * D_MODEL
DEPTH = 1
ALPHA = (2.0 * DEPTH) ** 0.25
LN_EPS = 1e-5

SUBLANES = 8
LANES = 128
VMEM_LIMIT_BYTES = 56 * 1024 * 1024

GROUPS_PER_BLOCK = 8
N_BLOCKS = SSM_GROUPS // GROUPS_PER_BLOCK
BLOCK_CH = GROUPS_PER_BLOCK * SSM_GROUP
BLOCK_ST = GROUPS_PER_BLOCK * SSM_STATE
N_STATE = SSM_GROUPS * SSM_STATE

T_TILE = 64
R_TILE = T_TILE * BATCH
FFN_ROWS = 512

f32 = jnp.float32
bf16 = jnp.bfloat16


def _layer_norm(y, g, b):
    mu = jnp.mean(y, axis=-1, keepdims=True)
    var = jnp.mean(jnp.square(y - mu), axis=-1, keepdims=True)
    return (y - mu) * lax.rsqrt(var + LN_EPS) * g + b


def _mixer_kernel(x_ref, win_ref, bin_ref, bmat_ref, are_ref, aim_ref, cre_ref, cimn_ref,
                  dskip_ref, gluw_ref, glub_ref, wssm_ref, convw_ref, wconv_ref, wo_ref,
                  g_ref, b_ref, o_ref, st_re, st_im, xs_re, xs_im, vext):
    rows = x_ref.shape[0]
    steps = rows // SUBLANES

    @pl.when(pl.program_id(0) == 0)
    def _():
        st_re[...] = jnp.zeros_like(st_re)
        st_im[...] = jnp.zeros_like(st_im)
        vext[0:(CONV_K - 1) * SUBLANES, :] = jnp.zeros(((CONV_K - 1) * SUBLANES, CONV_WIDTH), f32)

    x = x_ref[...]
    xb = x.astype(bf16)

    def proj(lo, hi):
        return jnp.dot(xb, win_ref[:, lo:hi], preferred_element_type=f32) + bin_ref[:, lo:hi]

    u = proj(0, SSM_WIDTH)
    ub = u.astype(bf16)
    for j in range(N_BLOCKS):
        bu = jnp.dot(ub[:, j * BLOCK_CH:(j + 1) * BLOCK_CH], bmat_ref[j],
                     preferred_element_type=f32)
        xs_re[:, j * BLOCK_ST:(j + 1) * BLOCK_ST] = bu[:, :BLOCK_ST]
        xs_im[:, j * BLOCK_ST:(j + 1) * BLOCK_ST] = bu[:, BLOCK_ST:]

    for j in range(N_BLOCKS):
        sl = pl.ds(j * BLOCK_ST, BLOCK_ST)
        ar = jnp.broadcast_to(are_ref[:, sl], (SUBLANES, BLOCK_ST))
        ai = jnp.broadcast_to(aim_ref[:, sl], (SUBLANES, BLOCK_ST))

        def body(t, carry, sl=sl, ar=ar, ai=ai):
            sr, si = carry
            r0 = pl.multiple_of(t * SUBLANES, SUBLANES)
            nr = ar * sr - ai * si + xs_re[pl.ds(r0, SUBLANES), sl]
            ni = ar * si + ai * sr + xs_im[pl.ds(r0, SUBLANES), sl]
            xs_re[pl.ds(r0, SUBLANES), sl] = nr
            xs_im[pl.ds(r0, SUBLANES), sl] = ni
            return nr, ni

        sr, si = lax.fori_loop(0, steps, body, (st_re[:, sl], st_im[:, sl]), unroll=8)
        st_re[:, sl] = sr
        st_im[:, sl] = si

    ys = []
    for j in range(N_BLOCKS):
        sl = slice(j * BLOCK_ST, (j + 1) * BLOCK_ST)
        ys.append(jnp.dot(xs_re[:, sl].astype(bf16), cre_ref[j], preferred_element_type=f32)
                  + jnp.dot(xs_im[:, sl].astype(bf16), cimn_ref[j], preferred_element_type=f32))
    y_a = jnp.concatenate(ys, axis=-1) + dskip_ref[...] * u
    g = jax.nn.gelu(y_a)
    y_a = g * jax.nn.sigmoid(
        jnp.dot(g.astype(bf16), gluw_ref[...], preferred_element_type=f32) + glub_ref[...])
    y_a = jnp.dot(y_a.astype(bf16), wssm_ref[...], preferred_element_type=f32)

    h = proj(SSM_WIDTH, SSM_WIDTH + CONV_WIDTH)
    c_gate = proj(SSM_WIDTH + CONV_WIDTH, SSM_WIDTH + 2 * CONV_WIDTH)
    halo = (CONV_K - 1) * SUBLANES
    vext[halo:halo + rows, :] = c_gate * h
    z = convw_ref[0:1, :] * vext[0:rows, :]
    for k in range(1, CONV_K):
        z = z + convw_ref[k:k + 1, :] * vext[k * SUBLANES:k * SUBLANES + rows, :]
    vext[0:halo, :] = vext[rows:rows + halo, :]
    b_gate = proj(SSM_WIDTH + 2 * CONV_WIDTH, SSM_WIDTH + 3 * CONV_WIDTH)
    y_b = jnp.dot((b_gate * z).astype(bf16), wconv_ref[...], preferred_element_type=f32)

    o4 = SSM_WIDTH + 3 * CONV_WIDTH
    merged = (jax.nn.sigmoid(proj(o4, o4 + D_MODEL)) * y_a
              + jax.nn.sigmoid(proj(o4 + D_MODEL, o4 + 2 * D_MODEL)) * y_b)
    mix = jnp.dot(merged.astype(bf16), wo_ref[...], preferred_element_type=f32)
    o_ref[...] = _layer_norm(ALPHA * x + mix, g_ref[...], b_ref[...])


def _ffn_kernel(x_ref, wg_ref, wu_ref, wd_ref, g_ref, b_ref, o_ref):
    x = x_ref[...]
    xb = x.astype(bf16)
    gate = jnp.dot(xb, wg_ref[...], preferred_element_type=f32)
    up = jnp.dot(xb, wu_ref[...], preferred_element_type=f32)
    hid = (jax.nn.silu(gate) * up).astype(bf16)
    ffn = jnp.dot(hid, wd_ref[...], preferred_element_type=f32)
    o_ref[...] = _layer_norm(ALPHA * x + ffn, g_ref[...], b_ref[...])


def _const_spec(shape):
    nd = len(shape)
    return pl.BlockSpec(shape, lambda i, nd=nd: (0,) * nd, pipeline_mode=pl.Buffered(1))


def _ssm_params(lam_re, lam_im, log_dt, b_re, b_im, c_re, c_im):
    dt = jnp.exp(log_dt)[:, None]
    mag = jnp.exp(lam_re * dt)
    lb_re = mag * jnp.cos(lam_im * dt)
    lb_im = mag * jnp.sin(lam_im * dt)
    den = lam_re * lam_re + lam_im * lam_im
    num_re = lb_re - 1.0
    fr = (num_re * lam_re + lb_im * lam_im) / den
    fi = (lb_im * lam_re - num_re * lam_im) / den
    bb_re = fr[..., None] * b_re - fi[..., None] * b_im
    bb_im = fr[..., None] * b_im + fi[..., None] * b_re
    eye = jnp.eye(GROUPS_PER_BLOCK, dtype=f32)

    def in_block(bb):
        bb = bb.reshape(N_BLOCKS, GROUPS_PER_BLOCK, SSM_STATE, SSM_GROUP)
        m = jnp.einsum('jgpc,gh->jgchp', bb, eye)
        return m.reshape(N_BLOCKS, BLOCK_CH, BLOCK_ST)

    def out_block(cc):
        cc = cc.reshape(N_BLOCKS, GROUPS_PER_BLOCK, SSM_GROUP, SSM_STATE)
        m = jnp.einsum('jgcp,gh->jgphc', cc, eye)
        return m.reshape(N_BLOCKS, BLOCK_ST, BLOCK_CH)

    bmat = jnp.concatenate([in_block(bb_re), in_block(bb_im)], axis=-1).astype(bf16)
    return (bmat, lb_re.reshape(1, N_STATE), lb_im.reshape(1, N_STATE),
            out_block(c_re).astype(bf16), out_block(-c_im).astype(bf16))


def _layer(xt, w_in, b_in, lam_re, lam_im, log_dt, b_re, b_im, c_re, c_im, d_skip, glu_w, glu_b,
           w_ssm_out, conv_w, w_conv_out, w_o, ln1_g, ln1_b, w_gate, w_up, w_down, ln2_g, ln2_b):
    n_rows = xt.shape[0]
    bmat, a_re, a_im, cre, cimn = _ssm_params(lam_re, lam_im, log_dt, b_re, b_im, c_re, c_im)
    row = lambda v: v.reshape(1, -1).astype(f32)
    mixer_in = (w_in.astype(bf16), row(b_in), bmat, a_re, a_im, cre, cimn, row(d_skip),
                glu_w.astype(bf16), row(glu_b), w_ssm_out.astype(bf16), conv_w.astype(f32),
                w_conv_out.astype(bf16), w_o.astype(bf16), row(ln1_g), row(ln1_b))
    halo = (CONV_K - 1) * SUBLANES
    x1 = pl.pallas_call(
        _mixer_kernel,
        grid=(n_rows // R_TILE,),
        in_specs=[pl.BlockSpec((R_TILE, D_MODEL), lambda i: (i, 0))]
                 + [_const_spec(a.shape) for a in mixer_in],
        out_specs=pl.BlockSpec((R_TILE, D_MODEL), lambda i: (i, 0)),
        out_shape=jax.ShapeDtypeStruct((n_rows, D_MODEL), f32),
        scratch_shapes=[pltpu.VMEM((SUBLANES, N_STATE), f32), pltpu.VMEM((SUBLANES, N_STATE), f32),
                        pltpu.VMEM((R_TILE, N_STATE), f32), pltpu.VMEM((R_TILE, N_STATE), f32),
                        pltpu.VMEM((R_TILE + halo, CONV_WIDTH), f32)],
        compiler_params=pltpu.CompilerParams(dimension_semantics=("arbitrary",),
                                             vmem_limit_bytes=VMEM_LIMIT_BYTES),
        name="mixer",
    )(xt, *mixer_in)

    ffn_in = (w_gate.astype(bf16), w_up.astype(bf16), w_down.astype(bf16), row(ln2_g), row(ln2_b))
    return pl.pallas_call(
        _ffn_kernel,
        grid=(n_rows // FFN_ROWS,),
        in_specs=[pl.BlockSpec((FFN_ROWS, D_MODEL), lambda i: (i, 0))]
                 + [_const_spec(a.shape) for a in ffn_in],
        out_specs=pl.BlockSpec((FFN_ROWS, D_MODEL), lambda i: (i, 0)),
        out_shape=jax.ShapeDtypeStruct((n_rows, D_MODEL), f32),
        compiler_params=pltpu.CompilerParams(dimension_semantics=("arbitrary",),
                                             vmem_limit_bytes=VMEM_LIMIT_BYTES),
        name="ffn",
    )(x1, *ffn_in)


def kernel(x, w_in, b_in, ssm_lambda_re, ssm_lambda_im, ssm_log_dt, ssm_b_re, ssm_b_im, ssm_c_re, ssm_c_im, ssm_d, glu_w, glu_b, w_ssm_out, conv_w, w_conv_out, w_o, ln1_g, ln1_b, w_gate, w_up, w_down, ln2_g, ln2_b):
    bsz, slen, d = x.shape
    assert (bsz, d) == (BATCH, D_MODEL) and slen % T_TILE == 0 and BATCH == SUBLANES
    xt = jnp.transpose(x, (1, 0, 2)).reshape(slen * bsz, d)
    for l in range(w_in.shape[0]):
        xt = _layer(xt, w_in[l], b_in[l], ssm_lambda_re[l], ssm_lambda_im[l], ssm_log_dt[l],
                    ssm_b_re[l], ssm_b_im[l], ssm_c_re[l], ssm_c_im[l], ssm_d[l], glu_w[l],
                    glu_b[l], w_ssm_out[l], conv_w[l], w_conv_out[l], w_o[l], ln1_g[l], ln1_b[l],
                    w_gate[l], w_up[l], w_down[l], ln2_g[l], ln2_b[l])
    return jnp.transpose(xt.reshape(slen, bsz, d), (1, 0, 2))
```

```python
import functools
import math

import jax
import jax.numpy as jnp
from jax import lax
from jax.experimental import pallas as pl
from jax.experimental.pallas import tpu as pltpu

D_MODEL = 1024
BATCH = 8
SEQ = 4096
SSM_WIDTH = D_MODEL // 2
SSM_GROUP = 16
SSM_GROUPS = SSM_WIDTH // SSM_GROUP
SSM_STATE = 64
CONV_WIDTH = D_MODEL // 2
CONV_K = 3
FFN_HIDDEN = 2816
IN_COLS = SSM_WIDTH + 3 * CONV_WIDTH + 2 * D_MODEL
DEPTH = 1
ALPHA = (2.0 * DEPTH) ** 0.25
LN_EPS = 1e-5

SUBLANES = 8
LANES = 128
VMEM_LIMIT_BYTES = 56 * 1024 * 1024

GROUPS_PER_BLOCK = 8
N_BLOCKS = SSM_GROUPS // GROUPS_PER_BLOCK
BLOCK_CH = GROUPS_PER_BLOCK * SSM_GROUP
BLOCK_ST = GROUPS_PER_BLOCK * SSM_STATE
N_STATE = SSM_GROUPS * SSM_STATE

T_TILE = 64
R_TILE = T_TILE * BATCH
FFN_T_TILE = 64
FFN_ROWS = FFN_T_TILE * BATCH
FFN_N_TILES = SEQ // FFN_T_TILE

f32 = jnp.float32
bf16 = jnp.bfloat16


def _layer_norm(y, g, b):
    mu = jnp.mean(y, axis=-1, keepdims=True)
    var = jnp.mean(jnp.square(y - mu), axis=-1, keepdims=True)
    return (y - mu) * lax.rsqrt(var + LN_EPS) * g + b


def _x_tile_copy(x_hbm, xbuf, sem, tile, slot, b):
    return pltpu.make_async_copy(x_hbm.at[b, pl.ds(tile * T_TILE, T_TILE), :],
                                 xbuf.at[slot, :, b, :], sem.at[slot, b])


def _mixer_kernel(x_hbm, win_ref, bin_ref, bmat_ref, are_ref, aim_ref, cre_ref, cimn_ref,
                  dskip_ref, gluw_ref, glub_ref, wssm_ref, convw_ref, wconv_ref, wo_ref,
                  g_ref, b_ref, o_ref, xbuf, xsem, st_re, st_im, xs_re, xs_im, vext):
    rows = o_ref.shape[0]
    steps = rows // SUBLANES
    i = pl.program_id(0)
    slot = i % 2

    @pl.when(i == 0)
    def _():
        for b in range(BATCH):
            _x_tile_copy(x_hbm, xbuf, xsem, 0, 0, b).start()
        st_re[...] = jnp.zeros_like(st_re)
        st_im[...] = jnp.zeros_like(st_im)
        vext[0:(CONV_K - 1) * SUBLANES, :] = jnp.zeros(((CONV_K - 1) * SUBLANES, CONV_WIDTH), f32)

    @pl.when(i + 1 < pl.num_programs(0))
    def _():
        for b in range(BATCH):
            _x_tile_copy(x_hbm, xbuf, xsem, i + 1, 1 - slot, b).start()

    for b in range(BATCH):
        _x_tile_copy(x_hbm, xbuf, xsem, i, slot, b).wait()

    x = xbuf[slot].reshape(rows, D_MODEL)
    xb = x.astype(bf16)

    def proj(lo, hi):
        return jnp.dot(xb, win_ref[:, lo:hi], preferred_element_type=f32) + bin_ref[:, lo:hi]

    u = proj(0, SSM_WIDTH)
    ub = u.astype(bf16)
    for j in range(N_BLOCKS):
        bu = jnp.dot(ub[:, j * BLOCK_CH:(j + 1) * BLOCK_CH], bmat_ref[j],
                     preferred_element_type=f32)
        xs_re[:, j * BLOCK_ST:(j + 1) * BLOCK_ST] = bu[:, :BLOCK_ST]
        xs_im[:, j * BLOCK_ST:(j + 1) * BLOCK_ST] = bu[:, BLOCK_ST:]

    for j in range(N_BLOCKS):
        sl = pl.ds(j * BLOCK_ST, BLOCK_ST)
        ar = jnp.broadcast_to(are_ref[:, sl], (SUBLANES, BLOCK_ST))
        ai = jnp.broadcast_to(aim_ref[:, sl], (SUBLANES, BLOCK_ST))

        def body(t, carry, sl=sl, ar=ar, ai=ai):
            sr, si = carry
            r0 = pl.multiple_of(t * SUBLANES, SUBLANES)
            nr = ar * sr - ai * si + xs_re[pl.ds(r0, SUBLANES), sl]
            ni = ar * si + ai * sr + xs_im[pl.ds(r0, SUBLANES), sl]
            xs_re[pl.ds(r0, SUBLANES), sl] = nr
            xs_im[pl.ds(r0, SUBLANES), sl] = ni
            return nr, ni

        sr, si = lax.fori_loop(0, steps, body, (st_re[:, sl], st_im[:, sl]), unroll=8)
        st_re[:, sl] = sr
        st_im[:, sl] = si

    ys = []
    for j in range(N_BLOCKS):
        sl = slice(j * BLOCK_ST, (j + 1) * BLOCK_ST)
        ys.append(jnp.dot(xs_re[:, sl].astype(bf16), cre_ref[j], preferred_element_type=f32)
                  + jnp.dot(xs_im[:, sl].astype(bf16), cimn_ref[j], preferred_element_type=f32))
    y_a = jnp.concatenate(ys, axis=-1) + dskip_ref[...] * u
    g = jax.nn.gelu(y_a)
    y_a = g * jax.nn.sigmoid(
        jnp.dot(g.astype(bf16), gluw_ref[...], preferred_element_type=f32) + glub_ref[...])
    y_a = jnp.dot(y_a.astype(bf16), wssm_ref[...], preferred_element_type=f32)

    h = proj(SSM_WIDTH, SSM_WIDTH + CONV_WIDTH)
    c_gate = proj(SSM_WIDTH + CONV_WIDTH, SSM_WIDTH + 2 * CONV_WIDTH)
    halo = (CONV_K - 1) * SUBLANES
    vext[halo:halo + rows, :] = c_gate * h
    z = convw_ref[0:1, :] * vext[0:rows, :]
    for k in range(1, CONV_K):
        z = z + convw_ref[k:k + 1, :] * vext[k * SUBLANES:k * SUBLANES + rows, :]
    vext[0:halo, :] = vext[rows:rows + halo, :]
    b_gate = proj(SSM_WIDTH + 2 * CONV_WIDTH, SSM_WIDTH + 3 * CONV_WIDTH)
    y_b = jnp.dot((b_gate * z).astype(bf16), wconv_ref[...], preferred_element_type=f32)

    o4 = SSM_WIDTH + 3 * CONV_WIDTH
    merged = (jax.nn.sigmoid(proj(o4, o4 + D_MODEL)) * y_a
              + jax.nn.sigmoid(proj(o4 + D_MODEL, o4 + 2 * D_MODEL)) * y_b)
    mix = jnp.dot(merged.astype(bf16), wo_ref[...], preferred_element_type=f32)
    o_ref[...] = _layer_norm(ALPHA * x + mix, g_ref[...], b_ref[...])


def _out_tile_copy(obuf, o_hbm, sem, tile, slot, b):
    return pltpu.make_async_copy(obuf.at[slot, :, b, :],
                                 o_hbm.at[b, pl.ds(tile * FFN_T_TILE, FFN_T_TILE), :],
                                 sem.at[slot, b])


def _ffn_kernel(x_ref, wg_ref, wu_ref, wd_ref, g_ref, b_ref, o_hbm, obuf, osem):
    i = pl.program_id(0)
    last = pl.num_programs(0) - 1
    slot = i % 2
    x = x_ref[...]
    xb = x.astype(bf16)
    gate = jnp.dot(xb, wg_ref[...], preferred_element_type=f32)
    up = jnp.dot(xb, wu_ref[...], preferred_element_type=f32)
    hid = (jax.nn.silu(gate) * up).astype(bf16)
    ffn = jnp.dot(hid, wd_ref[...], preferred_element_type=f32)
    y = _layer_norm(ALPHA * x + ffn, g_ref[...], b_ref[...])

    @pl.when(i >= 2)
    def _():
        for b in range(BATCH):
            _out_tile_copy(obuf, o_hbm, osem, i - 2, slot, b).wait()

    obuf[slot] = y.reshape(FFN_T_TILE, BATCH, D_MODEL)
    for b in range(BATCH):
        _out_tile_copy(obuf, o_hbm, osem, i, slot, b).start()

    @pl.when(i == last)
    def _():
        if FFN_N_TILES >= 2:
            for b in range(BATCH):
                _out_tile_copy(obuf, o_hbm, osem, i - 1, 1 - slot, b).wait()
        for b in range(BATCH):
            _out_tile_copy(obuf, o_hbm, osem, i, slot, b).wait()


def _const_spec(shape):
    nd = len(shape)
    return pl.BlockSpec(shape, lambda i, nd=nd: (0,) * nd, pipeline_mode=pl.Buffered(1))


def _ssm_params(lam_re, lam_im, log_dt, b_re, b_im, c_re, c_im):
    dt = jnp.exp(log_dt)[:, None]
    mag = jnp.exp(lam_re * dt)
    lb_re = mag * jnp.cos(lam_im * dt)
    lb_im = mag * jnp.sin(lam_im * dt)
    den = lam_re * lam_re + lam_im * lam_im
    num_re = lb_re - 1.0
    fr = (num_re * lam_re + lb_im * lam_im) / den
    fi = (lb_im * lam_re - num_re * lam_im) / den
    bb_re = fr[..., None] * b_re - fi[..., None] * b_im
    bb_im = fr[..., None] * b_im + fi[..., None] * b_re
    eye = jnp.eye(GROUPS_PER_BLOCK, dtype=f32)

    def in_block(bb):
        bb = bb.reshape(N_BLOCKS, GROUPS_PER_BLOCK, SSM_STATE, SSM_GROUP)
        m = jnp.einsum('jgpc,gh->jgchp', bb, eye)
        return m.reshape(N_BLOCKS, BLOCK_CH, BLOCK_ST)

    def out_block(cc):
        cc = cc.reshape(N_BLOCKS, GROUPS_PER_BLOCK, SSM_GROUP, SSM_STATE)
        m = jnp.einsum('jgcp,gh->jgphc', cc, eye)
        return m.reshape(N_BLOCKS, BLOCK_ST, BLOCK_CH)

    bmat = jnp.concatenate([in_block(bb_re), in_block(bb_im)], axis=-1).astype(bf16)
    return (bmat, lb_re.reshape(1, N_STATE), lb_im.reshape(1, N_STATE),
            out_block(c_re).astype(bf16), out_block(-c_im).astype(bf16))


def _layer(x, w_in, b_in, lam_re, lam_im, log_dt, b_re, b_im, c_re, c_im, d_skip, glu_w, glu_b,
           w_ssm_out, conv_w, w_conv_out, w_o, ln1_g, ln1_b, w_gate, w_up, w_down, ln2_g, ln2_b):
    n_rows = SEQ * BATCH
    bmat, a_re, a_im, cre, cimn = _ssm_params(lam_re, lam_im, log_dt, b_re, b_im, c_re, c_im)
    row = lambda v: v.reshape(1, -1).astype(f32)
    mixer_in = (w_in.astype(bf16), row(b_in), bmat, a_re, a_im, cre, cimn, row(d_skip),
                glu_w.astype(bf16), row(glu_b), w_ssm_out.astype(bf16), conv_w.astype(f32),
                w_conv_out.astype(bf16), w_o.astype(bf16), row(ln1_g), row(ln1_b))
    halo = (CONV_K - 1) * SUBLANES
    x1 = pl.pallas_call(
        _mixer_kernel,
        grid=(SEQ // T_TILE,),
        in_specs=[pl.BlockSpec(memory_space=pl.ANY)] + [_const_spec(a.shape) for a in mixer_in],
        out_specs=pl.BlockSpec((R_TILE, D_MODEL), lambda i: (i, 0)),
        out_shape=jax.ShapeDtypeStruct((n_rows, D_MODEL), f32),
        scratch_shapes=[pltpu.VMEM((2, T_TILE, BATCH, D_MODEL), f32),
                        pltpu.SemaphoreType.DMA((2, BATCH)),
                        pltpu.VMEM((SUBLANES, N_STATE), f32), pltpu.VMEM((SUBLANES, N_STATE), f32),
                        pltpu.VMEM((R_TILE, N_STATE), f32), pltpu.VMEM((R_TILE, N_STATE), f32),
                        pltpu.VMEM((R_TILE + halo, CONV_WIDTH), f32)],
        compiler_params=pltpu.CompilerParams(dimension_semantics=("arbitrary",),
                                             vmem_limit_bytes=VMEM_LIMIT_BYTES),
        name="mixer",
    )(x, *mixer_in)

    ffn_in = (w_gate.astype(bf16), w_up.astype(bf16), w_down.astype(bf16), row(ln2_g), row(ln2_b))
    return pl.pallas_call(
        _ffn_kernel,
        grid=(FFN_N_TILES,),
        in_specs=[pl.BlockSpec((FFN_ROWS, D_MODEL), lambda i: (i, 0))]
                 + [_const_spec(a.shape) for a in ffn_in],
        out_specs=pl.BlockSpec(memory_space=pl.ANY),
        out_shape=jax.ShapeDtypeStruct((BATCH, SEQ, D_MODEL), f32),
        scratch_shapes=[pltpu.VMEM((2, FFN_T_TILE, BATCH, D_MODEL), f32),
                        pltpu.SemaphoreType.DMA((2, BATCH))],
        compiler_params=pltpu.CompilerParams(dimension_semantics=("arbitrary",),
                                             vmem_limit_bytes=VMEM_LIMIT_BYTES),
        name="ffn",
    )(x1, *ffn_in)


def kernel(x, w_in, b_in, ssm_lambda_re, ssm_lambda_im, ssm_log_dt, ssm_b_re, ssm_b_im, ssm_c_re, ssm_c_im, ssm_d, glu_w, glu_b, w_ssm_out, conv_w, w_conv_out, w_o, ln1_g, ln1_b, w_gate, w_up, w_down, ln2_g, ln2_b):
    assert x.shape == (BATCH, SEQ, D_MODEL) and BATCH == SUBLANES
    for l in range(w_in.shape[0]):
        x = _layer(x, w_in[l], b_in[l], ssm_lambda_re[l], ssm_lambda_im[l], ssm_log_dt[l],
                   ssm_b_re[l], ssm_b_im[l], ssm_c_re[l], ssm_c_im[l], ssm_d[l], glu_w[l],
                   glu_b[l], w_ssm_out[l], conv_w[l], w_conv_out[l], w_o[l], ln1_g[l], ln1_b[l],
                   w_gate[l], w_up[l], w_down[l], ln2_g[l], ln2_b[l])
    return x
```

```python
import jax
import jax.numpy as jnp
from jax import lax
from jax.experimental import pallas as pl
from jax.experimental.pallas import tpu as pltpu

D_MODEL = 1024
BATCH = 8
SEQ = 4096
SSM_WIDTH = D_MODEL // 2
SSM_GROUP = 16
SSM_GROUPS = SSM_WIDTH // SSM_GROUP
SSM_STATE = 64
CONV_WIDTH = D_MODEL // 2
CONV_K = 3
FFN_HIDDEN = 2816
IN_COLS = SSM_WIDTH + 3 * CONV_WIDTH + 2 * D_MODEL
DEPTH = 1
ALPHA = (2.0 * DEPTH) ** 0.25
LN_EPS = 1e-5

SUBLANES = 8
VMEM_LIMIT_BYTES = 56 * 1024 * 1024

GROUPS_PER_BLOCK = 8
N_BLOCKS = SSM_GROUPS // GROUPS_PER_BLOCK
BLOCK_CH = GROUPS_PER_BLOCK * SSM_GROUP
BLOCK_ST = GROUPS_PER_BLOCK * SSM_STATE
N_STATE = SSM_GROUPS * SSM_STATE

PHASES = 2
T_TILE = 64
R_TILE = T_TILE * BATCH
H_TILE = T_TILE // PHASES
N_TILES = SEQ // T_TILE

f32 = jnp.float32
bf16 = jnp.bfloat16


def _layer_norm(y, g, b):
    mu = jnp.mean(y, axis=-1, keepdims=True)
    var = jnp.mean(jnp.square(y - mu), axis=-1, keepdims=True)
    return (y - mu) * lax.rsqrt(var + LN_EPS) * g + b


def _tile_copies(hbm, buf, sem, tile, slot, to_vmem):
    copies = []
    for ph in range(PHASES):
        for b in range(BATCH):
            h = hbm.at[b, pl.ds(tile * H_TILE, H_TILE), ph, :]
            v = buf.at[slot, pl.ds(ph * H_TILE, H_TILE), b, :]
            src, dst = (h, v) if to_vmem else (v, h)
            copies.append(pltpu.make_async_copy(src, dst, sem.at[slot, ph, b]))
    return copies


def _mixer_kernel(x_hbm, win_ref, bin_ref, bmat_ref, a2re_ref, a2im_ref, wre_ref, wim_ref, kd_ref,
                  dskip_ref, gluw_ref, glub_ref, wssm_ref, convw_ref, wconv_ref, wo_ref,
                  g_ref, b_ref, o_ref, xbuf, xsem, xs_re, xs_im, ve_ext, vo_ext):
    rows = R_TILE
    half = rows // PHASES
    i = pl.program_id(0)
    slot = i % 2

    @pl.when(i == 0)
    def _():
        for cp in _tile_copies(x_hbm, xbuf, xsem, 0, 0, True):
            cp.start()
        zeros = jnp.zeros((SUBLANES, N_STATE), f32)
        xs_re[0:SUBLANES, :] = zeros
        xs_im[0:SUBLANES, :] = zeros
        ve_ext[0:SUBLANES, :] = jnp.zeros((SUBLANES, CONV_WIDTH), f32)
        vo_ext[0:SUBLANES, :] = jnp.zeros((SUBLANES, CONV_WIDTH), f32)

    @pl.when(i + 1 < pl.num_programs(0))
    def _():
        for cp in _tile_copies(x_hbm, xbuf, xsem, i + 1, 1 - slot, True):
            cp.start()

    for cp in _tile_copies(x_hbm, xbuf, xsem, i, slot, True):
        cp.wait()

    x = xbuf[slot].reshape(rows, D_MODEL)
    xb = x.astype(bf16)

    def proj(lo, hi):
        return jnp.dot(xb, win_ref[:, lo:hi], preferred_element_type=f32) + bin_ref[:, lo:hi]

    u = proj(0, SSM_WIDTH)
    ub = u.astype(bf16)
    u2 = []
    for j in range(N_BLOCKS):
        cs = slice(j * BLOCK_CH, (j + 1) * BLOCK_CH)
        u2.append(jnp.concatenate([ub[:half, cs], ub[half:, cs]], axis=-1))
        bu = jnp.dot(u2[j], bmat_ref[j], preferred_element_type=f32)
        xs_re[SUBLANES:SUBLANES + half, j * BLOCK_ST:(j + 1) * BLOCK_ST] = bu[:, :BLOCK_ST]
        xs_im[SUBLANES:SUBLANES + half, j * BLOCK_ST:(j + 1) * BLOCK_ST] = bu[:, BLOCK_ST:]

    for j in range(N_BLOCKS):
        sl = pl.ds(j * BLOCK_ST, BLOCK_ST)
        ar = jnp.broadcast_to(a2re_ref[:, sl], (SUBLANES, BLOCK_ST))
        ai = jnp.broadcast_to(a2im_ref[:, sl], (SUBLANES, BLOCK_ST))

        def body(k, carry, sl=sl, ar=ar, ai=ai):
            sr, si = carry
            r0 = pl.multiple_of((k + 1) * SUBLANES, SUBLANES)
            nr = ar * sr - ai * si + xs_re[pl.ds(r0, SUBLANES), sl]
            ni = ar * si + ai * sr + xs_im[pl.ds(r0, SUBLANES), sl]
            xs_re[pl.ds(r0, SUBLANES), sl] = nr
            xs_im[pl.ds(r0, SUBLANES), sl] = ni
            return nr, ni

        lax.fori_loop(0, H_TILE, body, (xs_re[0:SUBLANES, sl], xs_im[0:SUBLANES, sl]), unroll=8)

    y_e, y_o = [], []
    for j in range(N_BLOCKS):
        sl = slice(j * BLOCK_ST, (j + 1) * BLOCK_ST)
        y2 = (jnp.dot(xs_re[0:half, sl].astype(bf16), wre_ref[j], preferred_element_type=f32)
              + jnp.dot(xs_im[0:half, sl].astype(bf16), wim_ref[j], preferred_element_type=f32)
              + jnp.dot(u2[j], kd_ref[j], preferred_element_type=f32))
        y_e.append(y2[:, :BLOCK_CH])
        y_o.append(y2[:, BLOCK_CH:])
    xs_re[0:SUBLANES, :] = xs_re[half:half + SUBLANES, :]
    xs_im[0:SUBLANES, :] = xs_im[half:half + SUBLANES, :]
    y_a = jnp.concatenate([jnp.concatenate(y_e, axis=-1), jnp.concatenate(y_o, axis=-1)], axis=0)
    y_a = y_a + dskip_ref[...] * u
    g = jax.nn.gelu(y_a)
    y_a = g * jax.nn.sigmoid(
        jnp.dot(g.astype(bf16), gluw_ref[...], preferred_element_type=f32) + glub_ref[...])
    y_a = jnp.dot(y_a.astype(bf16), wssm_ref[...], preferred_element_type=f32)

    h = proj(SSM_WIDTH, SSM_WIDTH + CONV_WIDTH)
    c_gate = proj(SSM_WIDTH + CONV_WIDTH, SSM_WIDTH + 2 * CONV_WIDTH)
    v = c_gate * h
    ve_ext[SUBLANES:SUBLANES + half, :] = v[:half]
    vo_ext[SUBLANES:SUBLANES + half, :] = v[half:]
    w0, w1, w2 = convw_ref[0:1, :], convw_ref[1:2, :], convw_ref[2:3, :]
    z_e = w0 * ve_ext[0:half, :] + w1 * vo_ext[0:half, :] + w2 * v[:half]
    z_o = w0 * vo_ext[0:half, :] + w1 * v[:half] + w2 * v[half:]
    ve_ext[0:SUBLANES, :] = ve_ext[half:half + SUBLANES, :]
    vo_ext[0:SUBLANES, :] = vo_ext[half:half + SUBLANES, :]
    z = jnp.concatenate([z_e, z_o], axis=0)
    b_gate = proj(SSM_WIDTH + 2 * CONV_WIDTH, SSM_WIDTH + 3 * CONV_WIDTH)
    y_b = jnp.dot((b_gate * z).astype(bf16), wconv_ref[...], preferred_element_type=f32)

    o4 = SSM_WIDTH + 3 * CONV_WIDTH
    merged = (jax.nn.sigmoid(proj(o4, o4 + D_MODEL)) * y_a
              + jax.nn.sigmoid(proj(o4 + D_MODEL, o4 + 2 * D_MODEL)) * y_b)
    mix = jnp.dot(merged.astype(bf16), wo_ref[...], preferred_element_type=f32)
    o_ref[...] = _layer_norm(ALPHA * x + mix, g_ref[...], b_ref[...])


def _ffn_kernel(x_ref, wg_ref, wu_ref, wd_ref, g_ref, b_ref, o_hbm, obuf, osem):
    i = pl.program_id(0)
    last = pl.num_programs(0) - 1
    slot = i % 2
    x = x_ref[...]
    xb = x.astype(bf16)
    gate = jnp.dot(xb, wg_ref[...], preferred_element_type=f32)
    up = jnp.dot(xb, wu_ref[...], preferred_element_type=f32)
    hid = (jax.nn.silu(gate) * up).astype(bf16)
    ffn = jnp.dot(hid, wd_ref[...], preferred_element_type=f32)
    y = _layer_norm(ALPHA * x + ffn, g_ref[...], b_ref[...])

    @pl.when(i >= 2)
    def _():
        for cp in _tile_copies(o_hbm, obuf, osem, i - 2, slot, False):
            cp.wait()

    obuf[slot] = y.reshape(T_TILE, BATCH, D_MODEL)
    for cp in _tile_copies(o_hbm, obuf, osem, i, slot, False):
        cp.start()

    @pl.when(i == last)
    def _():
        if N_TILES >= 2:
            for cp in _tile_copies(o_hbm, obuf, osem, i - 1, 1 - slot, False):
                cp.wait()
        for cp in _tile_copies(o_hbm, obuf, osem, i, slot, False):
            cp.wait()


def _const_spec(shape):
    nd = len(shape)
    return pl.BlockSpec(shape, lambda i, nd=nd: (0,) * nd, pipeline_mode=pl.Buffered(1))


def _ssm_params(lam_re, lam_im, log_dt, b_re, b_im, c_re, c_im):
    dt = jnp.exp(log_dt)[:, None]
    mag = jnp.exp(lam_re * dt)
    a_re = mag * jnp.cos(lam_im * dt)
    a_im = mag * jnp.sin(lam_im * dt)
    den = lam_re * lam_re + lam_im * lam_im
    num_re = a_re - 1.0
    fr = (num_re * lam_re + a_im * lam_im) / den
    fi = (a_im * lam_re - num_re * lam_im) / den
    bb_re = fr[..., None] * b_re - fi[..., None] * b_im
    bb_im = fr[..., None] * b_im + fi[..., None] * b_re
    a2_re = a_re * a_re - a_im * a_im
    a2_im = 2.0 * a_re * a_im
    abb_re = a_re[..., None] * bb_re - a_im[..., None] * bb_im
    abb_im = a_re[..., None] * bb_im + a_im[..., None] * bb_re
    ca_re = c_re * a_re[:, None, :] - c_im * a_im[:, None, :]
    ca_im = c_re * a_im[:, None, :] + c_im * a_re[:, None, :]
    ca2_re = c_re * a2_re[:, None, :] - c_im * a2_im[:, None, :]
    ca2_im = c_re * a2_im[:, None, :] + c_im * a2_re[:, None, :]
    hi = lax.Precision.HIGHEST
    k0 = (jnp.einsum('gop,gpc->goc', c_re, bb_re, precision=hi)
          - jnp.einsum('gop,gpc->goc', c_im, bb_im, precision=hi))
    k1 = (jnp.einsum('gop,gpc->goc', c_re, abb_re, precision=hi)
          - jnp.einsum('gop,gpc->goc', c_im, abb_im, precision=hi))
    eye = jnp.eye(GROUPS_PER_BLOCK, dtype=f32)

    def block_diag(m, rows, cols):
        m = m.reshape(N_BLOCKS, GROUPS_PER_BLOCK, rows, 1, cols)
        m = m * eye[None, :, None, :, None]
        return m.reshape(N_BLOCKS, GROUPS_PER_BLOCK * rows, GROUPS_PER_BLOCK * cols)

    def in_block(m):
        return block_diag(jnp.swapaxes(m, 1, 2), SSM_GROUP, SSM_STATE)

    def out_block(m):
        return block_diag(jnp.swapaxes(m, 1, 2), SSM_STATE, SSM_GROUP)

    def direct_block(m):
        return block_diag(jnp.swapaxes(m, 1, 2), SSM_GROUP, SSM_GROUP)

    bmat = jnp.concatenate(
        [jnp.concatenate([in_block(abb_re), in_block(abb_im)], axis=-1),
         jnp.concatenate([in_block(bb_re), in_block(bb_im)], axis=-1)],
        axis=1).astype(bf16)
    w_re = jnp.concatenate([out_block(ca_re), out_block(ca2_re)], axis=-1).astype(bf16)
    w_im = jnp.concatenate([out_block(-ca_im), out_block(-ca2_im)], axis=-1).astype(bf16)
    kd = jnp.concatenate(
        [jnp.concatenate([direct_block(k0), direct_block(k1)], axis=-1),
         jnp.concatenate([jnp.zeros_like(direct_block(k0)), direct_block(k0)], axis=-1)],
        axis=1).astype(bf16)
    return bmat, a2_re.reshape(1, N_STATE), a2_im.reshape(1, N_STATE), w_re, w_im, kd


def _layer(x, w_in, b_in, lam_re, lam_im, log_dt, b_re, b_im, c_re, c_im, d_skip, glu_w, glu_b,
           w_ssm_out, conv_w, w_conv_out, w_o, ln1_g, ln1_b, w_gate, w_up, w_down, ln2_g, ln2_b):
    n_rows = SEQ * BATCH
    x4 = x.reshape(BATCH, SEQ // PHASES, PHASES, D_MODEL)
    row = lambda v: v.reshape(1, -1).astype(f32)
    mixer_in = (w_in.astype(bf16), row(b_in),
                *_ssm_params(lam_re, lam_im, log_dt, b_re, b_im, c_re, c_im), row(d_skip),
                glu_w.astype(bf16), row(glu_b), w_ssm_out.astype(bf16), conv_w.astype(f32),
                w_conv_out.astype(bf16), w_o.astype(bf16), row(ln1_g), row(ln1_b))
    half = R_TILE // PHASES
    x1 = pl.pallas_call(
        _mixer_kernel,
        grid=(N_TILES,),
        in_specs=[pl.BlockSpec(memory_space=pl.ANY)] + [_const_spec(a.shape) for a in mixer_in],
        out_specs=pl.BlockSpec((R_TILE, D_MODEL), lambda i: (i, 0)),
        out_shape=jax.ShapeDtypeStruct((n_rows, D_MODEL), f32),
        scratch_shapes=[pltpu.VMEM((2, T_TILE, BATCH, D_MODEL), f32),
                        pltpu.SemaphoreType.DMA((2, PHASES, BATCH)),
                        pltpu.VMEM((half + SUBLANES, N_STATE), f32),
                        pltpu.VMEM((half + SUBLANES, N_STATE), f32),
                        pltpu.VMEM((half + SUBLANES, CONV_WIDTH), f32),
                        pltpu.VMEM((half + SUBLANES, CONV_WIDTH), f32)],
        compiler_params=pltpu.CompilerParams(dimension_semantics=("arbitrary",),
                                             vmem_limit_bytes=VMEM_LIMIT_BYTES),
        name="mixer",
    )(x4, *mixer_in)

    ffn_in = (w_gate.astype(bf16), w_up.astype(bf16), w_down.astype(bf16), row(ln2_g), row(ln2_b))
    out4 = pl.pallas_call(
        _ffn_kernel,
        grid=(N_TILES,),
        in_specs=[pl.BlockSpec((R_TILE, D_MODEL), lambda i: (i, 0))]
                 + [_const_spec(a.shape) for a in ffn_in],
        out_specs=pl.BlockSpec(memory_space=pl.ANY),
        out_shape=jax.ShapeDtypeStruct((BATCH, SEQ // PHASES, PHASES, D_MODEL), f32),
        scratch_shapes=[pltpu.VMEM((2, T_TILE, BATCH, D_MODEL), f32),
                        pltpu.SemaphoreType.DMA((2, PHASES, BATCH))],
        compiler_params=pltpu.CompilerParams(dimension_semantics=("arbitrary",),
                                             vmem_limit_bytes=VMEM_LIMIT_BYTES),
        name="ffn",
    )(x1, *ffn_in)
    return out4.reshape(BATCH, SEQ, D_MODEL)


def kernel(x, w_in, b_in, ssm_lambda_re, ssm_lambda_im, ssm_log_dt, ssm_b_re, ssm_b_im, ssm_c_re, ssm_c_im, ssm_d, glu_w, glu_b, w_ssm_out, conv_w, w_conv_out, w_o, ln1_g, ln1_b, w_gate, w_up, w_down, ln2_g, ln2_b):
    assert x.shape == (BATCH, SEQ, D_MODEL) and BATCH == SUBLANES
    for l in range(w_in.shape[0]):
        x = _layer(x, w_in[l], b_in[l], ssm_lambda_re[l], ssm_lambda_im[l], ssm_log_dt[l],
                   ssm_b_re[l], ssm_b_im[l], ssm_c_re[l], ssm_c_im[l], ssm_d[l], glu_w[l],
                   glu_b[l], w_ssm_out[l], conv_w[l], w_conv_out[l], w_o[l], ln1_g[l], ln1_b[l],
                   w_gate[l], w_up[l], w_down[l], ln2_g[l], ln2_b[l])
    return x
```

```python
import jax
import jax.numpy as jnp
from jax import lax
from jax.experimental import pallas as pl
from jax.experimental.pallas import tpu as pltpu

D_MODEL = 1024
BATCH = 8
SEQ = 4096
SSM_WIDTH = D_MODEL // 2
SSM_GROUP = 16
SSM_GROUPS = SSM_WIDTH // SSM_GROUP
SSM_STATE = 64
CONV_WIDTH = D_MODEL // 2
CONV_K = 3
FFN_HIDDEN = 2816
IN_COLS = SSM_WIDTH + 3 * CONV_WIDTH + 2 * D_MODEL
DEPTH = 1
ALPHA = (2.0 * DEPTH) ** 0.25
LN_EPS = 1e-5

SUBLANES = 8
VMEM_LIMIT_BYTES = 56 * 1024 * 1024

GROUPS_PER_BLOCK = 8
N_BLOCKS = SSM_GROUPS // GROUPS_PER_BLOCK
BLOCK_CH = GROUPS_PER_BLOCK * SSM_GROUP
BLOCK_ST = GROUPS_PER_BLOCK * SSM_STATE
N_STATE = SSM_GROUPS * SSM_STATE

PHASES = 2
T_TILE = 64
R_TILE = T_TILE * BATCH
H_TILE = T_TILE // PHASES
N_TILES = SEQ // T_TILE

f32 = jnp.float32
bf16 = jnp.bfloat16


def _layer_norm(y, g, b):
    mu = jnp.mean(y, axis=-1, keepdims=True)
    var = jnp.mean(jnp.square(y - mu), axis=-1, keepdims=True)
    return (y - mu) * lax.rsqrt(var + LN_EPS) * g + b


def _tile_copies(hbm, buf, sem, tile, slot, to_vmem):
    copies = []
    for b in range(BATCH):
        h = hbm.at[b, pl.ds(tile * T_TILE, T_TILE), :]
        v = buf.at[slot, :, b, :]
        src, dst = (h, v) if to_vmem else (v, h)
        copies.append(pltpu.make_async_copy(src, dst, sem.at[slot, b]))
    return copies


def _split_phases(x3):
    x4 = x3.reshape(H_TILE, PHASES, SUBLANES, x3.shape[-1])
    return jnp.concatenate([x4[:, ph].reshape(H_TILE * SUBLANES, x3.shape[-1])
                            for ph in range(PHASES)], axis=0)


def _merge_phases(y):
    half = H_TILE * SUBLANES
    parts = [y[ph * half:(ph + 1) * half].reshape(H_TILE, SUBLANES, y.shape[-1])
             for ph in range(PHASES)]
    return jnp.stack(parts, axis=1).reshape(T_TILE, SUBLANES, y.shape[-1])


def _mixer_kernel(x_hbm, win_ref, bin_ref, bmat_ref, a2re_ref, a2im_ref, wre_ref, wim_ref, kd_ref,
                  dskip_ref, gluw_ref, glub_ref, wssm_ref, convw_ref, wconv_ref, wo_ref,
                  g_ref, b_ref, o_ref, xbuf, xsem, xs_re, xs_im, ve_ext, vo_ext):
    rows = R_TILE
    half = rows // PHASES
    i = pl.program_id(0)
    slot = i % 2

    @pl.when(i == 0)
    def _():
        for cp in _tile_copies(x_hbm, xbuf, xsem, 0, 0, True):
            cp.start()
        zeros = jnp.zeros((SUBLANES, N_STATE), f32)
        xs_re[0:SUBLANES, :] = zeros
        xs_im[0:SUBLANES, :] = zeros
        ve_ext[0:SUBLANES, :] = jnp.zeros((SUBLANES, CONV_WIDTH), f32)
        vo_ext[0:SUBLANES, :] = jnp.zeros((SUBLANES, CONV_WIDTH), f32)

    @pl.when(i + 1 < pl.num_programs(0))
    def _():
        for cp in _tile_copies(x_hbm, xbuf, xsem, i + 1, 1 - slot, True):
            cp.start()

    for cp in _tile_copies(x_hbm, xbuf, xsem, i, slot, True):
        cp.wait()

    x = _split_phases(xbuf[slot])
    xb = x.astype(bf16)

    def proj(lo, hi):
        return jnp.dot(xb, win_ref[:, lo:hi], preferred_element_type=f32) + bin_ref[:, lo:hi]

    u = proj(0, SSM_WIDTH)
    ub = u.astype(bf16)
    u2 = []
    for j in range(N_BLOCKS):
        cs = slice(j * BLOCK_CH, (j + 1) * BLOCK_CH)
        u2.append(jnp.concatenate([ub[:half, cs], ub[half:, cs]], axis=-1))
        bu = jnp.dot(u2[j], bmat_ref[j], preferred_element_type=f32)
        xs_re[SUBLANES:SUBLANES + half, j * BLOCK_ST:(j + 1) * BLOCK_ST] = bu[:, :BLOCK_ST]
        xs_im[SUBLANES:SUBLANES + half, j * BLOCK_ST:(j + 1) * BLOCK_ST] = bu[:, BLOCK_ST:]

    for j in range(N_BLOCKS):
        sl = pl.ds(j * BLOCK_ST, BLOCK_ST)
        ar = jnp.broadcast_to(a2re_ref[:, sl], (SUBLANES, BLOCK_ST))
        ai = jnp.broadcast_to(a2im_ref[:, sl], (SUBLANES, BLOCK_ST))

        def body(k, carry, sl=sl, ar=ar, ai=ai):
            sr, si = carry
            r0 = pl.multiple_of((k + 1) * SUBLANES, SUBLANES)
            nr = ar * sr - ai * si + xs_re[pl.ds(r0, SUBLANES), sl]
            ni = ar * si + ai * sr + xs_im[pl.ds(r0, SUBLANES), sl]
            xs_re[pl.ds(r0, SUBLANES), sl] = nr
            xs_im[pl.ds(r0, SUBLANES), sl] = ni
            return nr, ni

        lax.fori_loop(0, H_TILE, body, (xs_re[0:SUBLANES, sl], xs_im[0:SUBLANES, sl]), unroll=8)

    y_e, y_o = [], []
    for j in range(N_BLOCKS):
        sl = slice(j * BLOCK_ST, (j + 1) * BLOCK_ST)
        y2 = (jnp.dot(xs_re[0:half, sl].astype(bf16), wre_ref[j], preferred_element_type=f32)
              + jnp.dot(xs_im[0:half, sl].astype(bf16), wim_ref[j], preferred_element_type=f32)
              + jnp.dot(u2[j], kd_ref[j], preferred_element_type=f32))
        y_e.append(y2[:, :BLOCK_CH])
        y_o.append(y2[:, BLOCK_CH:])
    xs_re[0:SUBLANES, :] = xs_re[half:half + SUBLANES, :]
    xs_im[0:SUBLANES, :] = xs_im[half:half + SUBLANES, :]
    y_a = jnp.concatenate([jnp.concatenate(y_e, axis=-1), jnp.concatenate(y_o, axis=-1)], axis=0)
    y_a = y_a + dskip_ref[...] * u
    g = jax.nn.gelu(y_a)
    y_a = g * jax.nn.sigmoid(
        jnp.dot(g.astype(bf16), gluw_ref[...], preferred_element_type=f32) + glub_ref[...])
    y_a = jnp.dot(y_a.astype(bf16), wssm_ref[...], preferred_element_type=f32)

    h = proj(SSM_WIDTH, SSM_WIDTH + CONV_WIDTH)
    c_gate = proj(SSM_WIDTH + CONV_WIDTH, SSM_WIDTH + 2 * CONV_WIDTH)
    v = c_gate * h
    ve_ext[SUBLANES:SUBLANES + half, :] = v[:half]
    vo_ext[SUBLANES:SUBLANES + half, :] = v[half:]
    w0, w1, w2 = convw_ref[0:1, :], convw_ref[1:2, :], convw_ref[2:3, :]
    z_e = w0 * ve_ext[0:half, :] + w1 * vo_ext[0:half, :] + w2 * v[:half]
    z_o = w0 * vo_ext[0:half, :] + w1 * v[:half] + w2 * v[half:]
    ve_ext[0:SUBLANES, :] = ve_ext[half:half + SUBLANES, :]
    vo_ext[0:SUBLANES, :] = vo_ext[half:half + SUBLANES, :]
    z = jnp.concatenate([z_e, z_o], axis=0)
    b_gate = proj(SSM_WIDTH + 2 * CONV_WIDTH, SSM_WIDTH + 3 * CONV_WIDTH)
    y_b = jnp.dot((b_gate * z).astype(bf16), wconv_ref[...], preferred_element_type=f32)

    o4 = SSM_WIDTH + 3 * CONV_WIDTH
    merged = (jax.nn.sigmoid(proj(o4, o4 + D_MODEL)) * y_a
              + jax.nn.sigmoid(proj(o4 + D_MODEL, o4 + 2 * D_MODEL)) * y_b)
    mix = jnp.dot(merged.astype(bf16), wo_ref[...], preferred_element_type=f32)
    o_ref[...] = _layer_norm(ALPHA * x + mix, g_ref[...], b_ref[...])


def _ffn_kernel(x_ref, wg_ref, wu_ref, wd_ref, g_ref, b_ref, o_hbm, obuf, osem):
    i = pl.program_id(0)
    last = pl.num_programs(0) - 1
    slot = i % 2
    x = x_ref[...]
    xb = x.astype(bf16)
    gate = jnp.dot(xb, wg_ref[...], preferred_element_type=f32)
    up = jnp.dot(xb, wu_ref[...], preferred_element_type=f32)
    hid = (jax.nn.silu(gate) * up).astype(bf16)
    ffn = jnp.dot(hid, wd_ref[...], preferred_element_type=f32)
    y = _layer_norm(ALPHA * x + ffn, g_ref[...], b_ref[...])

    @pl.when(i >= 2)
    def _():
        for cp in _tile_copies(o_hbm, obuf, osem, i - 2, slot, False):
            cp.wait()

    obuf[slot] = _merge_phases(y)
    for cp in _tile_copies(o_hbm, obuf, osem, i, slot, False):
        cp.start()

    @pl.when(i == last)
    def _():
        if N_TILES >= 2:
            for cp in _tile_copies(o_hbm, obuf, osem, i - 1, 1 - slot, False):
                cp.wait()
        for cp in _tile_copies(o_hbm, obuf, osem, i, slot, False):
            cp.wait()


def _const_spec(shape):
    nd = len(shape)
    return pl.BlockSpec(shape, lambda i, nd=nd: (0,) * nd, pipeline_mode=pl.Buffered(1))


def _ssm_params(lam_re, lam_im, log_dt, b_re, b_im, c_re, c_im):
    dt = jnp.exp(log_dt)[:, None]
    mag = jnp.exp(lam_re * dt)
    a_re = mag * jnp.cos(lam_im * dt)
    a_im = mag * jnp.sin(lam_im * dt)
    den = lam_re * lam_re + lam_im * lam_im
    num_re = a_re - 1.0
    fr = (num_re * lam_re + a_im * lam_im) / den
    fi = (a_im * lam_re - num_re * lam_im) / den
    bb_re = fr[..., None] * b_re - fi[..., None] * b_im
    bb_im = fr[..., None] * b_im + fi[..., None] * b_re
    a2_re = a_re * a_re - a_im * a_im
    a2_im = 2.0 * a_re * a_im
    abb_re = a_re[..., None] * bb_re - a_im[..., None] * bb_im
    abb_im = a_re[..., None] * bb_im + a_im[..., None] * bb_re
    ca_re = c_re * a_re[:, None, :] - c_im * a_im[:, None, :]
    ca_im = c_re * a_im[:, None, :] + c_im * a_re[:, None, :]
    ca2_re = c_re * a2_re[:, None, :] - c_im * a2_im[:, None, :]
    ca2_im = c_re * a2_im[:, None, :] + c_im * a2_re[:, None, :]
    hi = lax.Precision.HIGHEST
    k0 = (jnp.einsum('gop,gpc->goc', c_re, bb_re, precision=hi)
          - jnp.einsum('gop,gpc->goc', c_im, bb_im, precision=hi))
    k1 = (jnp.einsum('gop,gpc->goc', c_re, abb_re, precision=hi)
          - jnp.einsum('gop,gpc->goc', c_im, abb_im, precision=hi))
    eye = jnp.eye(GROUPS_PER_BLOCK, dtype=f32)

    def block_diag(m, rows, cols):
        m = m.reshape(N_BLOCKS, GROUPS_PER_BLOCK, rows, 1, cols)
        m = m * eye[None, :, None, :, None]
        return m.reshape(N_BLOCKS, GROUPS_PER_BLOCK * rows, GROUPS_PER_BLOCK * cols)

    def in_block(m):
        return block_diag(jnp.swapaxes(m, 1, 2), SSM_GROUP, SSM_STATE)

    def out_block(m):
        return block_diag(jnp.swapaxes(m, 1, 2), SSM_STATE, SSM_GROUP)

    def direct_block(m):
        return block_diag(jnp.swapaxes(m, 1, 2), SSM_GROUP, SSM_GROUP)

    bmat = jnp.concatenate(
        [jnp.concatenate([in_block(abb_re), in_block(abb_im)], axis=-1),
         jnp.concatenate([in_block(bb_re), in_block(bb_im)], axis=-1)],
        axis=1).astype(bf16)
    w_re = jnp.concatenate([out_block(ca_re), out_block(ca2_re)], axis=-1).astype(bf16)
    w_im = jnp.concatenate([out_block(-ca_im), out_block(-ca2_im)], axis=-1).astype(bf16)
    kd = jnp.concatenate(
        [jnp.concatenate([direct_block(k0), direct_block(k1)], axis=-1),
         jnp.concatenate([jnp.zeros_like(direct_block(k0)), direct_block(k0)], axis=-1)],
        axis=1).astype(bf16)
    return bmat, a2_re.reshape(1, N_STATE), a2_im.reshape(1, N_STATE), w_re, w_im, kd


def _layer(x, w_in, b_in, lam_re, lam_im, log_dt, b_re, b_im, c_re, c_im, d_skip, glu_w, glu_b,
           w_ssm_out, conv_w, w_conv_out, w_o, ln1_g, ln1_b, w_gate, w_up, w_down, ln2_g, ln2_b):
    n_rows = SEQ * BATCH
    row = lambda v: v.reshape(1, -1).astype(f32)
    mixer_in = (w_in.astype(bf16), row(b_in),
                *_ssm_params(lam_re, lam_im, log_dt, b_re, b_im, c_re, c_im), row(d_skip),
                glu_w.astype(bf16), row(glu_b), w_ssm_out.astype(bf16), conv_w.astype(f32),
                w_conv_out.astype(bf16), w_o.astype(bf16), row(ln1_g), row(ln1_b))
    half = R_TILE // PHASES
    x1 = pl.pallas_call(
        _mixer_kernel,
        grid=(N_TILES,),
        in_specs=[pl.BlockSpec(memory_space=pl.ANY)] + [_const_spec(a.shape) for a in mixer_in],
        out_specs=pl.BlockSpec((R_TILE, D_MODEL), lambda i: (i, 0)),
        out_shape=jax.ShapeDtypeStruct((n_rows, D_MODEL), f32),
        scratch_shapes=[pltpu.VMEM((2, T_TILE, BATCH, D_MODEL), f32),
                        pltpu.SemaphoreType.DMA((2, BATCH)),
                        pltpu.VMEM((half + SUBLANES, N_STATE), f32),
                        pltpu.VMEM((half + SUBLANES, N_STATE), f32),
                        pltpu.VMEM((half + SUBLANES, CONV_WIDTH), f32),
                        pltpu.VMEM((half + SUBLANES, CONV_WIDTH), f32)],
        compiler_params=pltpu.CompilerParams(dimension_semantics=("arbitrary",),
                                             vmem_limit_bytes=VMEM_LIMIT_BYTES),
        name="mixer",
    )(x, *mixer_in)

    ffn_in = (w_gate.astype(bf16), w_up.astype(bf16), w_down.astype(bf16), row(ln2_g), row(ln2_b))
    return pl.pallas_call(
        _ffn_kernel,
        grid=(N_TILES,),
        in_specs=[pl.BlockSpec((R_TILE, D_MODEL), lambda i: (i, 0))]
                 + [_const_spec(a.shape) for a in ffn_in],
        out_specs=pl.BlockSpec(memory_space=pl.ANY),
        out_shape=jax.ShapeDtypeStruct((BATCH, SEQ, D_MODEL), f32),
        scratch_shapes=[pltpu.VMEM((2, T_TILE, BATCH, D_MODEL), f32),
                        pltpu.SemaphoreType.DMA((2, BATCH))],
        compiler_params=pltpu.CompilerParams(dimension_semantics=("arbitrary",),
                                             vmem_limit_bytes=VMEM_LIMIT_BYTES),
        name="ffn",
    )(x1, *ffn_in)


def kernel(x, w_in, b_in, ssm_lambda_re, ssm_lambda_im, ssm_log_dt, ssm_b_re, ssm_b_im, ssm_c_re, ssm_c_im, ssm_d, glu_w, glu_b, w_ssm_out, conv_w, w_conv_out, w_o, ln1_g, ln1_b, w_gate, w_up, w_down, ln2_g, ln2_b):
    assert x.shape == (BATCH, SEQ, D_MODEL) and BATCH == SUBLANES
    for l in range(w_in.shape[0]):
        x = _layer(x, w_in[l], b_in[l], ssm_lambda_re[l], ssm_lambda_im[l], ssm_log_dt[l],
                   ssm_b_re[l], ssm_b_im[l], ssm_c_re[l], ssm_c_im[l], ssm_d[l], glu_w[l],
                   glu_b[l], w_ssm_out[l], conv_w[l], w_conv_out[l], w_o[l], ln1_g[l], ln1_b[l],
                   w_gate[l], w_up[l], w_down[l], ln2_g[l], ln2_b[l])
    return x
```

```python
import jax
import jax.numpy as jnp
from jax import lax
from jax.experimental import pallas as pl
from jax.experimental.pallas import tpu as pltpu

D_MODEL = 1024
BATCH = 8
SEQ = 4096
SSM_WIDTH = D_MODEL // 2
SSM_GROUP = 16
SSM_GROUPS = SSM_WIDTH // SSM_GROUP
SSM_STATE = 64
CONV_WIDTH = D_MODEL // 2
CONV_K = 3
FFN_HIDDEN = 2816
IN_COLS = SSM_WIDTH + 3 * CONV_WIDTH + 2 * D_MODEL
DEPTH = 1
ALPHA = (2.0 * DEPTH) ** 0.25
LN_EPS = 1e-5

SUBLANES = 8
VMEM_LIMIT_BYTES = 56 * 1024 * 1024

GROUPS_PER_BLOCK = 8
N_BLOCKS = SSM_GROUPS // GROUPS_PER_BLOCK
BLOCK_CH = GROUPS_PER_BLOCK * SSM_GROUP
BLOCK_ST = GROUPS_PER_BLOCK * SSM_STATE
N_STATE = SSM_GROUPS * SSM_STATE

PHASES = 2
T_TILE = 64
R_TILE = T_TILE * BATCH
H_TILE = T_TILE // PHASES
N_TILES = SEQ // T_TILE

f32 = jnp.float32
bf16 = jnp.bfloat16


def _layer_norm(y, g, b):
    mu = jnp.mean(y, axis=-1, keepdims=True)
    var = jnp.mean(jnp.square(y - mu), axis=-1, keepdims=True)
    return (y - mu) * lax.rsqrt(var + LN_EPS) * g + b


def _tile_copies(hbm, buf, sem, tile, slot, to_vmem):
    copies = []
    for b in range(BATCH):
        h = hbm.at[b, pl.ds(tile * T_TILE, T_TILE), :]
        v = buf.at[slot, :, b, :]
        src, dst = (h, v) if to_vmem else (v, h)
        copies.append(pltpu.make_async_copy(src, dst, sem.at[slot, b]))
    return copies


def _split_phases(x3):
    x4 = x3.reshape(H_TILE, PHASES, SUBLANES, x3.shape[-1])
    return jnp.concatenate([x4[:, ph].reshape(H_TILE * SUBLANES, x3.shape[-1])
                            for ph in range(PHASES)], axis=0)


def _merge_phases(y):
    half = H_TILE * SUBLANES
    parts = [y[ph * half:(ph + 1) * half].reshape(H_TILE, SUBLANES, y.shape[-1])
             for ph in range(PHASES)]
    return jnp.stack(parts, axis=1).reshape(T_TILE, SUBLANES, y.shape[-1])


def _mixer_kernel(x_hbm, win_ref, bin_ref, bmat_ref, a2re_ref, a2im_ref, wre_ref, wim_ref, kd_ref,
                  dskip_ref, gluw_ref, glub_ref, wssm_ref, convw_ref, wconv_ref, wo_ref,
                  g_ref, b_ref, o_ref, xbuf, xsem, xs_re, xs_im, ve_ext, vo_ext):
    rows = R_TILE
    half = rows // PHASES
    i = pl.program_id(0)
    slot = i % 2

    @pl.when(i == 0)
    def _():
        for cp in _tile_copies(x_hbm, xbuf, xsem, 0, 0, True):
            cp.start()
        zeros = jnp.zeros((SUBLANES, N_STATE), f32)
        xs_re[0:SUBLANES, :] = zeros
        xs_im[0:SUBLANES, :] = zeros
        ve_ext[0:SUBLANES, :] = jnp.zeros((SUBLANES, CONV_WIDTH), f32)
        vo_ext[0:SUBLANES, :] = jnp.zeros((SUBLANES, CONV_WIDTH), f32)

    @pl.when(i + 1 < pl.num_programs(0))
    def _():
        for cp in _tile_copies(x_hbm, xbuf, xsem, i + 1, 1 - slot, True):
            cp.start()

    for cp in _tile_copies(x_hbm, xbuf, xsem, i, slot, True):
        cp.wait()

    x = _split_phases(xbuf[slot])
    xb = x.astype(bf16)

    def proj(lo, hi):
        return jnp.dot(xb, win_ref[:, lo:hi], preferred_element_type=f32) + bin_ref[:, lo:hi]

    u = proj(0, SSM_WIDTH)
    ub = u.astype(bf16)
    u2 = []
    for j in range(N_BLOCKS):
        cs = slice(j * BLOCK_CH, (j + 1) * BLOCK_CH)
        u2.append(jnp.concatenate([ub[:half, cs], ub[half:, cs]], axis=-1))
        bu = jnp.dot(u2[j], bmat_ref[j], preferred_element_type=f32)
        xs_re[SUBLANES:SUBLANES + half, j * BLOCK_ST:(j + 1) * BLOCK_ST] = bu[:, :BLOCK_ST]
        xs_im[SUBLANES:SUBLANES + half, j * BLOCK_ST:(j + 1) * BLOCK_ST] = bu[:, BLOCK_ST:]

    for j in range(N_BLOCKS):
        sl = pl.ds(j * BLOCK_ST, BLOCK_ST)
        ar = jnp.broadcast_to(a2re_ref[:, sl], (SUBLANES, BLOCK_ST))
        ai = jnp.broadcast_to(a2im_ref[:, sl], (SUBLANES, BLOCK_ST))

        def body(k, carry, sl=sl, ar=ar, ai=ai):
            sr, si = carry
            r0 = pl.multiple_of((k + 1) * SUBLANES, SUBLANES)
            nr = ar * sr - ai * si + xs_re[pl.ds(r0, SUBLANES), sl]
            ni = ar * si + ai * sr + xs_im[pl.ds(r0, SUBLANES), sl]
            xs_re[pl.ds(r0, SUBLANES), sl] = nr
            xs_im[pl.ds(r0, SUBLANES), sl] = ni
            return nr, ni

        lax.fori_loop(0, H_TILE, body, (xs_re[0:SUBLANES, sl], xs_im[0:SUBLANES, sl]), unroll=True)

    y_e, y_o = [], []
    for j in range(N_BLOCKS):
        sl = slice(j * BLOCK_ST, (j + 1) * BLOCK_ST)
        y2 = (jnp.dot(xs_re[0:half, sl].astype(bf16), wre_ref[j], preferred_element_type=f32)
              + jnp.dot(xs_im[0:half, sl].astype(bf16), wim_ref[j], preferred_element_type=f32)
              + jnp.dot(u2[j], kd_ref[j], preferred_element_type=f32))
        y_e.append(y2[:, :BLOCK_CH])
        y_o.append(y2[:, BLOCK_CH:])
    xs_re[0:SUBLANES, :] = xs_re[half:half + SUBLANES, :]
    xs_im[0:SUBLANES, :] = xs_im[half:half + SUBLANES, :]
    y_a = jnp.concatenate([jnp.concatenate(y_e, axis=-1), jnp.concatenate(y_o, axis=-1)], axis=0)
    y_a = y_a + dskip_ref[...] * u
    g = jax.nn.gelu(y_a)
    y_a = g * jax.nn.sigmoid(
        jnp.dot(g.astype(bf16), gluw_ref[...], preferred_element_type=f32) + glub_ref[...])
    y_a = jnp.dot(y_a.astype(bf16), wssm_ref[...], preferred_element_type=f32)

    h = proj(SSM_WIDTH, SSM_WIDTH + CONV_WIDTH)
    c_gate = proj(SSM_WIDTH + CONV_WIDTH, SSM_WIDTH + 2 * CONV_WIDTH)
    v = c_gate * h
    ve_ext[SUBLANES:SUBLANES + half, :] = v[:half]
    vo_ext[SUBLANES:SUBLANES + half, :] = v[half:]
    w0, w1, w2 = convw_ref[0:1, :], convw_ref[1:2, :], convw_ref[2:3, :]
    z_e = w0 * ve_ext[0:half, :] + w1 * vo_ext[0:half, :] + w2 * v[:half]
    z_o = w0 * vo_ext[0:half, :] + w1 * v[:half] + w2 * v[half:]
    ve_ext[0:SUBLANES, :] = ve_ext[half:half + SUBLANES, :]
    vo_ext[0:SUBLANES, :] = vo_ext[half:half + SUBLANES, :]
    z = jnp.concatenate([z_e, z_o], axis=0)
    b_gate = proj(SSM_WIDTH + 2 * CONV_WIDTH, SSM_WIDTH + 3 * CONV_WIDTH)
    y_b = jnp.dot((b_gate * z).astype(bf16), wconv_ref[...], preferred_element_type=f32)

    o4 = SSM_WIDTH + 3 * CONV_WIDTH
    merged = (jax.nn.sigmoid(proj(o4, o4 + D_MODEL)) * y_a
              + jax.nn.sigmoid(proj(o4 + D_MODEL, o4 + 2 * D_MODEL)) * y_b)
    mix = jnp.dot(merged.astype(bf16), wo_ref[...], preferred_element_type=f32)
    o_ref[...] = _layer_norm(ALPHA * x + mix, g_ref[...], b_ref[...])


def _ffn_kernel(x_ref, wg_ref, wu_ref, wd_ref, g_ref, b_ref, o_hbm, obuf, osem):
    i = pl.program_id(0)
    last = pl.num_programs(0) - 1
    slot = i % 2
    x = x_ref[...]
    xb = x.astype(bf16)
    gate = jnp.dot(xb, wg_ref[...], preferred_element_type=f32)
    up = jnp.dot(xb, wu_ref[...], preferred_element_type=f32)
    hid = (jax.nn.silu(gate) * up).astype(bf16)
    ffn = jnp.dot(hid, wd_ref[...], preferred_element_type=f32)
    y = _layer_norm(ALPHA * x + ffn, g_ref[...], b_ref[...])

    @pl.when(i >= 2)
    def _():
        for cp in _tile_copies(o_hbm, obuf, osem, i - 2, slot, False):
            cp.wait()

    obuf[slot] = _merge_phases(y)
    for cp in _tile_copies(o_hbm, obuf, osem, i, slot, False):
        cp.start()

    @pl.when(i == last)
    def _():
        if N_TILES >= 2:
            for cp in _tile_copies(o_hbm, obuf, osem, i - 1, 1 - slot, False):
                cp.wait()
        for cp in _tile_copies(o_hbm, obuf, osem, i, slot, False):
            cp.wait()


def _const_spec(shape):
    nd = len(shape)
    return pl.BlockSpec(shape, lambda i, nd=nd: (0,) * nd, pipeline_mode=pl.Buffered(1))


def _ssm_params(lam_re, lam_im, log_dt, b_re, b_im, c_re, c_im):
    dt = jnp.exp(log_dt)[:, None]
    mag = jnp.exp(lam_re * dt)
    a_re = mag * jnp.cos(lam_im * dt)
    a_im = mag * jnp.sin(lam_im * dt)
    den = lam_re * lam_re + lam_im * lam_im
    num_re = a_re - 1.0
    fr = (num_re * lam_re + a_im * lam_im) / den
    fi = (a_im * lam_re - num_re * lam_im) / den
    bb_re = fr[..., None] * b_re - fi[..., None] * b_im
    bb_im = fr[..., None] * b_im + fi[..., None] * b_re
    a2_re = a_re * a_re - a_im * a_im
    a2_im = 2.0 * a_re * a_im
    abb_re = a_re[..., None] * bb_re - a_im[..., None] * bb_im
    abb_im = a_re[..., None] * bb_im + a_im[..., None] * bb_re
    ca_re = c_re * a_re[:, None, :] - c_im * a_im[:, None, :]
    ca_im = c_re * a_im[:, None, :] + c_im * a_re[:, None, :]
    ca2_re = c_re * a2_re[:, None, :] - c_im * a2_im[:, None, :]
    ca2_im = c_re * a2_im[:, None, :] + c_im * a2_re[:, None, :]
    hi = lax.Precision.HIGHEST
    k0 = (jnp.einsum('gop,gpc->goc', c_re, bb_re, precision=hi)
          - jnp.einsum('gop,gpc->goc', c_im, bb_im, precision=hi))
    k1 = (jnp.einsum('gop,gpc->goc', c_re, abb_re, precision=hi)
          - jnp.einsum('gop,gpc->goc', c_im, abb_im, precision=hi))
    eye = jnp.eye(GROUPS_PER_BLOCK, dtype=f32)

    def block_diag(m, rows, cols):
        m = m.reshape(N_BLOCKS, GROUPS_PER_BLOCK, rows, 1, cols)
        m = m * eye[None, :, None, :, None]
        return m.reshape(N_BLOCKS, GROUPS_PER_BLOCK * rows, GROUPS_PER_BLOCK * cols)

    def in_block(m):
        return block_diag(jnp.swapaxes(m, 1, 2), SSM_GROUP, SSM_STATE)

    def out_block(m):
        return block_diag(jnp.swapaxes(m, 1, 2), SSM_STATE, SSM_GROUP)

    def direct_block(m):
        return block_diag(jnp.swapaxes(m, 1, 2), SSM_GROUP, SSM_GROUP)

    bmat = jnp.concatenate(
        [jnp.concatenate([in_block(abb_re), in_block(abb_im)], axis=-1),
         jnp.concatenate([in_block(bb_re), in_block(bb_im)], axis=-1)],
        axis=1).astype(bf16)
    w_re = jnp.concatenate([out_block(ca_re), out_block(ca2_re)], axis=-1).astype(bf16)
    w_im = jnp.concatenate([out_block(-ca_im), out_block(-ca2_im)], axis=-1).astype(bf16)
    kd = jnp.concatenate(
        [jnp.concatenate([direct_block(k0), direct_block(k1)], axis=-1),
         jnp.concatenate([jnp.zeros_like(direct_block(k0)), direct_block(k0)], axis=-1)],
        axis=1).astype(bf16)
    return bmat, a2_re.reshape(1, N_STATE), a2_im.reshape(1, N_STATE), w_re, w_im, kd


def _layer(x, w_in, b_in, lam_re, lam_im, log_dt, b_re, b_im, c_re, c_im, d_skip, glu_w, glu_b,
           w_ssm_out, conv_w, w_conv_out, w_o, ln1_g, ln1_b, w_gate, w_up, w_down, ln2_g, ln2_b):
    n_rows = SEQ * BATCH
    row = lambda v: v.reshape(1, -1).astype(f32)
    mixer_in = (w_in.astype(bf16), row(b_in),
                *_ssm_params(lam_re, lam_im, log_dt, b_re, b_im, c_re, c_im), row(d_skip),
                glu_w.astype(bf16), row(glu_b), w_ssm_out.astype(bf16), conv_w.astype(f32),
                w_conv_out.astype(bf16), w_o.astype(bf16), row(ln1_g), row(ln1_b))
    half = R_TILE // PHASES
    x1 = pl.pallas_call(
        _mixer_kernel,
        grid=(N_TILES,),
        in_specs=[pl.BlockSpec(memory_space=pl.ANY)] + [_const_spec(a.shape) for a in mixer_in],
        out_specs=pl.BlockSpec((R_TILE, D_MODEL), lambda i: (i, 0)),
        out_shape=jax.ShapeDtypeStruct((n_rows, D_MODEL), f32),
        scratch_shapes=[pltpu.VMEM((2, T_TILE, BATCH, D_MODEL), f32),
                        pltpu.SemaphoreType.DMA((2, BATCH)),
                        pltpu.VMEM((half + SUBLANES, N_STATE), f32),
                        pltpu.VMEM((half + SUBLANES, N_STATE), f32),
                        pltpu.VMEM((half + SUBLANES, CONV_WIDTH), f32),
                        pltpu.VMEM((half + SUBLANES, CONV_WIDTH), f32)],
        compiler_params=pltpu.CompilerParams(dimension_semantics=("arbitrary",),
                                             vmem_limit_bytes=VMEM_LIMIT_BYTES),
        name="mixer",
    )(x, *mixer_in)

    ffn_in = (w_gate.astype(bf16), w_up.astype(bf16), w_down.astype(bf16), row(ln2_g), row(ln2_b))
    return pl.pallas_call(
        _ffn_kernel,
        grid=(N_TILES,),
        in_specs=[pl.BlockSpec((R_TILE, D_MODEL), lambda i: (i, 0))]
                 + [_const_spec(a.shape) for a in ffn_in],
        out_specs=pl.BlockSpec(memory_space=pl.ANY),
        out_shape=jax.ShapeDtypeStruct((BATCH, SEQ, D_MODEL), f32),
        scratch_shapes=[pltpu.VMEM((2, T_TILE, BATCH, D_MODEL), f32),
                        pltpu.SemaphoreType.DMA((2, BATCH))],
        compiler_params=pltpu.CompilerParams(dimension_semantics=("arbitrary",),
                                             vmem_limit_bytes=VMEM_LIMIT_BYTES),
        name="ffn",
    )(x1, *ffn_in)


def kernel(x, w_in, b_in, ssm_lambda_re, ssm_lambda_im, ssm_log_dt, ssm_b_re, ssm_b_im, ssm_c_re, ssm_c_im, ssm_d, glu_w, glu_b, w_ssm_out, conv_w, w_conv_out, w_o, ln1_g, ln1_b, w_gate, w_up, w_down, ln2_g, ln2_b):
    assert x.shape == (BATCH, SEQ, D_MODEL) and BATCH == SUBLANES
    for l in range(w_in.shape[0]):
        x = _layer(x, w_in[l], b_in[l], ssm_lambda_re[l], ssm_lambda_im[l], ssm_log_dt[l],
                   ssm_b_re[l], ssm_b_im[l], ssm_c_re[l], ssm_c_im[l], ssm_d[l], glu_w[l],
                   glu_b[l], w_ssm_out[l], conv_w[l], w_conv_out[l], w_o[l], ln1_g[l], ln1_b[l],
                   w_gate[l], w_up[l], w_down[l], ln2_g[l], ln2_b[l])
    return x
```

```python
import jax
import jax.numpy as jnp
from jax import lax
from jax.experimental import pallas as pl
from jax.experimental.pallas import tpu as pltpu

D_MODEL = 1024
BATCH = 8
SEQ = 4096
SSM_WIDTH = D_MODEL // 2
SSM_GROUP = 16
SSM_GROUPS = SSM_WIDTH // SSM_GROUP
SSM_STATE = 64
CONV_WIDTH = D_MODEL // 2
CONV_K = 3
FFN_HIDDEN = 2816
IN_COLS = SSM_WIDTH + 3 * CONV_WIDTH + 2 * D_MODEL
DEPTH = 1
ALPHA = (2.0 * DEPTH) ** 0.25
LN_EPS = 1e-5

SUBLANES = 8
VMEM_LIMIT_BYTES = 60 * 1024 * 1024

GROUPS_PER_BLOCK = 8
N_BLOCKS = SSM_GROUPS // GROUPS_PER_BLOCK
BLOCK_CH = GROUPS_PER_BLOCK * SSM_GROUP
BLOCK_ST = GROUPS_PER_BLOCK * SSM_STATE
N_STATE = SSM_GROUPS * SSM_STATE

PHASES = 2
T_TILE = 32
R_TILE = T_TILE * BATCH
H_TILE = T_TILE // PHASES
HALF = H_TILE * SUBLANES
N_TILES = SEQ // T_TILE
assert N_TILES >= 2 and SEQ % T_TILE == 0 and T_TILE % PHASES == 0

f32 = jnp.float32
bf16 = jnp.bfloat16


def _layer_norm(y, g, b):
    mu = jnp.mean(y, axis=-1, keepdims=True)
    var = jnp.mean(jnp.square(y - mu), axis=-1, keepdims=True)
    return (y - mu) * lax.rsqrt(var + LN_EPS) * g + b


def _tile_copies(hbm, buf, sem, tile, slot, to_vmem):
    copies = []
    for b in range(BATCH):
        h = hbm.at[b, pl.ds(tile * T_TILE, T_TILE), :]
        v = buf.at[slot, :, b, :]
        src, dst = (h, v) if to_vmem else (v, h)
        copies.append(pltpu.make_async_copy(src, dst, sem.at[slot, b]))
    return copies


def _split_phases(x3):
    x4 = x3.reshape(H_TILE, PHASES, SUBLANES, x3.shape[-1])
    return jnp.concatenate([x4[:, ph].reshape(HALF, x3.shape[-1]) for ph in range(PHASES)], axis=0)


def _merge_phases(y):
    parts = [y[ph * HALF:(ph + 1) * HALF].reshape(H_TILE, SUBLANES, y.shape[-1])
             for ph in range(PHASES)]
    return jnp.stack(parts, axis=1).reshape(T_TILE, SUBLANES, y.shape[-1])


def _mixer_tile(x, win_ref, bin_ref, bmat_ref, a2re_ref, a2im_ref, wre_ref, wim_ref, kd_ref,
                dskip_ref, gluw_ref, glub_ref, wssm_ref, convw_ref, wconv_ref, wo_ref,
                g_ref, b_ref, xs_re, xs_im, ve_ext, vo_ext):
    xb = x.astype(bf16)

    def proj(lo, hi):
        return jnp.dot(xb, win_ref[:, lo:hi], preferred_element_type=f32) + bin_ref[:, lo:hi]

    u = proj(0, SSM_WIDTH)
    ub = u.astype(bf16)
    u2 = []
    for j in range(N_BLOCKS):
        cs = slice(j * BLOCK_CH, (j + 1) * BLOCK_CH)
        u2.append(jnp.concatenate([ub[:HALF, cs], ub[HALF:, cs]], axis=-1))
        bu = jnp.dot(u2[j], bmat_ref[j], preferred_element_type=f32)
        xs_re[SUBLANES:SUBLANES + HALF, j * BLOCK_ST:(j + 1) * BLOCK_ST] = bu[:, :BLOCK_ST]
        xs_im[SUBLANES:SUBLANES + HALF, j * BLOCK_ST:(j + 1) * BLOCK_ST] = bu[:, BLOCK_ST:]

    for j in range(N_BLOCKS):
        sl = slice(j * BLOCK_ST, (j + 1) * BLOCK_ST)
        ar = jnp.broadcast_to(a2re_ref[:, sl], (SUBLANES, BLOCK_ST))
        ai = jnp.broadcast_to(a2im_ref[:, sl], (SUBLANES, BLOCK_ST))
        sr, si = xs_re[0:SUBLANES, sl], xs_im[0:SUBLANES, sl]
        for k in range(H_TILE):
            r = slice((k + 1) * SUBLANES, (k + 2) * SUBLANES)
            sr, si = (ar * sr - ai * si + xs_re[r, sl], ar * si + ai * sr + xs_im[r, sl])
            xs_re[r, sl] = sr
            xs_im[r, sl] = si

    y_e, y_o = [], []
    for j in range(N_BLOCKS):
        sl = slice(j * BLOCK_ST, (j + 1) * BLOCK_ST)
        y2 = (jnp.dot(xs_re[0:HALF, sl].astype(bf16), wre_ref[j], preferred_element_type=f32)
              + jnp.dot(xs_im[0:HALF, sl].astype(bf16), wim_ref[j], preferred_element_type=f32)
              + jnp.dot(u2[j], kd_ref[j], preferred_element_type=f32))
        y_e.append(y2[:, :BLOCK_CH])
        y_o.append(y2[:, BLOCK_CH:])
    xs_re[0:SUBLANES, :] = xs_re[HALF:HALF + SUBLANES, :]
    xs_im[0:SUBLANES, :] = xs_im[HALF:HALF + SUBLANES, :]
    y_a = jnp.concatenate([jnp.concatenate(y_e, axis=-1), jnp.concatenate(y_o, axis=-1)], axis=0)
    y_a = y_a + dskip_ref[...] * u
    g = jax.nn.gelu(y_a)
    y_a = g * jax.nn.sigmoid(
        jnp.dot(g.astype(bf16), gluw_ref[...], preferred_element_type=f32) + glub_ref[...])
    y_a = jnp.dot(y_a.astype(bf16), wssm_ref[...], preferred_element_type=f32)

    h = proj(SSM_WIDTH, SSM_WIDTH + CONV_WIDTH)
    c_gate = proj(SSM_WIDTH + CONV_WIDTH, SSM_WIDTH + 2 * CONV_WIDTH)
    v = c_gate * h
    ve_ext[SUBLANES:SUBLANES + HALF, :] = v[:HALF]
    vo_ext[SUBLANES:SUBLANES + HALF, :] = v[HALF:]
    w0, w1, w2 = convw_ref[0:1, :], convw_ref[1:2, :], convw_ref[2:3, :]
    z_e = w0 * ve_ext[0:HALF, :] + w1 * vo_ext[0:HALF, :] + w2 * v[:HALF]
    z_o = w0 * vo_ext[0:HALF, :] + w1 * v[:HALF] + w2 * v[HALF:]
    ve_ext[0:SUBLANES, :] = ve_ext[HALF:HALF + SUBLANES, :]
    vo_ext[0:SUBLANES, :] = vo_ext[HALF:HALF + SUBLANES, :]
    z = jnp.concatenate([z_e, z_o], axis=0)
    b_gate = proj(SSM_WIDTH + 2 * CONV_WIDTH, SSM_WIDTH + 3 * CONV_WIDTH)
    y_b = jnp.dot((b_gate * z).astype(bf16), wconv_ref[...], preferred_element_type=f32)

    o4 = SSM_WIDTH + 3 * CONV_WIDTH
    merged = (jax.nn.sigmoid(proj(o4, o4 + D_MODEL)) * y_a
              + jax.nn.sigmoid(proj(o4 + D_MODEL, o4 + 2 * D_MODEL)) * y_b)
    mix = jnp.dot(merged.astype(bf16), wo_ref[...], preferred_element_type=f32)
    return _layer_norm(ALPHA * x + mix, g_ref[...], b_ref[...])


def _ffn_tile(x, wg_ref, wu_ref, wd_ref, g_ref, b_ref):
    xb = x.astype(bf16)
    gate = jnp.dot(xb, wg_ref[...], preferred_element_type=f32)
    up = jnp.dot(xb, wu_ref[...], preferred_element_type=f32)
    hid = (jax.nn.silu(gate) * up).astype(bf16)
    ffn = jnp.dot(hid, wd_ref[...], preferred_element_type=f32)
    return _layer_norm(ALPHA * x + ffn, g_ref[...], b_ref[...])


N_MIXER_PARAMS = 17
N_FFN_PARAMS = 5


def _layer_kernel(x_hbm, *refs):
    mixer_refs = refs[:N_MIXER_PARAMS]
    ffn_refs = refs[N_MIXER_PARAMS:N_MIXER_PARAMS + N_FFN_PARAMS]
    o_hbm, xbuf, xsem, obuf, osem, x1_ref, xs_re, xs_im, ve_ext, vo_ext = \
        refs[N_MIXER_PARAMS + N_FFN_PARAMS:]
    i = pl.program_id(0)
    slot = i % 2
    ffn_tile = i - 1
    ffn_slot = (i + 1) % 2

    @pl.when(i == 0)
    def _():
        for cp in _tile_copies(x_hbm, xbuf, xsem, 0, 0, True):
            cp.start()
        zeros = jnp.zeros((SUBLANES, N_STATE), f32)
        xs_re[0:SUBLANES, :] = zeros
        xs_im[0:SUBLANES, :] = zeros
        ve_ext[0:SUBLANES, :] = jnp.zeros((SUBLANES, CONV_WIDTH), f32)
        vo_ext[0:SUBLANES, :] = jnp.zeros((SUBLANES, CONV_WIDTH), f32)
        x1_ref[...] = jnp.zeros_like(x1_ref)

    @pl.when(i + 1 < N_TILES)
    def _():
        for cp in _tile_copies(x_hbm, xbuf, xsem, i + 1, 1 - slot, True):
            cp.start()

    @pl.when(i < N_TILES)
    def _():
        for cp in _tile_copies(x_hbm, xbuf, xsem, i, slot, True):
            cp.wait()

    @pl.when(ffn_tile >= 2)
    def _():
        for cp in _tile_copies(o_hbm, obuf, osem, ffn_tile - 2, ffn_slot, False):
            cp.wait()

    y = _ffn_tile(x1_ref[...], *ffn_refs)
    x1_ref[...] = _mixer_tile(_split_phases(xbuf[slot]), *mixer_refs, xs_re, xs_im, ve_ext, vo_ext)
    obuf[ffn_slot] = _merge_phases(y)

    @pl.when(i > 0)
    def _():
        for cp in _tile_copies(o_hbm, obuf, osem, ffn_tile, ffn_slot, False):
            cp.start()

    @pl.when(i == N_TILES)
    def _():
        for cp in _tile_copies(o_hbm, obuf, osem, ffn_tile - 1, 1 - ffn_slot, False):
            cp.wait()
        for cp in _tile_copies(o_hbm, obuf, osem, ffn_tile, ffn_slot, False):
            cp.wait()


def _const_spec(shape):
    nd = len(shape)
    return pl.BlockSpec(shape, lambda i, nd=nd: (0,) * nd, pipeline_mode=pl.Buffered(1))


def _ssm_params(lam_re, lam_im, log_dt, b_re, b_im, c_re, c_im):
    dt = jnp.exp(log_dt)[:, None]
    mag = jnp.exp(lam_re * dt)
    a_re = mag * jnp.cos(lam_im * dt)
    a_im = mag * jnp.sin(lam_im * dt)
    den = lam_re * lam_re + lam_im * lam_im
    num_re = a_re - 1.0
    fr = (num_re * lam_re + a_im * lam_im) / den
    fi = (a_im * lam_re - num_re * lam_im) / den
    bb_re = fr[..., None] * b_re - fi[..., None] * b_im
    bb_im = fr[..., None] * b_im + fi[..., None] * b_re
    a2_re = a_re * a_re - a_im * a_im
    a2_im = 2.0 * a_re * a_im
    abb_re = a_re[..., None] * bb_re - a_im[..., None] * bb_im
    abb_im = a_re[..., None] * bb_im + a_im[..., None] * bb_re
    ca_re = c_re * a_re[:, None, :] - c_im * a_im[:, None, :]
    ca_im = c_re * a_im[:, None, :] + c_im * a_re[:, None, :]
    ca2_re = c_re * a2_re[:, None, :] - c_im * a2_im[:, None, :]
    ca2_im = c_re * a2_im[:, None, :] + c_im * a2_re[:, None, :]
    hi = lax.Precision.HIGHEST
    k0 = (jnp.einsum('gop,gpc->goc', c_re, bb_re, precision=hi)
          - jnp.einsum('gop,gpc->goc', c_im, bb_im, precision=hi))
    k1 = (jnp.einsum('gop,gpc->goc', c_re, abb_re, precision=hi)
          - jnp.einsum('gop,gpc->goc', c_im, abb_im, precision=hi))
    eye = jnp.eye(GROUPS_PER_BLOCK, dtype=f32)

    def block_diag(m, rows, cols):
        m = m.reshape(N_BLOCKS, GROUPS_PER_BLOCK, rows, 1, cols)
        m = m * eye[None, :, None, :, None]
        return m.reshape(N_BLOCKS, GROUPS_PER_BLOCK * rows, GROUPS_PER_BLOCK * cols)

    def in_block(m):
        return block_diag(jnp.swapaxes(m, 1, 2), SSM_GROUP, SSM_STATE)

    def out_block(m):
        return block_diag(jnp.swapaxes(m, 1, 2), SSM_STATE, SSM_GROUP)

    def direct_block(m):
        return block_diag(jnp.swapaxes(m, 1, 2), SSM_GROUP, SSM_GROUP)

    bmat = jnp.concatenate(
        [jnp.concatenate([in_block(abb_re), in_block(abb_im)], axis=-1),
         jnp.concatenate([in_block(bb_re), in_block(bb_im)], axis=-1)],
        axis=1).astype(bf16)
    w_re = jnp.concatenate([out_block(ca_re), out_block(ca2_re)], axis=-1).astype(bf16)
    w_im = jnp.concatenate([out_block(-ca_im), out_block(-ca2_im)], axis=-1).astype(bf16)
    kd = jnp.concatenate(
        [jnp.concatenate([direct_block(k0), direct_block(k1)], axis=-1),
         jnp.concatenate([jnp.zeros_like(direct_block(k0)), direct_block(k0)], axis=-1)],
        axis=1).astype(bf16)
    return bmat, a2_re.reshape(1, N_STATE), a2_im.reshape(1, N_STATE), w_re, w_im, kd


def _layer(x, w_in, b_in, lam_re, lam_im, log_dt, b_re, b_im, c_re, c_im, d_skip, glu_w, glu_b,
           w_ssm_out, conv_w, w_conv_out, w_o, ln1_g, ln1_b, w_gate, w_up, w_down, ln2_g, ln2_b):
    row = lambda v: v.reshape(1, -1).astype(f32)
    mixer_in = (w_in.astype(bf16), row(b_in),
                *_ssm_params(lam_re, lam_im, log_dt, b_re, b_im, c_re, c_im), row(d_skip),
                glu_w.astype(bf16), row(glu_b), w_ssm_out.astype(bf16), conv_w.astype(f32),
                w_conv_out.astype(bf16), w_o.astype(bf16), row(ln1_g), row(ln1_b))
    ffn_in = (w_gate.astype(bf16), w_up.astype(bf16), w_down.astype(bf16), row(ln2_g), row(ln2_b))
    assert len(mixer_in) == N_MIXER_PARAMS and len(ffn_in) == N_FFN_PARAMS
    return pl.pallas_call(
        _layer_kernel,
        grid=(N_TILES + 1,),
        in_specs=[pl.BlockSpec(memory_space=pl.ANY)]
                 + [_const_spec(a.shape) for a in mixer_in + ffn_in],
        out_specs=pl.BlockSpec(memory_space=pl.ANY),
        out_shape=jax.ShapeDtypeStruct((BATCH, SEQ, D_MODEL), f32),
        scratch_shapes=[pltpu.VMEM((2, T_TILE, BATCH, D_MODEL), f32),
                        pltpu.SemaphoreType.DMA((2, BATCH)),
                        pltpu.VMEM((2, T_TILE, BATCH, D_MODEL), f32),
                        pltpu.SemaphoreType.DMA((2, BATCH)),
                        pltpu.VMEM((R_TILE, D_MODEL), f32),
                        pltpu.VMEM((HALF + SUBLANES, N_STATE), f32),
                        pltpu.VMEM((HALF + SUBLANES, N_STATE), f32),
                        pltpu.VMEM((HALF + SUBLANES, CONV_WIDTH), f32),
                        pltpu.VMEM((HALF + SUBLANES, CONV_WIDTH), f32)],
        compiler_params=pltpu.CompilerParams(dimension_semantics=("arbitrary",),
                                             vmem_limit_bytes=VMEM_LIMIT_BYTES),
        name="layer",
    )(x, *mixer_in, *ffn_in)


def kernel(x, w_in, b_in, ssm_lambda_re, ssm_lambda_im, ssm_log_dt, ssm_b_re, ssm_b_im, ssm_c_re, ssm_c_im, ssm_d, glu_w, glu_b, w_ssm_out, conv_w, w_conv_out, w_o, ln1_g, ln1_b, w_gate, w_up, w_down, ln2_g, ln2_b):
    assert x.shape == (BATCH, SEQ, D_MODEL) and BATCH == SUBLANES
    for l in range(w_in.shape[0]):
        x = _layer(x, w_in[l], b_in[l], ssm_lambda_re[l], ssm_lambda_im[l], ssm_log_dt[l],
                   ssm_b_re[l], ssm_b_im[l], ssm_c_re[l], ssm_c_im[l], ssm_d[l], glu_w[l],
                   glu_b[l], w_ssm_out[l], conv_w[l], w_conv_out[l], w_o[l], ln1_g[l], ln1_b[l],
                   w_gate[l], w_up[l], w_down[l], ln2_g[l], ln2_b[l])
    return x
```

```python
import jax
import jax.numpy as jnp
from jax import lax
from jax.experimental import pallas as pl
from jax.experimental.pallas import tpu as pltpu

D_MODEL = 1024
BATCH = 8
SEQ = 4096
SSM_WIDTH = D_MODEL // 2
SSM_GROUP = 16
SSM_GROUPS = SSM_WIDTH // SSM_GROUP
SSM_STATE = 64
CONV_WIDTH = D_MODEL // 2
CONV_K = 3
FFN_HIDDEN = 2816
IN_COLS = SSM_WIDTH + 3 * CONV_WIDTH + 2 * D_MODEL
DEPTH = 1
ALPHA = (2.0 * DEPTH) ** 0.25
LN_EPS = 1e-5

SUBLANES = 8
VMEM_LIMIT_BYTES = 60 * 1024 * 1024

GROUPS_PER_BLOCK = 8
N_BLOCKS = SSM_GROUPS // GROUPS_PER_BLOCK
BLOCK_CH = GROUPS_PER_BLOCK * SSM_GROUP
BLOCK_ST = GROUPS_PER_BLOCK * SSM_STATE
N_STATE = SSM_GROUPS * SSM_STATE

PHASES = 2
T_TILE = 32
R_TILE = T_TILE * BATCH
H_TILE = T_TILE // PHASES
HALF = H_TILE * SUBLANES
N_TILES = SEQ // T_TILE
assert N_TILES >= 2 and SEQ % T_TILE == 0 and T_TILE % PHASES == 0

f32 = jnp.float32
bf16 = jnp.bfloat16


def _layer_norm(y, g, b):
    mu = jnp.mean(y, axis=-1, keepdims=True)
    var = jnp.mean(jnp.square(y - mu), axis=-1, keepdims=True)
    return (y - mu) * lax.rsqrt(var + LN_EPS) * g + b


def _tile_copies(hbm, buf, sem, tile, slot, to_vmem):
    copies = []
    for b in range(BATCH):
        h = hbm.at[b, pl.ds(tile * T_TILE, T_TILE), :]
        v = buf.at[slot, :, b, :]
        src, dst = (h, v) if to_vmem else (v, h)
        copies.append(pltpu.make_async_copy(src, dst, sem.at[slot, b]))
    return copies


def _split_phases(x3):
    x4 = x3.reshape(H_TILE, PHASES, SUBLANES, x3.shape[-1])
    return jnp.concatenate([x4[:, ph].reshape(HALF, x3.shape[-1]) for ph in range(PHASES)], axis=0)


def _merge_phases(y):
    parts = [y[ph * HALF:(ph + 1) * HALF].reshape(H_TILE, SUBLANES, y.shape[-1])
             for ph in range(PHASES)]
    return jnp.stack(parts, axis=1).reshape(T_TILE, SUBLANES, y.shape[-1])


def _layer_tile(pre1, x, mixer_refs, ffn_refs, xs_re, xs_im, ve_ext, vo_ext):
    (win_ref, bin_ref, bmat_ref, a2re_ref, a2im_ref, wre_ref, wim_ref, kd_ref, dskip_ref, gluw_ref,
     glub_ref, wssm_ref, convw_ref, wconv_ref, wo_ref, ln1g_ref, ln1b_ref) = mixer_refs
    wg_ref, wu_ref, wd_ref, ln2g_ref, ln2b_ref = ffn_refs
    dot = lambda a, w: jnp.dot(a, w, preferred_element_type=f32)
    x1 = _layer_norm(pre1, ln1g_ref[...], ln1b_ref[...])
    x1b = x1.astype(bf16)
    xb = x.astype(bf16)

    def proj(lo, hi):
        return dot(xb, win_ref[:, lo:hi]) + bin_ref[:, lo:hi]

    u = proj(0, SSM_WIDTH)
    ub = u.astype(bf16)
    u2 = []
    for j in range(N_BLOCKS):
        cs = slice(j * BLOCK_CH, (j + 1) * BLOCK_CH)
        u2.append(jnp.concatenate([ub[:HALF, cs], ub[HALF:, cs]], axis=-1))
        bu = dot(u2[j], bmat_ref[j])
        xs_re[SUBLANES:SUBLANES + HALF, j * BLOCK_ST:(j + 1) * BLOCK_ST] = bu[:, :BLOCK_ST]
        xs_im[SUBLANES:SUBLANES + HALF, j * BLOCK_ST:(j + 1) * BLOCK_ST] = bu[:, BLOCK_ST:]

    o4 = SSM_WIDTH + 3 * CONV_WIDTH
    h = proj(SSM_WIDTH, SSM_WIDTH + CONV_WIDTH)
    c_gate = proj(SSM_WIDTH + CONV_WIDTH, SSM_WIDTH + 2 * CONV_WIDTH)
    b_gate = proj(SSM_WIDTH + 2 * CONV_WIDTH, o4)
    gate_a = proj(o4, o4 + D_MODEL)
    gate_b = proj(o4 + D_MODEL, o4 + 2 * D_MODEL)

    for j in range(N_BLOCKS):
        sl = slice(j * BLOCK_ST, (j + 1) * BLOCK_ST)
        ar = jnp.broadcast_to(a2re_ref[:, sl], (SUBLANES, BLOCK_ST))
        ai = jnp.broadcast_to(a2im_ref[:, sl], (SUBLANES, BLOCK_ST))
        sr, si = xs_re[0:SUBLANES, sl], xs_im[0:SUBLANES, sl]
        for k in range(H_TILE):
            r = slice((k + 1) * SUBLANES, (k + 2) * SUBLANES)
            sr, si = (ar * sr - ai * si + xs_re[r, sl], ar * si + ai * sr + xs_im[r, sl])
            xs_re[r, sl] = sr
            xs_im[r, sl] = si

    y_e, y_o = [], []
    for j in range(N_BLOCKS):
        sl = slice(j * BLOCK_ST, (j + 1) * BLOCK_ST)
        y2 = (dot(xs_re[0:HALF, sl].astype(bf16), wre_ref[j])
              + dot(xs_im[0:HALF, sl].astype(bf16), wim_ref[j]) + dot(u2[j], kd_ref[j]))
        y_e.append(y2[:, :BLOCK_CH])
        y_o.append(y2[:, BLOCK_CH:])
    xs_re[0:SUBLANES, :] = xs_re[HALF:HALF + SUBLANES, :]
    xs_im[0:SUBLANES, :] = xs_im[HALF:HALF + SUBLANES, :]
    y_a = jnp.concatenate([jnp.concatenate(y_e, axis=-1), jnp.concatenate(y_o, axis=-1)], axis=0)
    y_a = y_a + dskip_ref[...] * u

    gate = dot(x1b, wg_ref[...])
    up = dot(x1b, wu_ref[...])

    v = c_gate * h
    ve_ext[SUBLANES:SUBLANES + HALF, :] = v[:HALF]
    vo_ext[SUBLANES:SUBLANES + HALF, :] = v[HALF:]
    w0, w1, w2 = convw_ref[0:1, :], convw_ref[1:2, :], convw_ref[2:3, :]
    z_e = w0 * ve_ext[0:HALF, :] + w1 * vo_ext[0:HALF, :] + w2 * v[:HALF]
    z_o = w0 * vo_ext[0:HALF, :] + w1 * v[:HALF] + w2 * v[HALF:]
    ve_ext[0:SUBLANES, :] = ve_ext[HALF:HALF + SUBLANES, :]
    vo_ext[0:SUBLANES, :] = vo_ext[HALF:HALF + SUBLANES, :]
    bz = (b_gate * jnp.concatenate([z_e, z_o], axis=0)).astype(bf16)

    g = jax.nn.gelu(y_a)
    glu = dot(g.astype(bf16), gluw_ref[...]) + glub_ref[...]
    y_b = dot(bz, wconv_ref[...])
    y_a = dot((g * jax.nn.sigmoid(glu)).astype(bf16), wssm_ref[...])

    hid = (jax.nn.silu(gate) * up).astype(bf16)
    ffn = dot(hid, wd_ref[...])
    merged = jax.nn.sigmoid(gate_a) * y_a + jax.nn.sigmoid(gate_b) * y_b
    mix = dot(merged.astype(bf16), wo_ref[...])
    out = _layer_norm(ALPHA * x1 + ffn, ln2g_ref[...], ln2b_ref[...])
    return out, ALPHA * x + mix


N_MIXER_PARAMS = 17
N_FFN_PARAMS = 5


def _layer_kernel(x_hbm, *refs):
    mixer_refs = refs[:N_MIXER_PARAMS]
    ffn_refs = refs[N_MIXER_PARAMS:N_MIXER_PARAMS + N_FFN_PARAMS]
    o_hbm, xbuf, xsem, obuf, osem, pre1_ref, xs_re, xs_im, ve_ext, vo_ext = \
        refs[N_MIXER_PARAMS + N_FFN_PARAMS:]
    i = pl.program_id(0)
    slot = i % 2
    ffn_tile = i - 1
    ffn_slot = (i + 1) % 2

    @pl.when(i == 0)
    def _():
        for cp in _tile_copies(x_hbm, xbuf, xsem, 0, 0, True):
            cp.start()
        zeros = jnp.zeros((SUBLANES, N_STATE), f32)
        xs_re[0:SUBLANES, :] = zeros
        xs_im[0:SUBLANES, :] = zeros
        ve_ext[0:SUBLANES, :] = jnp.zeros((SUBLANES, CONV_WIDTH), f32)
        vo_ext[0:SUBLANES, :] = jnp.zeros((SUBLANES, CONV_WIDTH), f32)
        pre1_ref[...] = jnp.zeros_like(pre1_ref)

    @pl.when(i + 1 < N_TILES)
    def _():
        for cp in _tile_copies(x_hbm, xbuf, xsem, i + 1, 1 - slot, True):
            cp.start()

    @pl.when(i < N_TILES)
    def _():
        for cp in _tile_copies(x_hbm, xbuf, xsem, i, slot, True):
            cp.wait()

    @pl.when(ffn_tile >= 2)
    def _():
        for cp in _tile_copies(o_hbm, obuf, osem, ffn_tile - 2, ffn_slot, False):
            cp.wait()

    y, pre1 = _layer_tile(pre1_ref[...], _split_phases(xbuf[slot]), mixer_refs, ffn_refs,
                          xs_re, xs_im, ve_ext, vo_ext)
    obuf[ffn_slot] = _merge_phases(y)
    pre1_ref[...] = pre1

    @pl.when(i > 0)
    def _():
        for cp in _tile_copies(o_hbm, obuf, osem, ffn_tile, ffn_slot, False):
            cp.start()

    @pl.when(i == N_TILES)
    def _():
        for cp in _tile_copies(o_hbm, obuf, osem, ffn_tile - 1, 1 - ffn_slot, False):
            cp.wait()
        for cp in _tile_copies(o_hbm, obuf, osem, ffn_tile, ffn_slot, False):
            cp.wait()


def _const_spec(shape):
    nd = len(shape)
    return pl.BlockSpec(shape, lambda i, nd=nd: (0,) * nd, pipeline_mode=pl.Buffered(1))


def _ssm_params(lam_re, lam_im, log_dt, b_re, b_im, c_re, c_im):
    dt = jnp.exp(log_dt)[:, None]
    mag = jnp.exp(lam_re * dt)
    a_re = mag * jnp.cos(lam_im * dt)
    a_im = mag * jnp.sin(lam_im * dt)
    den = lam_re * lam_re + lam_im * lam_im
    num_re = a_re - 1.0
    fr = (num_re * lam_re + a_im * lam_im) / den
    fi = (a_im * lam_re - num_re * lam_im) / den
    bb_re = fr[..., None] * b_re - fi[..., None] * b_im
    bb_im = fr[..., None] * b_im + fi[..., None] * b_re
    a2_re = a_re * a_re - a_im * a_im
    a2_im = 2.0 * a_re * a_im
    abb_re = a_re[..., None] * bb_re - a_im[..., None] * bb_im
    abb_im = a_re[..., None] * bb_im + a_im[..., None] * bb_re
    ca_re = c_re * a_re[:, None, :] - c_im * a_im[:, None, :]
    ca_im = c_re * a_im[:, None, :] + c_im * a_re[:, None, :]
    ca2_re = c_re * a2_re[:, None, :] - c_im * a2_im[:, None, :]
    ca2_im = c_re * a2_im[:, None, :] + c_im * a2_re[:, None, :]
    hi = lax.Precision.HIGHEST
    k0 = (jnp.einsum('gop,gpc->goc', c_re, bb_re, precision=hi)
          - jnp.einsum('gop,gpc->goc', c_im, bb_im, precision=hi))
    k1 = (jnp.einsum('gop,gpc->goc', c_re, abb_re, precision=hi)
          - jnp.einsum('gop,gpc->goc', c_im, abb_im, precision=hi))
    eye = jnp.eye(GROUPS_PER_BLOCK, dtype=f32)

    def block_diag(m, rows, cols):
        m = m.reshape(N_BLOCKS, GROUPS_PER_BLOCK, rows, 1, cols)
        m = m * eye[None, :, None, :, None]
        return m.reshape(N_BLOCKS, GROUPS_PER_BLOCK * rows, GROUPS_PER_BLOCK * cols)

    def in_block(m):
        return block_diag(jnp.swapaxes(m, 1, 2), SSM_GROUP, SSM_STATE)

    def out_block(m):
        return block_diag(jnp.swapaxes(m, 1, 2), SSM_STATE, SSM_GROUP)

    def direct_block(m):
        return block_diag(jnp.swapaxes(m, 1, 2), SSM_GROUP, SSM_GROUP)

    bmat = jnp.concatenate(
        [jnp.concatenate([in_block(abb_re), in_block(abb_im)], axis=-1),
         jnp.concatenate([in_block(bb_re), in_block(bb_im)], axis=-1)],
        axis=1).astype(bf16)
    w_re = jnp.concatenate([out_block(ca_re), out_block(ca2_re)], axis=-1).astype(bf16)
    w_im = jnp.concatenate([out_block(-ca_im), out_block(-ca2_im)], axis=-1).astype(bf16)
    kd = jnp.concatenate(
        [jnp.concatenate([direct_block(k0), direct_block(k1)], axis=-1),
         jnp.concatenate([jnp.zeros_like(direct_block(k0)), direct_block(k0)], axis=-1)],
        axis=1).astype(bf16)
    return bmat, a2_re.reshape(1, N_STATE), a2_im.reshape(1, N_STATE), w_re, w_im, kd


def _layer(x, w_in, b_in, lam_re, lam_im, log_dt, b_re, b_im, c_re, c_im, d_skip, glu_w, glu_b,
           w_ssm_out, conv_w, w_conv_out, w_o, ln1_g, ln1_b, w_gate, w_up, w_down, ln2_g, ln2_b):
    row = lambda v: v.reshape(1, -1).astype(f32)
    mixer_in = (w_in.astype(bf16), row(b_in),
                *_ssm_params(lam_re, lam_im, log_dt, b_re, b_im, c_re, c_im), row(d_skip),
                glu_w.astype(bf16), row(glu_b), w_ssm_out.astype(bf16), conv_w.astype(f32),
                w_conv_out.astype(bf16), w_o.astype(bf16), row(ln1_g), row(ln1_b))
    ffn_in = (w_gate.astype(bf16), w_up.astype(bf16), w_down.astype(bf16), row(ln2_g), row(ln2_b))
    assert len(mixer_in) == N_MIXER_PARAMS and len(ffn_in) == N_FFN_PARAMS
    return pl.pallas_call(
        _layer_kernel,
        grid=(N_TILES + 1,),
        in_specs=[pl.BlockSpec(memory_space=pl.ANY)]
                 + [_const_spec(a.shape) for a in mixer_in + ffn_in],
        out_specs=pl.BlockSpec(memory_space=pl.ANY),
        out_shape=jax.ShapeDtypeStruct((BATCH, SEQ, D_MODEL), f32),
        scratch_shapes=[pltpu.VMEM((2, T_TILE, BATCH, D_MODEL), f32),
                        pltpu.SemaphoreType.DMA((2, BATCH)),
                        pltpu.VMEM((2, T_TILE, BATCH, D_MODEL), f32),
                        pltpu.SemaphoreType.DMA((2, BATCH)),
                        pltpu.VMEM((R_TILE, D_MODEL), f32),
                        pltpu.VMEM((HALF + SUBLANES, N_STATE), f32),
                        pltpu.VMEM((HALF + SUBLANES, N_STATE), f32),
                        pltpu.VMEM((HALF + SUBLANES, CONV_WIDTH), f32),
                        pltpu.VMEM((HALF + SUBLANES, CONV_WIDTH), f32)],
        compiler_params=pltpu.CompilerParams(dimension_semantics=("arbitrary",),
                                             vmem_limit_bytes=VMEM_LIMIT_BYTES),
        name="layer",
    )(x, *mixer_in, *ffn_in)


def kernel(x, w_in, b_in, ssm_lambda_re, ssm_lambda_im, ssm_log_dt, ssm_b_re, ssm_b_im, ssm_c_re, ssm_c_im, ssm_d, glu_w, glu_b, w_ssm_out, conv_w, w_conv_out, w_o, ln1_g, ln1_b, w_gate, w_up, w_down, ln2_g, ln2_b):
    assert x.shape == (BATCH, SEQ, D_MODEL) and BATCH == SUBLANES
    for l in range(w_in.shape[0]):
        x = _layer(x, w_in[l], b_in[l], ssm_lambda_re[l], ssm_lambda_im[l], ssm_log_dt[l],
                   ssm_b_re[l], ssm_b_im[l], ssm_c_re[l], ssm_c_im[l], ssm_d[l], glu_w[l],
                   glu_b[l], w_ssm_out[l], conv_w[l], w_conv_out[l], w_o[l], ln1_g[l], ln1_b[l],
                   w_gate[l], w_up[l], w_down[l], ln2_g[l], ln2_b[l])
    return x
```

```python
import jax
import jax.numpy as jnp
from jax import lax
from jax.experimental import pallas as pl
from jax.experimental.pallas import tpu as pltpu

D_MODEL = 1024
BATCH = 8
SEQ = 4096
SSM_WIDTH = D_MODEL // 2
SSM_GROUP = 16
SSM_GROUPS = SSM_WIDTH // SSM_GROUP
SSM_STATE = 64
CONV_WIDTH = D_MODEL // 2
CONV_K = 3
FFN_HIDDEN = 2816
IN_COLS = SSM_WIDTH + 3 * CONV_WIDTH + 2 * D_MODEL
DEPTH = 1
ALPHA = (2.0 * DEPTH) ** 0.25
LN_EPS = 1e-5

SUBLANES = 8
VMEM_LIMIT_BYTES = 60 * 1024 * 1024

GROUPS_PER_BLOCK = 8
N_BLOCKS = SSM_GROUPS // GROUPS_PER_BLOCK
BLOCK_CH = GROUPS_PER_BLOCK * SSM_GROUP
BLOCK_ST = GROUPS_PER_BLOCK * SSM_STATE
N_STATE = SSM_GROUPS * SSM_STATE

PHASES = 2
T_TILE = 32
R_TILE = T_TILE * BATCH
H_TILE = T_TILE // PHASES
HALF = H_TILE * SUBLANES
N_TILES = SEQ // T_TILE
assert N_TILES >= 2 and SEQ % T_TILE == 0 and T_TILE % PHASES == 0

f32 = jnp.float32
bf16 = jnp.bfloat16


def _layer_norm(y, g, b):
    mu = jnp.mean(y, axis=-1, keepdims=True)
    var = jnp.mean(jnp.square(y - mu), axis=-1, keepdims=True)
    return (y - mu) * lax.rsqrt(var + LN_EPS) * g + b


def _tile_copies(hbm, buf, sem, tile, slot, to_vmem):
    copies = []
    for b in range(BATCH):
        h = hbm.at[b, pl.ds(tile * T_TILE, T_TILE), :]
        v = buf.at[slot, :, b, :]
        src, dst = (h, v) if to_vmem else (v, h)
        copies.append(pltpu.make_async_copy(src, dst, sem.at[slot, b]))
    return copies


def _split_phases(x3):
    x4 = x3.reshape(H_TILE, PHASES, SUBLANES, x3.shape[-1])
    return jnp.concatenate([x4[:, ph].reshape(HALF, x3.shape[-1]) for ph in range(PHASES)], axis=0)


def _merge_phases(y):
    parts = [y[ph * HALF:(ph + 1) * HALF].reshape(H_TILE, SUBLANES, y.shape[-1])
             for ph in range(PHASES)]
    return jnp.stack(parts, axis=1).reshape(T_TILE, SUBLANES, y.shape[-1])


def _layer_tile(pre1, x, mixer_refs, ffn_refs, xs_re, xs_im, ve_ext, vo_ext):
    (win_ref, bin_ref, bmat_ref, a2_ref, wre_ref, wim_ref, kd_ref, dskip_ref, gluw_ref, glub_ref,
     wssm_ref, convw_ref, wconv_ref, wo_ref, ln1g_ref, ln1b_ref) = mixer_refs
    wg_ref, wu_ref, wd_ref, ln2g_ref, ln2b_ref = ffn_refs
    dot = lambda a, w: jnp.dot(a, w, preferred_element_type=f32)
    x1 = _layer_norm(pre1, ln1g_ref[...], ln1b_ref[...])
    x1b = x1.astype(bf16)
    xb = x.astype(bf16)

    def proj(lo, hi):
        return dot(xb, win_ref[:, lo:hi]) + bin_ref[:, lo:hi]

    u = proj(0, SSM_WIDTH)
    ub = u.astype(bf16)
    u2 = []
    for j in range(N_BLOCKS):
        cs = slice(j * BLOCK_CH, (j + 1) * BLOCK_CH)
        u2.append(jnp.concatenate([ub[:HALF, cs], ub[HALF:, cs]], axis=-1))
        bu = dot(u2[j], bmat_ref[j])
        xs_re[SUBLANES:SUBLANES + HALF, j * BLOCK_ST:(j + 1) * BLOCK_ST] = bu[:, :BLOCK_ST]
        xs_im[SUBLANES:SUBLANES + HALF, j * BLOCK_ST:(j + 1) * BLOCK_ST] = bu[:, BLOCK_ST:]

    o4 = SSM_WIDTH + 3 * CONV_WIDTH
    h = proj(SSM_WIDTH, SSM_WIDTH + CONV_WIDTH)
    c_gate = proj(SSM_WIDTH + CONV_WIDTH, SSM_WIDTH + 2 * CONV_WIDTH)
    b_gate = proj(SSM_WIDTH + 2 * CONV_WIDTH, o4)
    gate_a = proj(o4, o4 + D_MODEL)
    gate_b = proj(o4 + D_MODEL, o4 + 2 * D_MODEL)

    for j in range(N_BLOCKS):
        sl = slice(j * BLOCK_ST, (j + 1) * BLOCK_ST)
        ar = jnp.broadcast_to(a2_ref[0:1, sl], (SUBLANES, BLOCK_ST))
        ai = jnp.broadcast_to(a2_ref[1:2, sl], (SUBLANES, BLOCK_ST))
        sr, si = xs_re[0:SUBLANES, sl], xs_im[0:SUBLANES, sl]
        for k in range(H_TILE):
            r = slice((k + 1) * SUBLANES, (k + 2) * SUBLANES)
            sr, si = (ar * sr - ai * si + xs_re[r, sl], ar * si + ai * sr + xs_im[r, sl])
            xs_re[r, sl] = sr
            xs_im[r, sl] = si

    y_e, y_o = [], []
    for j in range(N_BLOCKS):
        sl = slice(j * BLOCK_ST, (j + 1) * BLOCK_ST)
        y2 = (dot(xs_re[0:HALF, sl].astype(bf16), wre_ref[j])
              + dot(xs_im[0:HALF, sl].astype(bf16), wim_ref[j]) + dot(u2[j], kd_ref[j]))
        y_e.append(y2[:, :BLOCK_CH])
        y_o.append(y2[:, BLOCK_CH:])
    xs_re[0:SUBLANES, :] = xs_re[HALF:HALF + SUBLANES, :]
    xs_im[0:SUBLANES, :] = xs_im[HALF:HALF + SUBLANES, :]
    y_a = jnp.concatenate([jnp.concatenate(y_e, axis=-1), jnp.concatenate(y_o, axis=-1)], axis=0)
    y_a = y_a + dskip_ref[...] * u

    gate = dot(x1b, wg_ref[...])
    up = dot(x1b, wu_ref[...])

    v = c_gate * h
    ve_ext[SUBLANES:SUBLANES + HALF, :] = v[:HALF]
    vo_ext[SUBLANES:SUBLANES + HALF, :] = v[HALF:]
    w0, w1, w2 = convw_ref[0:1, :], convw_ref[1:2, :], convw_ref[2:3, :]
    z_e = w0 * ve_ext[0:HALF, :] + w1 * vo_ext[0:HALF, :] + w2 * v[:HALF]
    z_o = w0 * vo_ext[0:HALF, :] + w1 * v[:HALF] + w2 * v[HALF:]
    ve_ext[0:SUBLANES, :] = ve_ext[HALF:HALF + SUBLANES, :]
    vo_ext[0:SUBLANES, :] = vo_ext[HALF:HALF + SUBLANES, :]
    bz = (b_gate * jnp.concatenate([z_e, z_o], axis=0)).astype(bf16)

    g = jax.nn.gelu(y_a)
    glu = dot(g.astype(bf16), gluw_ref[...]) + glub_ref[...]
    y_b = dot(bz, wconv_ref[...])
    y_a = dot((g * jax.nn.sigmoid(glu)).astype(bf16), wssm_ref[...])

    hid = (jax.nn.silu(gate) * up).astype(bf16)
    ffn = dot(hid, wd_ref[...])
    merged = jax.nn.sigmoid(gate_a) * y_a + jax.nn.sigmoid(gate_b) * y_b
    mix = dot(merged.astype(bf16), wo_ref[...])
    out = _layer_norm(ALPHA * x1 + ffn, ln2g_ref[...], ln2b_ref[...])
    return out, ALPHA * x + mix


N_MIXER_PARAMS = 16
N_FFN_PARAMS = 5


def _layer_kernel(x_hbm, *refs):
    mixer_refs = refs[:N_MIXER_PARAMS]
    ffn_refs = refs[N_MIXER_PARAMS:N_MIXER_PARAMS + N_FFN_PARAMS]
    o_hbm, xbuf, xsem, obuf, osem, pre1_ref, xs_re, xs_im, ve_ext, vo_ext = \
        refs[N_MIXER_PARAMS + N_FFN_PARAMS:]
    i = pl.program_id(0)
    slot = i % 2
    ffn_tile = i - 1
    ffn_slot = (i + 1) % 2

    @pl.when(i == 0)
    def _():
        for cp in _tile_copies(x_hbm, xbuf, xsem, 0, 0, True):
            cp.start()
        zeros = jnp.zeros((SUBLANES, N_STATE), f32)
        xs_re[0:SUBLANES, :] = zeros
        xs_im[0:SUBLANES, :] = zeros
        ve_ext[0:SUBLANES, :] = jnp.zeros((SUBLANES, CONV_WIDTH), f32)
        vo_ext[0:SUBLANES, :] = jnp.zeros((SUBLANES, CONV_WIDTH), f32)
        pre1_ref[...] = jnp.zeros_like(pre1_ref)

    @pl.when(i + 1 < N_TILES)
    def _():
        for cp in _tile_copies(x_hbm, xbuf, xsem, i + 1, 1 - slot, True):
            cp.start()

    @pl.when(i < N_TILES)
    def _():
        for cp in _tile_copies(x_hbm, xbuf, xsem, i, slot, True):
            cp.wait()

    @pl.when(ffn_tile >= 2)
    def _():
        for cp in _tile_copies(o_hbm, obuf, osem, ffn_tile - 2, ffn_slot, False):
            cp.wait()

    y, pre1 = _layer_tile(pre1_ref[...], _split_phases(xbuf[slot]), mixer_refs, ffn_refs,
                          xs_re, xs_im, ve_ext, vo_ext)
    obuf[ffn_slot] = _merge_phases(y)
    pre1_ref[...] = pre1

    @pl.when(i > 0)
    def _():
        for cp in _tile_copies(o_hbm, obuf, osem, ffn_tile, ffn_slot, False):
            cp.start()

    @pl.when(i == N_TILES)
    def _():
        for cp in _tile_copies(o_hbm, obuf, osem, ffn_tile - 1, 1 - ffn_slot, False):
            cp.wait()
        for cp in _tile_copies(o_hbm, obuf, osem, ffn_tile, ffn_slot, False):
            cp.wait()


def _const_spec(shape):
    nd = len(shape)
    return pl.BlockSpec(shape, lambda i, nd=nd: (0,) * nd, pipeline_mode=pl.Buffered(1))


def _ssm_prep_kernel(lre_ref, lim_ref, ldt_ref, bre_ref, bim_ref, cre_ref, cim_ref,
                     bmat_ref, a2_ref, wre_ref, wim_ref, kd_ref):
    lam_re, lam_im = lre_ref[...], lim_ref[...]
    dt = jnp.exp(ldt_ref[...])
    mag = jnp.exp(lam_re * dt)
    a_re = mag * jnp.cos(lam_im * dt)
    a_im = mag * jnp.sin(lam_im * dt)
    den = lam_re * lam_re + lam_im * lam_im
    num_re = a_re - 1.0
    fr = (num_re * lam_re + a_im * lam_im) / den
    fi = (a_im * lam_re - num_re * lam_im) / den
    a2_re = a_re * a_re - a_im * a_im
    a2_im = 2.0 * a_re * a_im
    a2_ref[0:1, :] = a2_re
    a2_ref[1:2, :] = a2_im

    def cmul(xr, xi, yr, yi):
        return xr * yr - xi * yi, xr * yi + xi * yr

    b_re, b_im, c_re, c_im = bre_ref[...], bim_ref[...], cre_ref[...], cim_ref[...]
    bb_re, bb_im = cmul(fr, fi, b_re, b_im)
    abb_re, abb_im = cmul(a_re, a_im, bb_re, bb_im)
    ca_re, ca_im = cmul(a_re, a_im, c_re, c_im)
    ca2_re, ca2_im = cmul(a2_re, a2_im, c_re, c_im)

    row_group = lax.broadcasted_iota(jnp.int32, (BLOCK_CH, BLOCK_ST), 0) // SSM_GROUP
    col_group = lax.broadcasted_iota(jnp.int32, (BLOCK_CH, BLOCK_ST), 1) // SSM_STATE
    diag = row_group == col_group

    for j in range(N_BLOCKS):
        def bd(m):
            blk = m[:, j * BLOCK_ST:(j + 1) * BLOCK_ST]
            return jnp.where(diag, jnp.concatenate([blk] * GROUPS_PER_BLOCK, axis=0), 0.0)

        def nt(x, y):
            return lax.dot_general(x, y, (((1,), (1,)), ((), ())), precision=lax.Precision.HIGHEST,
                                   preferred_element_type=f32)

        bbr, bbi, abr, abi = bd(bb_re), bd(bb_im), bd(abb_re), bd(abb_im)
        cr, ci = bd(c_re), bd(c_im)
        bmat_ref[j, 0:BLOCK_CH, 0:BLOCK_ST] = abr.astype(bf16)
        bmat_ref[j, 0:BLOCK_CH, BLOCK_ST:2 * BLOCK_ST] = abi.astype(bf16)
        bmat_ref[j, BLOCK_CH:2 * BLOCK_CH, 0:BLOCK_ST] = bbr.astype(bf16)
        bmat_ref[j, BLOCK_CH:2 * BLOCK_CH, BLOCK_ST:2 * BLOCK_ST] = bbi.astype(bf16)
        wre_ref[j, :, 0:BLOCK_CH] = bd(ca_re).T.astype(bf16)
        wre_ref[j, :, BLOCK_CH:2 * BLOCK_CH] = bd(ca2_re).T.astype(bf16)
        wim_ref[j, :, 0:BLOCK_CH] = (-bd(ca_im)).T.astype(bf16)
        wim_ref[j, :, BLOCK_CH:2 * BLOCK_CH] = (-bd(ca2_im)).T.astype(bf16)
        k0 = nt(bbr, cr) - nt(bbi, ci)
        k1 = nt(abr, cr) - nt(abi, ci)
        kd_ref[j, 0:BLOCK_CH, 0:BLOCK_CH] = k0.astype(bf16)
        kd_ref[j, 0:BLOCK_CH, BLOCK_CH:2 * BLOCK_CH] = k1.astype(bf16)
        kd_ref[j, BLOCK_CH:2 * BLOCK_CH, 0:BLOCK_CH] = jnp.zeros((BLOCK_CH, BLOCK_CH), bf16)
        kd_ref[j, BLOCK_CH:2 * BLOCK_CH, BLOCK_CH:2 * BLOCK_CH] = k0.astype(bf16)


def _ssm_params(lam_re, lam_im, log_dt, b_re, b_im, c_re, c_im):
    flat = lambda m: m.astype(f32).reshape(1, N_STATE)
    ldt = jnp.broadcast_to(log_dt.astype(f32)[:, None], (SSM_GROUPS, SSM_STATE))
    b_t = lambda m: jnp.transpose(m.astype(f32), (2, 0, 1)).reshape(SSM_GROUP, N_STATE)
    c_t = lambda m: jnp.transpose(m.astype(f32), (1, 0, 2)).reshape(SSM_GROUP, N_STATE)
    return pl.pallas_call(
        _ssm_prep_kernel,
        out_shape=[jax.ShapeDtypeStruct((N_BLOCKS, 2 * BLOCK_CH, 2 * BLOCK_ST), bf16),
                   jax.ShapeDtypeStruct((2, N_STATE), f32),
                   jax.ShapeDtypeStruct((N_BLOCKS, BLOCK_ST, 2 * BLOCK_CH), bf16),
                   jax.ShapeDtypeStruct((N_BLOCKS, BLOCK_ST, 2 * BLOCK_CH), bf16),
                   jax.ShapeDtypeStruct((N_BLOCKS, 2 * BLOCK_CH, 2 * BLOCK_CH), bf16)],
        name="ssm_prep",
    )(flat(lam_re), flat(lam_im), flat(ldt), b_t(b_re), b_t(b_im), c_t(c_re), c_t(c_im))


def _layer(x, w_in, b_in, lam_re, lam_im, log_dt, b_re, b_im, c_re, c_im, d_skip, glu_w, glu_b,
           w_ssm_out, conv_w, w_conv_out, w_o, ln1_g, ln1_b, w_gate, w_up, w_down, ln2_g, ln2_b):
    row = lambda v: v.reshape(1, -1).astype(f32)
    mixer_in = (w_in.astype(bf16), row(b_in),
                *_ssm_params(lam_re, lam_im, log_dt, b_re, b_im, c_re, c_im), row(d_skip),
                glu_w.astype(bf16), row(glu_b), w_ssm_out.astype(bf16), conv_w.astype(f32),
                w_conv_out.astype(bf16), w_o.astype(bf16), row(ln1_g), row(ln1_b))
    ffn_in = (w_gate.astype(bf16), w_up.astype(bf16), w_down.astype(bf16), row(ln2_g), row(ln2_b))
    assert len(mixer_in) == N_MIXER_PARAMS and len(ffn_in) == N_FFN_PARAMS
    return pl.pallas_call(
        _layer_kernel,
        grid=(N_TILES + 1,),
        in_specs=[pl.BlockSpec(memory_space=pl.ANY)]
                 + [_const_spec(a.shape) for a in mixer_in + ffn_in],
        out_specs=pl.BlockSpec(memory_space=pl.ANY),
        out_shape=jax.ShapeDtypeStruct((BATCH, SEQ, D_MODEL), f32),
        scratch_shapes=[pltpu.VMEM((2, T_TILE, BATCH, D_MODEL), f32),
                        pltpu.SemaphoreType.DMA((2, BATCH)),
                        pltpu.VMEM((2, T_TILE, BATCH, D_MODEL), f32),
                        pltpu.SemaphoreType.DMA((2, BATCH)),
                        pltpu.VMEM((R_TILE, D_MODEL), f32),
                        pltpu.VMEM((HALF + SUBLANES, N_STATE), f32),
                        pltpu.VMEM((HALF + SUBLANES, N_STATE), f32),
                        pltpu.VMEM((HALF + SUBLANES, CONV_WIDTH), f32),
                        pltpu.VMEM((HALF + SUBLANES, CONV_WIDTH), f32)],
        compiler_params=pltpu.CompilerParams(dimension_semantics=("arbitrary",),
                                             vmem_limit_bytes=VMEM_LIMIT_BYTES),
        name="layer",
    )(x, *mixer_in, *ffn_in)


def kernel(x, w_in, b_in, ssm_lambda_re, ssm_lambda_im, ssm_log_dt, ssm_b_re, ssm_b_im, ssm_c_re, ssm_c_im, ssm_d, glu_w, glu_b, w_ssm_out, conv_w, w_conv_out, w_o, ln1_g, ln1_b, w_gate, w_up, w_down, ln2_g, ln2_b):
    assert x.shape == (BATCH, SEQ, D_MODEL) and BATCH == SUBLANES
    for l in range(w_in.shape[0]):
        x = _layer(x, w_in[l], b_in[l], ssm_lambda_re[l], ssm_lambda_im[l], ssm_log_dt[l],
                   ssm_b_re[l], ssm_b_im[l], ssm_c_re[l], ssm_c_im[l], ssm_d[l], glu_w[l],
                   glu_b[l], w_ssm_out[l], conv_w[l], w_conv_out[l], w_o[l], ln1_g[l], ln1_b[l],
                   w_gate[l], w_up[l], w_down[l], ln2_g[l], ln2_b[l])
    return x
```

```python
import jax
import jax.numpy as jnp
from jax import lax
from jax.experimental import pallas as pl
from jax.experimental.pallas import tpu as pltpu

D_MODEL = 1024
BATCH = 8
SEQ = 4096
SSM_WIDTH = D_MODEL // 2
SSM_GROUP = 16
SSM_GROUPS = SSM_WIDTH // SSM_GROUP
SSM_STATE = 64
CONV_WIDTH = D_MODEL // 2
CONV_K = 3
FFN_HIDDEN = 2816
IN_COLS = SSM_WIDTH + 3 * CONV_WIDTH + 2 * D_MODEL
DEPTH = 1
ALPHA = (2.0 * DEPTH) ** 0.25
LN_EPS = 1e-5

SUBLANES = 8
VMEM_LIMIT_BYTES = 60 * 1024 * 1024

GROUPS_PER_BLOCK = 8
N_BLOCKS = SSM_GROUPS // GROUPS_PER_BLOCK
BLOCK_CH = GROUPS_PER_BLOCK * SSM_GROUP
BLOCK_ST = GROUPS_PER_BLOCK * SSM_STATE
N_STATE = SSM_GROUPS * SSM_STATE

PHASES = 2
T_TILE = 32
R_TILE = T_TILE * BATCH
H_TILE = T_TILE // PHASES
HALF = H_TILE * SUBLANES
N_TILES = SEQ // T_TILE
assert N_TILES >= 2 and SEQ % T_TILE == 0 and T_TILE % PHASES == 0

f32 = jnp.float32
bf16 = jnp.bfloat16


def _layer_norm(y, g, b):
    mu = jnp.mean(y, axis=-1, keepdims=True)
    var = jnp.mean(jnp.square(y - mu), axis=-1, keepdims=True)
    return (y - mu) * lax.rsqrt(var + LN_EPS) * g + b


def _tile_copies(hbm, buf, sem, tile, slot, to_vmem):
    copies = []
    for b in range(BATCH):
        h = hbm.at[b, pl.ds(tile * T_TILE, T_TILE), :]
        v = buf.at[slot, :, b, :]
        src, dst = (h, v) if to_vmem else (v, h)
        copies.append(pltpu.make_async_copy(src, dst, sem.at[slot, b]))
    return copies


def _split_phases(x3):
    x4 = x3.reshape(H_TILE, PHASES, SUBLANES, x3.shape[-1])
    return jnp.concatenate([x4[:, ph].reshape(HALF, x3.shape[-1]) for ph in range(PHASES)], axis=0)


def _merge_phases(y):
    parts = [y[ph * HALF:(ph + 1) * HALF].reshape(H_TILE, SUBLANES, y.shape[-1])
             for ph in range(PHASES)]
    return jnp.stack(parts, axis=1).reshape(T_TILE, SUBLANES, y.shape[-1])


def _layer_tile(pre1, x, mixer_refs, ffn_refs, xs_re, xs_im, ve_ext, vo_ext):
    (win_ref, bin_ref, bmat_ref, a2_ref, wre_ref, wim_ref, kd_ref, dskip_ref, gluw_ref, glub_ref,
     wssm_ref, convw_ref, wconv_ref, wo_ref, ln1g_ref, ln1b_ref) = mixer_refs
    wg_ref, wu_ref, wd_ref, ln2g_ref, ln2b_ref = ffn_refs
    dot = lambda a, w: jnp.dot(a, w, preferred_element_type=f32)
    x1 = _layer_norm(pre1, ln1g_ref[...], ln1b_ref[...])
    x1b = x1.astype(bf16)
    xb = x.astype(bf16)

    def proj(lo, hi):
        return dot(xb, win_ref[:, lo:hi]) + bin_ref[:, lo:hi]

    u = proj(0, SSM_WIDTH)
    h = proj(SSM_WIDTH, SSM_WIDTH + CONV_WIDTH)
    ub = u.astype(bf16)
    u2 = []
    for j in range(N_BLOCKS):
        cs = slice(j * BLOCK_CH, (j + 1) * BLOCK_CH)
        u2.append(jnp.concatenate([ub[:HALF, cs], ub[HALF:, cs]], axis=-1))
        bu = dot(u2[j], bmat_ref[j])
        xs_re[SUBLANES:SUBLANES + HALF, j * BLOCK_ST:(j + 1) * BLOCK_ST] = bu[:, :BLOCK_ST]
        xs_im[SUBLANES:SUBLANES + HALF, j * BLOCK_ST:(j + 1) * BLOCK_ST] = bu[:, BLOCK_ST:]

    o4 = SSM_WIDTH + 3 * CONV_WIDTH
    c_gate = proj(SSM_WIDTH + CONV_WIDTH, SSM_WIDTH + 2 * CONV_WIDTH)
    b_gate = proj(SSM_WIDTH + 2 * CONV_WIDTH, o4)
    gate_a = proj(o4, o4 + D_MODEL)
    gate_b = proj(o4 + D_MODEL, o4 + 2 * D_MODEL)

    for j in range(N_BLOCKS):
        sl = slice(j * BLOCK_ST, (j + 1) * BLOCK_ST)
        ar = jnp.broadcast_to(a2_ref[0:1, sl], (SUBLANES, BLOCK_ST))
        ai = jnp.broadcast_to(a2_ref[1:2, sl], (SUBLANES, BLOCK_ST))
        sr, si = xs_re[0:SUBLANES, sl], xs_im[0:SUBLANES, sl]
        for k in range(H_TILE):
            r = slice((k + 1) * SUBLANES, (k + 2) * SUBLANES)
            sr, si = (ar * sr - ai * si + xs_re[r, sl], ar * si + ai * sr + xs_im[r, sl])
            xs_re[r, sl] = sr
            xs_im[r, sl] = si

    gate = dot(x1b, wg_ref[...])

    y_e, y_o = [], []
    for j in range(N_BLOCKS):
        sl = slice(j * BLOCK_ST, (j + 1) * BLOCK_ST)
        y2 = (dot(xs_re[0:HALF, sl].astype(bf16), wre_ref[j])
              + dot(xs_im[0:HALF, sl].astype(bf16), wim_ref[j]) + dot(u2[j], kd_ref[j]))
        y_e.append(y2[:, :BLOCK_CH])
        y_o.append(y2[:, BLOCK_CH:])
    xs_re[0:SUBLANES, :] = xs_re[HALF:HALF + SUBLANES, :]
    xs_im[0:SUBLANES, :] = xs_im[HALF:HALF + SUBLANES, :]
    y_a = jnp.concatenate([jnp.concatenate(y_e, axis=-1), jnp.concatenate(y_o, axis=-1)], axis=0)
    y_a = y_a + dskip_ref[...] * u

    up = dot(x1b, wu_ref[...])

    v = c_gate * h
    ve_ext[SUBLANES:SUBLANES + HALF, :] = v[:HALF]
    vo_ext[SUBLANES:SUBLANES + HALF, :] = v[HALF:]
    w0, w1, w2 = convw_ref[0:1, :], convw_ref[1:2, :], convw_ref[2:3, :]
    z_e = w0 * ve_ext[0:HALF, :] + w1 * vo_ext[0:HALF, :] + w2 * v[:HALF]
    z_o = w0 * vo_ext[0:HALF, :] + w1 * v[:HALF] + w2 * v[HALF:]
    ve_ext[0:SUBLANES, :] = ve_ext[HALF:HALF + SUBLANES, :]
    vo_ext[0:SUBLANES, :] = vo_ext[HALF:HALF + SUBLANES, :]
    bz = (b_gate * jnp.concatenate([z_e, z_o], axis=0)).astype(bf16)

    g = jax.nn.gelu(y_a)
    glu = dot(g.astype(bf16), gluw_ref[...]) + glub_ref[...]
    y_b = dot(bz, wconv_ref[...])

    hid = (jax.nn.silu(gate) * up).astype(bf16)
    ffn_lo = dot(hid, wd_ref[:, 0:D_MODEL // 2])
    y_a = dot((g * jax.nn.sigmoid(glu)).astype(bf16), wssm_ref[...])
    ffn_hi = dot(hid, wd_ref[:, D_MODEL // 2:D_MODEL])
    ffn = jnp.concatenate([ffn_lo, ffn_hi], axis=-1)
    merged = jax.nn.sigmoid(gate_a) * y_a + jax.nn.sigmoid(gate_b) * y_b
    mix = dot(merged.astype(bf16), wo_ref[...])
    out = _layer_norm(ALPHA * x1 + ffn, ln2g_ref[...], ln2b_ref[...])
    return out, ALPHA * x + mix


N_MIXER_PARAMS = 16
N_FFN_PARAMS = 5


def _layer_kernel(x_hbm, *refs):
    mixer_refs = refs[:N_MIXER_PARAMS]
    ffn_refs = refs[N_MIXER_PARAMS:N_MIXER_PARAMS + N_FFN_PARAMS]
    o_hbm, xbuf, xsem, obuf, osem, pre1_ref, xs_re, xs_im, ve_ext, vo_ext = \
        refs[N_MIXER_PARAMS + N_FFN_PARAMS:]
    i = pl.program_id(0)
    slot = i % 2
    ffn_tile = i - 1
    ffn_slot = (i + 1) % 2

    @pl.when(i == 0)
    def _():
        for cp in _tile_copies(x_hbm, xbuf, xsem, 0, 0, True):
            cp.start()
        zeros = jnp.zeros((SUBLANES, N_STATE), f32)
        xs_re[0:SUBLANES, :] = zeros
        xs_im[0:SUBLANES, :] = zeros
        ve_ext[0:SUBLANES, :] = jnp.zeros((SUBLANES, CONV_WIDTH), f32)
        vo_ext[0:SUBLANES, :] = jnp.zeros((SUBLANES, CONV_WIDTH), f32)
        pre1_ref[...] = jnp.zeros_like(pre1_ref)

    @pl.when(i + 1 < N_TILES)
    def _():
        for cp in _tile_copies(x_hbm, xbuf, xsem, i + 1, 1 - slot, True):
            cp.start()

    @pl.when(i < N_TILES)
    def _():
        for cp in _tile_copies(x_hbm, xbuf, xsem, i, slot, True):
            cp.wait()

    @pl.when(ffn_tile >= 2)
    def _():
        for cp in _tile_copies(o_hbm, obuf, osem, ffn_tile - 2, ffn_slot, False):
            cp.wait()

    y, pre1 = _layer_tile(pre1_ref[...], _split_phases(xbuf[slot]), mixer_refs, ffn_refs,
                          xs_re, xs_im, ve_ext, vo_ext)
    obuf[ffn_slot] = _merge_phases(y)
    pre1_ref[...] = pre1

    @pl.when(i > 0)
    def _():
        for cp in _tile_copies(o_hbm, obuf, osem, ffn_tile, ffn_slot, False):
            cp.start()

    @pl.when(i == N_TILES)
    def _():
        for cp in _tile_copies(o_hbm, obuf, osem, ffn_tile - 1, 1 - ffn_slot, False):
            cp.wait()
        for cp in _tile_copies(o_hbm, obuf, osem, ffn_tile, ffn_slot, False):
            cp.wait()


def _const_spec(shape):
    nd = len(shape)
    return pl.BlockSpec(shape, lambda i, nd=nd: (0,) * nd, pipeline_mode=pl.Buffered(1))


def _ssm_prep_kernel(lre_ref, lim_ref, ldt_ref, bre_ref, bim_ref, cre_ref, cim_ref,
                     bmat_ref, a2_ref, wre_ref, wim_ref, kd_ref):
    lam_re, lam_im = lre_ref[...], lim_ref[...]
    dt = jnp.exp(ldt_ref[...])
    mag = jnp.exp(lam_re * dt)
    a_re = mag * jnp.cos(lam_im * dt)
    a_im = mag * jnp.sin(lam_im * dt)
    den = lam_re * lam_re + lam_im * lam_im
    num_re = a_re - 1.0
    fr = (num_re * lam_re + a_im * lam_im) / den
    fi = (a_im * lam_re - num_re * lam_im) / den
    a2_re = a_re * a_re - a_im * a_im
    a2_im = 2.0 * a_re * a_im
    a2_ref[0:1, :] = a2_re
    a2_ref[1:2, :] = a2_im

    def cmul(xr, xi, yr, yi):
        return xr * yr - xi * yi, xr * yi + xi * yr

    b_re, b_im, c_re, c_im = bre_ref[...], bim_ref[...], cre_ref[...], cim_ref[...]
    bb_re, bb_im = cmul(fr, fi, b_re, b_im)
    abb_re, abb_im = cmul(a_re, a_im, bb_re, bb_im)
    ca_re, ca_im = cmul(a_re, a_im, c_re, c_im)
    ca2_re, ca2_im = cmul(a2_re, a2_im, c_re, c_im)

    row_group = lax.broadcasted_iota(jnp.int32, (BLOCK_CH, BLOCK_ST), 0) // SSM_GROUP
    col_group = lax.broadcasted_iota(jnp.int32, (BLOCK_CH, BLOCK_ST), 1) // SSM_STATE
    diag = row_group == col_group

    for j in range(N_BLOCKS):
        def bd(m):
            blk = m[:, j * BLOCK_ST:(j + 1) * BLOCK_ST]
            return jnp.where(diag, jnp.concatenate([blk] * GROUPS_PER_BLOCK, axis=0), 0.0)

        def nt(x, y):
            return lax.dot_general(x, y, (((1,), (1,)), ((), ())), precision=lax.Precision.HIGHEST,
                                   preferred_element_type=f32)

        bbr, bbi, abr, abi = bd(bb_re), bd(bb_im), bd(abb_re), bd(abb_im)
        cr, ci = bd(c_re), bd(c_im)
        bmat_ref[j, 0:BLOCK_CH, 0:BLOCK_ST] = abr.astype(bf16)
        bmat_ref[j, 0:BLOCK_CH, BLOCK_ST:2 * BLOCK_ST] = abi.astype(bf16)
        bmat_ref[j, BLOCK_CH:2 * BLOCK_CH, 0:BLOCK_ST] = bbr.astype(bf16)
        bmat_ref[j, BLOCK_CH:2 * BLOCK_CH, BLOCK_ST:2 * BLOCK_ST] = bbi.astype(bf16)
        wre_ref[j, :, 0:BLOCK_CH] = bd(ca_re).T.astype(bf16)
        wre_ref[j, :, BLOCK_CH:2 * BLOCK_CH] = bd(ca2_re).T.astype(bf16)
        wim_ref[j, :, 0:BLOCK_CH] = (-bd(ca_im)).T.astype(bf16)
        wim_ref[j, :, BLOCK_CH:2 * BLOCK_CH] = (-bd(ca2_im)).T.astype(bf16)
        k0 = nt(bbr, cr) - nt(bbi, ci)
        k1 = nt(abr, cr) - nt(abi, ci)
        kd_ref[j, 0:BLOCK_CH, 0:BLOCK_CH] = k0.astype(bf16)
        kd_ref[j, 0:BLOCK_CH, BLOCK_CH:2 * BLOCK_CH] = k1.astype(bf16)
        kd_ref[j, BLOCK_CH:2 * BLOCK_CH, 0:BLOCK_CH] = jnp.zeros((BLOCK_CH, BLOCK_CH), bf16)
        kd_ref[j, BLOCK_CH:2 * BLOCK_CH, BLOCK_CH:2 * BLOCK_CH] = k0.astype(bf16)


def _ssm_params(lam_re, lam_im, log_dt, b_re, b_im, c_re, c_im):
    flat = lambda m: m.astype(f32).reshape(1, N_STATE)
    ldt = jnp.broadcast_to(log_dt.astype(f32)[:, None], (SSM_GROUPS, SSM_STATE))
    b_t = lambda m: jnp.transpose(m.astype(f32), (2, 0, 1)).reshape(SSM_GROUP, N_STATE)
    c_t = lambda m: jnp.transpose(m.astype(f32), (1, 0, 2)).reshape(SSM_GROUP, N_STATE)
    return pl.pallas_call(
        _ssm_prep_kernel,
        out_shape=[jax.ShapeDtypeStruct((N_BLOCKS, 2 * BLOCK_CH, 2 * BLOCK_ST), bf16),
                   jax.ShapeDtypeStruct((2, N_STATE), f32),
                   jax.ShapeDtypeStruct((N_BLOCKS, BLOCK_ST, 2 * BLOCK_CH), bf16),
                   jax.ShapeDtypeStruct((N_BLOCKS, BLOCK_ST, 2 * BLOCK_CH), bf16),
                   jax.ShapeDtypeStruct((N_BLOCKS, 2 * BLOCK_CH, 2 * BLOCK_CH), bf16)],
        name="ssm_prep",
    )(flat(lam_re), flat(lam_im), flat(ldt), b_t(b_re), b_t(b_im), c_t(c_re), c_t(c_im))


def _layer(x, w_in, b_in, lam_re, lam_im, log_dt, b_re, b_im, c_re, c_im, d_skip, glu_w, glu_b,
           w_ssm_out, conv_w, w_conv_out, w_o, ln1_g, ln1_b, w_gate, w_up, w_down, ln2_g, ln2_b):
    row = lambda v: v.reshape(1, -1).astype(f32)
    mixer_in = (w_in.astype(bf16), row(b_in),
                *_ssm_params(lam_re, lam_im, log_dt, b_re, b_im, c_re, c_im), row(d_skip),
                glu_w.astype(bf16), row(glu_b), w_ssm_out.astype(bf16), conv_w.astype(f32),
                w_conv_out.astype(bf16), w_o.astype(bf16), row(ln1_g), row(ln1_b))
    ffn_in = (w_gate.astype(bf16), w_up.astype(bf16), w_down.astype(bf16), row(ln2_g), row(ln2_b))
    assert len(mixer_in) == N_MIXER_PARAMS and len(ffn_in) == N_FFN_PARAMS
    return pl.pallas_call(
        _layer_kernel,
        grid=(N_TILES + 1,),
        in_specs=[pl.BlockSpec(memory_space=pl.ANY)]
                 + [_const_spec(a.shape) for a in mixer_in + ffn_in],
        out_specs=pl.BlockSpec(memory_space=pl.ANY),
        out_shape=jax.ShapeDtypeStruct((BATCH, SEQ, D_MODEL), f32),
        scratch_shapes=[pltpu.VMEM((2, T_TILE, BATCH, D_MODEL), f32),
                        pltpu.SemaphoreType.DMA((2, BATCH)),
                        pltpu.VMEM((2, T_TILE, BATCH, D_MODEL), f32),
                        pltpu.SemaphoreType.DMA((2, BATCH)),
                        pltpu.VMEM((R_TILE, D_MODEL), f32),
                        pltpu.VMEM((HALF + SUBLANES, N_STATE), f32),
                        pltpu.VMEM((HALF + SUBLANES, N_STATE), f32),
                        pltpu.VMEM((HALF + SUBLANES, CONV_WIDTH), f32),
                        pltpu.VMEM((HALF + SUBLANES, CONV_WIDTH), f32)],
        compiler_params=pltpu.CompilerParams(dimension_semantics=("arbitrary",),
                                             vmem_limit_bytes=VMEM_LIMIT_BYTES),
        name="layer",
    )(x, *mixer_in, *ffn_in)


def kernel(x, w_in, b_in, ssm_lambda_re, ssm_lambda_im, ssm_log_dt, ssm_b_re, ssm_b_im, ssm_c_re, ssm_c_im, ssm_d, glu_w, glu_b, w_ssm_out, conv_w, w_conv_out, w_o, ln1_g, ln1_b, w_gate, w_up, w_down, ln2_g, ln2_b):
    assert x.shape == (BATCH, SEQ, D_MODEL) and BATCH == SUBLANES
    for l in range(w_in.shape[0]):
        x = _layer(x, w_in[l], b_in[l], ssm_lambda_re[l], ssm_lambda_im[l], ssm_log_dt[l],
                   ssm_b_re[l], ssm_b_im[l], ssm_c_re[l], ssm_c_im[l], ssm_d[l], glu_w[l],
                   glu_b[l], w_ssm_out[l], conv_w[l], w_conv_out[l], w_o[l], ln1_g[l], ln1_b[l],
                   w_gate[l], w_up[l], w_down[l], ln2_g[l], ln2_b[l])
    return x
```

```python
import jax
import jax.numpy as jnp
from jax import lax
from jax.experimental import pallas as pl
from jax.experimental.pallas import tpu as pltpu

D_MODEL = 1024
BATCH = 8
SEQ = 4096
SSM_WIDTH = D_MODEL // 2
SSM_GROUP = 16
SSM_GROUPS = SSM_WIDTH // SSM_GROUP
SSM_STATE = 64
CONV_WIDTH = D_MODEL // 2
CONV_K = 3
FFN_HIDDEN = 2816
IN_COLS = SSM_WIDTH + 3 * CONV_WIDTH + 2 * D_MODEL
DEPTH = 1
ALPHA = (2.0 * DEPTH) ** 0.25
LN_EPS = 1e-5

SUBLANES = 8
VMEM_LIMIT_BYTES = 63 * 1024 * 1024

GROUPS_PER_BLOCK = 8
N_BLOCKS = SSM_GROUPS // GROUPS_PER_BLOCK
BLOCK_CH = GROUPS_PER_BLOCK * SSM_GROUP
BLOCK_ST = GROUPS_PER_BLOCK * SSM_STATE
N_STATE = SSM_GROUPS * SSM_STATE

PHASES = 2
T_TILE = 64
R_TILE = T_TILE * BATCH
H_TILE = T_TILE // PHASES
HALF = H_TILE * SUBLANES
N_TILES = SEQ // T_TILE
assert N_TILES >= 2 and SEQ % T_TILE == 0 and T_TILE % PHASES == 0

f32 = jnp.float32
bf16 = jnp.bfloat16


def _layer_norm(y, g, b):
    mu = jnp.mean(y, axis=-1, keepdims=True)
    var = jnp.mean(jnp.square(y - mu), axis=-1, keepdims=True)
    return (y - mu) * lax.rsqrt(var + LN_EPS) * g + b


def _tile_copies(hbm, buf, sem, tile, slot, to_vmem):
    copies = []
    for b in range(BATCH):
        h = hbm.at[b, pl.ds(tile * T_TILE, T_TILE), :]
        v = buf.at[slot, :, b, :]
        src, dst = (h, v) if to_vmem else (v, h)
        copies.append(pltpu.make_async_copy(src, dst, sem.at[slot, b]))
    return copies


def _split_phases(x3):
    x4 = x3.reshape(H_TILE, PHASES, SUBLANES, x3.shape[-1])
    return jnp.concatenate([x4[:, ph].reshape(HALF, x3.shape[-1]) for ph in range(PHASES)], axis=0)


def _merge_phases(y):
    parts = [y[ph * HALF:(ph + 1) * HALF].reshape(H_TILE, SUBLANES, y.shape[-1])
             for ph in range(PHASES)]
    return jnp.stack(parts, axis=1).reshape(T_TILE, SUBLANES, y.shape[-1])


def _layer_tile(pre1, x, mixer_refs, ffn_refs, xs_re, xs_im, ve_ext, vo_ext):
    (win_ref, bin_ref, bmat_ref, a2_ref, wre_ref, wim_ref, kd_ref, dskip_ref, gluw_ref, glub_ref,
     wssm_ref, convw_ref, wconv_ref, wo_ref, ln1g_ref, ln1b_ref) = mixer_refs
    wg_ref, wu_ref, wd_ref, ln2g_ref, ln2b_ref = ffn_refs
    dot = lambda a, w: jnp.dot(a, w, preferred_element_type=f32)
    x1 = _layer_norm(pre1, ln1g_ref[...], ln1b_ref[...])
    x1b = x1.astype(bf16)
    xb = x.astype(bf16)

    def proj(lo, hi):
        return dot(xb, win_ref[:, lo:hi]) + bin_ref[:, lo:hi]

    u = proj(0, SSM_WIDTH)
    h = proj(SSM_WIDTH, SSM_WIDTH + CONV_WIDTH)
    ub = u.astype(bf16)
    u2 = []
    for j in range(N_BLOCKS):
        cs = slice(j * BLOCK_CH, (j + 1) * BLOCK_CH)
        u2.append(jnp.concatenate([ub[:HALF, cs], ub[HALF:, cs]], axis=-1))
        bu = dot(u2[j], bmat_ref[j])
        xs_re[SUBLANES:SUBLANES + HALF, j * BLOCK_ST:(j + 1) * BLOCK_ST] = bu[:, :BLOCK_ST]
        xs_im[SUBLANES:SUBLANES + HALF, j * BLOCK_ST:(j + 1) * BLOCK_ST] = bu[:, BLOCK_ST:]

    o4 = SSM_WIDTH + 3 * CONV_WIDTH
    c_gate = proj(SSM_WIDTH + CONV_WIDTH, SSM_WIDTH + 2 * CONV_WIDTH)
    b_gate = proj(SSM_WIDTH + 2 * CONV_WIDTH, o4)
    gate_a = proj(o4, o4 + D_MODEL)
    gate_b = proj(o4 + D_MODEL, o4 + 2 * D_MODEL)

    for j in range(N_BLOCKS):
        sl = slice(j * BLOCK_ST, (j + 1) * BLOCK_ST)
        ar = jnp.broadcast_to(a2_ref[0:1, sl], (SUBLANES, BLOCK_ST))
        ai = jnp.broadcast_to(a2_ref[1:2, sl], (SUBLANES, BLOCK_ST))
        sr, si = xs_re[0:SUBLANES, sl], xs_im[0:SUBLANES, sl]
        for k in range(H_TILE):
            r = slice((k + 1) * SUBLANES, (k + 2) * SUBLANES)
            sr, si = (ar * sr - ai * si + xs_re[r, sl], ar * si + ai * sr + xs_im[r, sl])
            xs_re[r, sl] = sr
            xs_im[r, sl] = si

    gate = dot(x1b, wg_ref[...])

    y_e, y_o = [], []
    for j in range(N_BLOCKS):
        sl = slice(j * BLOCK_ST, (j + 1) * BLOCK_ST)
        y2 = (dot(xs_re[0:HALF, sl].astype(bf16), wre_ref[j])
              + dot(xs_im[0:HALF, sl].astype(bf16), wim_ref[j]) + dot(u2[j], kd_ref[j]))
        y_e.append(y2[:, :BLOCK_CH])
        y_o.append(y2[:, BLOCK_CH:])
    xs_re[0:SUBLANES, :] = xs_re[HALF:HALF + SUBLANES, :]
    xs_im[0:SUBLANES, :] = xs_im[HALF:HALF + SUBLANES, :]
    y_a = jnp.concatenate([jnp.concatenate(y_e, axis=-1), jnp.concatenate(y_o, axis=-1)], axis=0)
    y_a = y_a + dskip_ref[...] * u

    up = dot(x1b, wu_ref[...])

    v = c_gate * h
    ve_ext[SUBLANES:SUBLANES + HALF, :] = v[:HALF]
    vo_ext[SUBLANES:SUBLANES + HALF, :] = v[HALF:]
    w0, w1, w2 = convw_ref[0:1, :], convw_ref[1:2, :], convw_ref[2:3, :]
    z_e = w0 * ve_ext[0:HALF, :] + w1 * vo_ext[0:HALF, :] + w2 * v[:HALF]
    z_o = w0 * vo_ext[0:HALF, :] + w1 * v[:HALF] + w2 * v[HALF:]
    ve_ext[0:SUBLANES, :] = ve_ext[HALF:HALF + SUBLANES, :]
    vo_ext[0:SUBLANES, :] = vo_ext[HALF:HALF + SUBLANES, :]
    bz = (b_gate * jnp.concatenate([z_e, z_o], axis=0)).astype(bf16)

    g = jax.nn.gelu(y_a)
    glu = dot(g.astype(bf16), gluw_ref[...]) + glub_ref[...]
    y_b = dot(bz, wconv_ref[...])

    hid = (jax.nn.silu(gate) * up).astype(bf16)
    ffn_lo = dot(hid, wd_ref[:, 0:D_MODEL // 2])
    y_a = dot((g * jax.nn.sigmoid(glu)).astype(bf16), wssm_ref[...])
    ffn_hi = dot(hid, wd_ref[:, D_MODEL // 2:D_MODEL])
    ffn = jnp.concatenate([ffn_lo, ffn_hi], axis=-1)
    merged = jax.nn.sigmoid(gate_a) * y_a + jax.nn.sigmoid(gate_b) * y_b
    mix = dot(merged.astype(bf16), wo_ref[...])
    out = _layer_norm(ALPHA * x1 + ffn, ln2g_ref[...], ln2b_ref[...])
    return out, ALPHA * x + mix


N_MIXER_PARAMS = 16
N_FFN_PARAMS = 5


def _layer_kernel(x_hbm, *refs):
    mixer_refs = refs[:N_MIXER_PARAMS]
    ffn_refs = refs[N_MIXER_PARAMS:N_MIXER_PARAMS + N_FFN_PARAMS]
    o_hbm, xbuf, xsem, obuf, osem, pre1_ref, xs_re, xs_im, ve_ext, vo_ext = \
        refs[N_MIXER_PARAMS + N_FFN_PARAMS:]
    i = pl.program_id(0)
    slot = i % 2
    ffn_tile = i - 1

    @pl.when(i == 0)
    def _():
        for cp in _tile_copies(x_hbm, xbuf, xsem, 0, 0, True):
            cp.start()
        zeros = jnp.zeros((SUBLANES, N_STATE), f32)
        xs_re[0:SUBLANES, :] = zeros
        xs_im[0:SUBLANES, :] = zeros
        ve_ext[0:SUBLANES, :] = jnp.zeros((SUBLANES, CONV_WIDTH), f32)
        vo_ext[0:SUBLANES, :] = jnp.zeros((SUBLANES, CONV_WIDTH), f32)
        pre1_ref[...] = jnp.zeros_like(pre1_ref)

    @pl.when(i + 1 < N_TILES)
    def _():
        for cp in _tile_copies(x_hbm, xbuf, xsem, i + 1, 1 - slot, True):
            cp.start()

    @pl.when(i < N_TILES)
    def _():
        for cp in _tile_copies(x_hbm, xbuf, xsem, i, slot, True):
            cp.wait()

    @pl.when(ffn_tile >= 1)
    def _():
        for cp in _tile_copies(o_hbm, obuf, osem, ffn_tile - 1, 0, False):
            cp.wait()

    y, pre1 = _layer_tile(pre1_ref[...], _split_phases(xbuf[slot]), mixer_refs, ffn_refs,
                          xs_re, xs_im, ve_ext, vo_ext)
    obuf[0] = _merge_phases(y)
    pre1_ref[...] = pre1

    @pl.when(i > 0)
    def _():
        for cp in _tile_copies(o_hbm, obuf, osem, ffn_tile, 0, False):
            cp.start()

    @pl.when(i == N_TILES)
    def _():
        for cp in _tile_copies(o_hbm, obuf, osem, ffn_tile, 0, False):
            cp.wait()


def _const_spec(shape):
    nd = len(shape)
    return pl.BlockSpec(shape, lambda i, nd=nd: (0,) * nd, pipeline_mode=pl.Buffered(1))


def _ssm_prep_kernel(lre_ref, lim_ref, ldt_ref, bre_ref, bim_ref, cre_ref, cim_ref,
                     bmat_ref, a2_ref, wre_ref, wim_ref, kd_ref):
    lam_re, lam_im = lre_ref[...], lim_ref[...]
    dt = jnp.exp(ldt_ref[...])
    mag = jnp.exp(lam_re * dt)
    a_re = mag * jnp.cos(lam_im * dt)
    a_im = mag * jnp.sin(lam_im * dt)
    den = lam_re * lam_re + lam_im * lam_im
    num_re = a_re - 1.0
    fr = (num_re * lam_re + a_im * lam_im) / den
    fi = (a_im * lam_re - num_re * lam_im) / den
    a2_re = a_re * a_re - a_im * a_im
    a2_im = 2.0 * a_re * a_im
    a2_ref[0:1, :] = a2_re
    a2_ref[1:2, :] = a2_im

    def cmul(xr, xi, yr, yi):
        return xr * yr - xi * yi, xr * yi + xi * yr

    b_re, b_im, c_re, c_im = bre_ref[...], bim_ref[...], cre_ref[...], cim_ref[...]
    bb_re, bb_im = cmul(fr, fi, b_re, b_im)
    abb_re, abb_im = cmul(a_re, a_im, bb_re, bb_im)
    ca_re, ca_im = cmul(a_re, a_im, c_re, c_im)
    ca2_re, ca2_im = cmul(a2_re, a2_im, c_re, c_im)

    row_group = lax.broadcasted_iota(jnp.int32, (BLOCK_CH, BLOCK_ST), 0) // SSM_GROUP
    col_group = lax.broadcasted_iota(jnp.int32, (BLOCK_CH, BLOCK_ST), 1) // SSM_STATE
    diag = row_group == col_group

    for j in range(N_BLOCKS):
        def bd(m):
            blk = m[:, j * BLOCK_ST:(j + 1) * BLOCK_ST]
            return jnp.where(diag, jnp.concatenate([blk] * GROUPS_PER_BLOCK, axis=0), 0.0)

        def nt(x, y):
            return lax.dot_general(x, y, (((1,), (1,)), ((), ())), precision=lax.Precision.HIGHEST,
                                   preferred_element_type=f32)

        bbr, bbi, abr, abi = bd(bb_re), bd(bb_im), bd(abb_re), bd(abb_im)
        cr, ci = bd(c_re), bd(c_im)
        bmat_ref[j, 0:BLOCK_CH, 0:BLOCK_ST] = abr.astype(bf16)
        bmat_ref[j, 0:BLOCK_CH, BLOCK_ST:2 * BLOCK_ST] = abi.astype(bf16)
        bmat_ref[j, BLOCK_CH:2 * BLOCK_CH, 0:BLOCK_ST] = bbr.astype(bf16)
        bmat_ref[j, BLOCK_CH:2 * BLOCK_CH, BLOCK_ST:2 * BLOCK_ST] = bbi.astype(bf16)
        wre_ref[j, :, 0:BLOCK_CH] = bd(ca_re).T.astype(bf16)
        wre_ref[j, :, BLOCK_CH:2 * BLOCK_CH] = bd(ca2_re).T.astype(bf16)
        wim_ref[j, :, 0:BLOCK_CH] = (-bd(ca_im)).T.astype(bf16)
        wim_ref[j, :, BLOCK_CH:2 * BLOCK_CH] = (-bd(ca2_im)).T.astype(bf16)
        k0 = nt(bbr, cr) - nt(bbi, ci)
        k1 = nt(abr, cr) - nt(abi, ci)
        kd_ref[j, 0:BLOCK_CH, 0:BLOCK_CH] = k0.astype(bf16)
        kd_ref[j, 0:BLOCK_CH, BLOCK_CH:2 * BLOCK_CH] = k1.astype(bf16)
        kd_ref[j, BLOCK_CH:2 * BLOCK_CH, 0:BLOCK_CH] = jnp.zeros((BLOCK_CH, BLOCK_CH), bf16)
        kd_ref[j, BLOCK_CH:2 * BLOCK_CH, BLOCK_CH:2 * BLOCK_CH] = k0.astype(bf16)


def _ssm_params(lam_re, lam_im, log_dt, b_re, b_im, c_re, c_im):
    flat = lambda m: m.astype(f32).reshape(1, N_STATE)
    ldt = jnp.broadcast_to(log_dt.astype(f32)[:, None], (SSM_GROUPS, SSM_STATE))
    b_t = lambda m: jnp.transpose(m.astype(f32), (2, 0, 1)).reshape(SSM_GROUP, N_STATE)
    c_t = lambda m: jnp.transpose(m.astype(f32), (1, 0, 2)).reshape(SSM_GROUP, N_STATE)
    return pl.pallas_call(
        _ssm_prep_kernel,
        out_shape=[jax.ShapeDtypeStruct((N_BLOCKS, 2 * BLOCK_CH, 2 * BLOCK_ST), bf16),
                   jax.ShapeDtypeStruct((2, N_STATE), f32),
                   jax.ShapeDtypeStruct((N_BLOCKS, BLOCK_ST, 2 * BLOCK_CH), bf16),
                   jax.ShapeDtypeStruct((N_BLOCKS, BLOCK_ST, 2 * BLOCK_CH), bf16),
                   jax.ShapeDtypeStruct((N_BLOCKS, 2 * BLOCK_CH, 2 * BLOCK_CH), bf16)],
        name="ssm_prep",
    )(flat(lam_re), flat(lam_im), flat(ldt), b_t(b_re), b_t(b_im), c_t(c_re), c_t(c_im))


def _layer(x, w_in, b_in, lam_re, lam_im, log_dt, b_re, b_im, c_re, c_im, d_skip, glu_w, glu_b,
           w_ssm_out, conv_w, w_conv_out, w_o, ln1_g, ln1_b, w_gate, w_up, w_down, ln2_g, ln2_b):
    row = lambda v: v.reshape(1, -1).astype(f32)
    mixer_in = (w_in.astype(bf16), row(b_in),
                *_ssm_params(lam_re, lam_im, log_dt, b_re, b_im, c_re, c_im), row(d_skip),
                glu_w.astype(bf16), row(glu_b), w_ssm_out.astype(bf16), conv_w.astype(f32),
                w_conv_out.astype(bf16), w_o.astype(bf16), row(ln1_g), row(ln1_b))
    ffn_in = (w_gate.astype(bf16), w_up.astype(bf16), w_down.astype(bf16), row(ln2_g), row(ln2_b))
    assert len(mixer_in) == N_MIXER_PARAMS and len(ffn_in) == N_FFN_PARAMS
    return pl.pallas_call(
        _layer_kernel,
        grid=(N_TILES + 1,),
        in_specs=[pl.BlockSpec(memory_space=pl.ANY)]
                 + [_const_spec(a.shape) for a in mixer_in + ffn_in],
        out_specs=pl.BlockSpec(memory_space=pl.ANY),
        out_shape=jax.ShapeDtypeStruct((BATCH, SEQ, D_MODEL), f32),
        scratch_shapes=[pltpu.VMEM((2, T_TILE, BATCH, D_MODEL), f32),
                        pltpu.SemaphoreType.DMA((2, BATCH)),
                        pltpu.VMEM((1, T_TILE, BATCH, D_MODEL), f32),
                        pltpu.SemaphoreType.DMA((1, BATCH)),
                        pltpu.VMEM((R_TILE, D_MODEL), f32),
                        pltpu.VMEM((HALF + SUBLANES, N_STATE), f32),
                        pltpu.VMEM((HALF + SUBLANES, N_STATE), f32),
                        pltpu.VMEM((HALF + SUBLANES, CONV_WIDTH), f32),
                        pltpu.VMEM((HALF + SUBLANES, CONV_WIDTH), f32)],
        compiler_params=pltpu.CompilerParams(dimension_semantics=("arbitrary",),
                                             vmem_limit_bytes=VMEM_LIMIT_BYTES),
        name="layer",
    )(x, *mixer_in, *ffn_in)


def kernel(x, w_in, b_in, ssm_lambda_re, ssm_lambda_im, ssm_log_dt, ssm_b_re, ssm_b_im, ssm_c_re, ssm_c_im, ssm_d, glu_w, glu_b, w_ssm_out, conv_w, w_conv_out, w_o, ln1_g, ln1_b, w_gate, w_up, w_down, ln2_g, ln2_b):
    assert x.shape == (BATCH, SEQ, D_MODEL) and BATCH == SUBLANES
    for l in range(w_in.shape[0]):
        x = _layer(x, w_in[l], b_in[l], ssm_lambda_re[l], ssm_lambda_im[l], ssm_log_dt[l],
                   ssm_b_re[l], ssm_b_im[l], ssm_c_re[l], ssm_c_im[l], ssm_d[l], glu_w[l],
                   glu_b[l], w_ssm_out[l], conv_w[l], w_conv_out[l], w_o[l], ln1_g[l], ln1_b[l],
                   w_gate[l], w_up[l], w_down[l], ln2_g[l], ln2_b[l])
    return x
```

```python
import jax
import jax.numpy as jnp
from jax import lax
from jax.experimental import pallas as pl
from jax.experimental.pallas import tpu as pltpu

D_MODEL = 1024
BATCH = 8
SEQ = 4096
SSM_WIDTH = D_MODEL // 2
SSM_GROUP = 16
SSM_GROUPS = SSM_WIDTH // SSM_GROUP
SSM_STATE = 64
CONV_WIDTH = D_MODEL // 2
CONV_K = 3
FFN_HIDDEN = 2816
IN_COLS = SSM_WIDTH + 3 * CONV_WIDTH + 2 * D_MODEL
DEPTH = 1
ALPHA = (2.0 * DEPTH) ** 0.25
LN_EPS = 1e-5

SUBLANES = 8
VMEM_LIMIT_BYTES = 60 * 1024 * 1024

GROUPS_PER_BLOCK = 8
N_BLOCKS = SSM_GROUPS // GROUPS_PER_BLOCK
BLOCK_CH = GROUPS_PER_BLOCK * SSM_GROUP
BLOCK_ST = GROUPS_PER_BLOCK * SSM_STATE
N_STATE = SSM_GROUPS * SSM_STATE

PHASES = 2
T_TILE = 32
R_TILE = T_TILE * BATCH
H_TILE = T_TILE // PHASES
HALF = H_TILE * SUBLANES
N_TILES = SEQ // T_TILE
assert N_TILES >= 2 and SEQ % T_TILE == 0 and T_TILE % PHASES == 0

f32 = jnp.float32
bf16 = jnp.bfloat16


def _layer_norm(y, g, b):
    mu = jnp.mean(y, axis=-1, keepdims=True)
    var = jnp.mean(jnp.square(y - mu), axis=-1, keepdims=True)
    return (y - mu) * lax.rsqrt(var + LN_EPS) * g + b


def _tile_copies(hbm, buf, sem, tile, slot, to_vmem):
    copies = []
    for b in range(BATCH):
        h = hbm.at[b, pl.ds(tile * T_TILE, T_TILE), :]
        v = buf.at[slot, :, b, :]
        src, dst = (h, v) if to_vmem else (v, h)
        copies.append(pltpu.make_async_copy(src, dst, sem.at[slot, b]))
    return copies


def _split_phases(x3):
    x4 = x3.reshape(H_TILE, PHASES, SUBLANES, x3.shape[-1])
    return jnp.concatenate([x4[:, ph].reshape(HALF, x3.shape[-1]) for ph in range(PHASES)], axis=0)


def _merge_phases(y):
    parts = [y[ph * HALF:(ph + 1) * HALF].reshape(H_TILE, SUBLANES, y.shape[-1])
             for ph in range(PHASES)]
    return jnp.stack(parts, axis=1).reshape(T_TILE, SUBLANES, y.shape[-1])


def _layer_tile(pre1, x, mixer_refs, ffn_refs, xs_re, xs_im, ve_ext, vo_ext):
    (win_ref, bin_ref, bmat_ref, a2_ref, wre_ref, wim_ref, kd_ref, dskip_ref, gluw_ref, glub_ref,
     wssm_ref, convw_ref, wconv_ref, wo_ref, ln1g_ref, ln1b_ref) = mixer_refs
    wg_ref, wu_ref, wd_ref, ln2g_ref, ln2b_ref = ffn_refs
    dot = lambda a, w: jnp.dot(a, w, preferred_element_type=f32)
    x1 = _layer_norm(pre1, ln1g_ref[...], ln1b_ref[...])
    x1b = x1.astype(bf16)
    xb = x.astype(bf16)

    def proj(lo, hi):
        return dot(xb, win_ref[:, lo:hi]) + bin_ref[:, lo:hi]

    u = proj(0, SSM_WIDTH)
    h = proj(SSM_WIDTH, SSM_WIDTH + CONV_WIDTH)
    ub = u.astype(bf16)
    u2 = []
    for j in range(N_BLOCKS):
        cs = slice(j * BLOCK_CH, (j + 1) * BLOCK_CH)
        u2.append(jnp.concatenate([ub[:HALF, cs], ub[HALF:, cs]], axis=-1))
        bu = dot(u2[j], bmat_ref[j])
        xs_re[SUBLANES:SUBLANES + HALF, j * BLOCK_ST:(j + 1) * BLOCK_ST] = bu[:, :BLOCK_ST]
        xs_im[SUBLANES:SUBLANES + HALF, j * BLOCK_ST:(j + 1) * BLOCK_ST] = bu[:, BLOCK_ST:]

    o4 = SSM_WIDTH + 3 * CONV_WIDTH
    c_gate = proj(SSM_WIDTH + CONV_WIDTH, SSM_WIDTH + 2 * CONV_WIDTH)
    b_gate = proj(SSM_WIDTH + 2 * CONV_WIDTH, o4)
    gate_a = proj(o4, o4 + D_MODEL)
    gate_b = proj(o4 + D_MODEL, o4 + 2 * D_MODEL)

    for j in range(N_BLOCKS):
        sl = slice(j * BLOCK_ST, (j + 1) * BLOCK_ST)
        ar = jnp.broadcast_to(a2_ref[0:1, sl], (SUBLANES, BLOCK_ST))
        ai = jnp.broadcast_to(a2_ref[1:2, sl], (SUBLANES, BLOCK_ST))
        sr, si = xs_re[0:SUBLANES, sl], xs_im[0:SUBLANES, sl]
        for k in range(H_TILE):
            r = slice((k + 1) * SUBLANES, (k + 2) * SUBLANES)
            sr, si = (ar * sr - ai * si + xs_re[r, sl], ar * si + ai * sr + xs_im[r, sl])
            xs_re[r, sl] = sr
            xs_im[r, sl] = si

    gate = dot(x1b, wg_ref[...])

    y_e, y_o = [], []
    for j in range(N_BLOCKS):
        sl = slice(j * BLOCK_ST, (j + 1) * BLOCK_ST)
        y2 = (dot(xs_re[0:HALF, sl].astype(bf16), wre_ref[j])
              + dot(xs_im[0:HALF, sl].astype(bf16), wim_ref[j]) + dot(u2[j], kd_ref[j]))
        y_e.append(y2[:, :BLOCK_CH])
        y_o.append(y2[:, BLOCK_CH:])
    xs_re[0:SUBLANES, :] = xs_re[HALF:HALF + SUBLANES, :]
    xs_im[0:SUBLANES, :] = xs_im[HALF:HALF + SUBLANES, :]
    y_a = jnp.concatenate([jnp.concatenate(y_e, axis=-1), jnp.concatenate(y_o, axis=-1)], axis=0)
    y_a = y_a + dskip_ref[...] * u

    up = dot(x1b, wu_ref[...])

    v = c_gate * h
    ve_ext[SUBLANES:SUBLANES + HALF, :] = v[:HALF]
    vo_ext[SUBLANES:SUBLANES + HALF, :] = v[HALF:]
    w0, w1, w2 = convw_ref[0:1, :], convw_ref[1:2, :], convw_ref[2:3, :]
    z_e = w0 * ve_ext[0:HALF, :] + w1 * vo_ext[0:HALF, :] + w2 * v[:HALF]
    z_o = w0 * vo_ext[0:HALF, :] + w1 * v[:HALF] + w2 * v[HALF:]
    ve_ext[0:SUBLANES, :] = ve_ext[HALF:HALF + SUBLANES, :]
    vo_ext[0:SUBLANES, :] = vo_ext[HALF:HALF + SUBLANES, :]
    bz = (b_gate * jnp.concatenate([z_e, z_o], axis=0)).astype(bf16)

    g = jax.nn.gelu(y_a)
    glu = dot(g.astype(bf16), gluw_ref[...]) + glub_ref[...]
    y_b = dot(bz, wconv_ref[...])

    hid = (jax.nn.silu(gate) * up).astype(bf16)
    ffn_lo = dot(hid, wd_ref[:, 0:D_MODEL // 2])
    y_a = dot((g * jax.nn.sigmoid(glu)).astype(bf16), wssm_ref[...])
    ffn_hi = dot(hid, wd_ref[:, D_MODEL // 2:D_MODEL])
    ffn = jnp.concatenate([ffn_lo, ffn_hi], axis=-1)
    merged = jax.nn.sigmoid(gate_a) * y_a + jax.nn.sigmoid(gate_b) * y_b
    mix = dot(merged.astype(bf16), wo_ref[...])
    out = _layer_norm(ALPHA * x1 + ffn, ln2g_ref[...], ln2b_ref[...])
    return out, ALPHA * x + mix


N_MIXER_PARAMS = 16
N_FFN_PARAMS = 5
CAST_PARAMS = (0, 8, 10, 12, 13, 16, 17, 18)
CAST_ROWS = 128


def _load_weights_bf16(hbm_refs, vmem_refs, stage, sem):
    chunks = []
    for w_hbm, w_vmem in zip(hbm_refs, vmem_refs):
        rows, cols = w_hbm.shape
        assert rows % CAST_ROWS == 0 and cols <= stage.shape[-1]
        for r0 in range(0, rows, CAST_ROWS):
            chunks.append((w_hbm, w_vmem, r0, cols))

    def copy(k):
        w_hbm, _, r0, cols = chunks[k]
        return pltpu.make_async_copy(w_hbm.at[pl.ds(r0, CAST_ROWS), :],
                                     stage.at[k % 2, :, pl.ds(0, cols)], sem.at[k % 2])

    copy(0).start()
    for k in range(len(chunks)):
        if k + 1 < len(chunks):
            copy(k + 1).start()
        copy(k).wait()
        _, w_vmem, r0, cols = chunks[k]
        w_vmem[pl.ds(r0, CAST_ROWS), :] = stage[k % 2, :, 0:cols].astype(bf16)


def _layer_kernel(x_hbm, *refs):
    n_params = N_MIXER_PARAMS + N_FFN_PARAMS
    params = list(refs[:n_params])
    o_hbm, xbuf, xsem, obuf, osem, pre1_ref, xs_re, xs_im, ve_ext, vo_ext, stage, wsem = \
        refs[n_params:n_params + 12]
    w_vmem = refs[n_params + 12:]
    w_hbm = [params[p] for p in CAST_PARAMS]
    for p, w in zip(CAST_PARAMS, w_vmem):
        params[p] = w
    mixer_refs = params[:N_MIXER_PARAMS]
    ffn_refs = params[N_MIXER_PARAMS:]
    i = pl.program_id(0)
    slot = i % 2
    ffn_tile = i - 1
    ffn_slot = (i + 1) % 2

    @pl.when(i == 0)
    def _():
        for cp in _tile_copies(x_hbm, xbuf, xsem, 0, 0, True):
            cp.start()
        zeros = jnp.zeros((SUBLANES, N_STATE), f32)
        xs_re[0:SUBLANES, :] = zeros
        xs_im[0:SUBLANES, :] = zeros
        ve_ext[0:SUBLANES, :] = jnp.zeros((SUBLANES, CONV_WIDTH), f32)
        vo_ext[0:SUBLANES, :] = jnp.zeros((SUBLANES, CONV_WIDTH), f32)
        pre1_ref[...] = jnp.zeros_like(pre1_ref)
        _load_weights_bf16(w_hbm, w_vmem, stage, wsem)

    @pl.when(i + 1 < N_TILES)
    def _():
        for cp in _tile_copies(x_hbm, xbuf, xsem, i + 1, 1 - slot, True):
            cp.start()

    @pl.when(i < N_TILES)
    def _():
        for cp in _tile_copies(x_hbm, xbuf, xsem, i, slot, True):
            cp.wait()

    @pl.when(ffn_tile >= 2)
    def _():
        for cp in _tile_copies(o_hbm, obuf, osem, ffn_tile - 2, ffn_slot, False):
            cp.wait()

    y, pre1 = _layer_tile(pre1_ref[...], _split_phases(xbuf[slot]), mixer_refs, ffn_refs,
                          xs_re, xs_im, ve_ext, vo_ext)
    obuf[ffn_slot] = _merge_phases(y)
    pre1_ref[...] = pre1

    @pl.when(i > 0)
    def _():
        for cp in _tile_copies(o_hbm, obuf, osem, ffn_tile, ffn_slot, False):
            cp.start()

    @pl.when(i == N_TILES)
    def _():
        for cp in _tile_copies(o_hbm, obuf, osem, ffn_tile - 1, 1 - ffn_slot, False):
            cp.wait()
        for cp in _tile_copies(o_hbm, obuf, osem, ffn_tile, ffn_slot, False):
            cp.wait()


def _const_spec(shape):
    nd = len(shape)
    return pl.BlockSpec(shape, lambda i, nd=nd: (0,) * nd, pipeline_mode=pl.Buffered(1))


def _ssm_prep_kernel(lre_ref, lim_ref, ldt_ref, bre_ref, bim_ref, cre_ref, cim_ref,
                     bmat_ref, a2_ref, wre_ref, wim_ref, kd_ref):
    lam_re, lam_im = lre_ref[...], lim_ref[...]
    dt = jnp.exp(ldt_ref[...])
    mag = jnp.exp(lam_re * dt)
    a_re = mag * jnp.cos(lam_im * dt)
    a_im = mag * jnp.sin(lam_im * dt)
    den = lam_re * lam_re + lam_im * lam_im
    num_re = a_re - 1.0
    fr = (num_re * lam_re + a_im * lam_im) / den
    fi = (a_im * lam_re - num_re * lam_im) / den
    a2_re = a_re * a_re - a_im * a_im
    a2_im = 2.0 * a_re * a_im
    a2_ref[0:1, :] = a2_re
    a2_ref[1:2, :] = a2_im

    def cmul(xr, xi, yr, yi):
        return xr * yr - xi * yi, xr * yi + xi * yr

    b_re, b_im, c_re, c_im = bre_ref[...], bim_ref[...], cre_ref[...], cim_ref[...]
    bb_re, bb_im = cmul(fr, fi, b_re, b_im)
    abb_re, abb_im = cmul(a_re, a_im, bb_re, bb_im)
    ca_re, ca_im = cmul(a_re, a_im, c_re, c_im)
    ca2_re, ca2_im = cmul(a2_re, a2_im, c_re, c_im)

    row_group = lax.broadcasted_iota(jnp.int32, (BLOCK_CH, BLOCK_ST), 0) // SSM_GROUP
    col_group = lax.broadcasted_iota(jnp.int32, (BLOCK_CH, BLOCK_ST), 1) // SSM_STATE
    diag = row_group == col_group

    for j in range(N_BLOCKS):
        def bd(m):
            blk = m[:, j * BLOCK_ST:(j + 1) * BLOCK_ST]
            return jnp.where(diag, jnp.concatenate([blk] * GROUPS_PER_BLOCK, axis=0), 0.0)

        def nt(x, y):
            return lax.dot_general(x, y, (((1,), (1,)), ((), ())), precision=lax.Precision.HIGHEST,
                                   preferred_element_type=f32)

        bbr, bbi, abr, abi = bd(bb_re), bd(bb_im), bd(abb_re), bd(abb_im)
        cr, ci = bd(c_re), bd(c_im)
        bmat_ref[j, 0:BLOCK_CH, 0:BLOCK_ST] = abr.astype(bf16)
        bmat_ref[j, 0:BLOCK_CH, BLOCK_ST:2 * BLOCK_ST] = abi.astype(bf16)
        bmat_ref[j, BLOCK_CH:2 * BLOCK_CH, 0:BLOCK_ST] = bbr.astype(bf16)
        bmat_ref[j, BLOCK_CH:2 * BLOCK_CH, BLOCK_ST:2 * BLOCK_ST] = bbi.astype(bf16)
        wre_ref[j, :, 0:BLOCK_CH] = bd(ca_re).T.astype(bf16)
        wre_ref[j, :, BLOCK_CH:2 * BLOCK_CH] = bd(ca2_re).T.astype(bf16)
        wim_ref[j, :, 0:BLOCK_CH] = (-bd(ca_im)).T.astype(bf16)
        wim_ref[j, :, BLOCK_CH:2 * BLOCK_CH] = (-bd(ca2_im)).T.astype(bf16)
        k0 = nt(bbr, cr) - nt(bbi, ci)
        k1 = nt(abr, cr) - nt(abi, ci)
        kd_ref[j, 0:BLOCK_CH, 0:BLOCK_CH] = k0.astype(bf16)
        kd_ref[j, 0:BLOCK_CH, BLOCK_CH:2 * BLOCK_CH] = k1.astype(bf16)
        kd_ref[j, BLOCK_CH:2 * BLOCK_CH, 0:BLOCK_CH] = jnp.zeros((BLOCK_CH, BLOCK_CH), bf16)
        kd_ref[j, BLOCK_CH:2 * BLOCK_CH, BLOCK_CH:2 * BLOCK_CH] = k0.astype(bf16)


def _ssm_params(lam_re, lam_im, log_dt, b_re, b_im, c_re, c_im):
    flat = lambda m: m.astype(f32).reshape(1, N_STATE)
    ldt = jnp.broadcast_to(log_dt.astype(f32)[:, None], (SSM_GROUPS, SSM_STATE))
    b_t = lambda m: jnp.transpose(m.astype(f32), (2, 0, 1)).reshape(SSM_GROUP, N_STATE)
    c_t = lambda m: jnp.transpose(m.astype(f32), (1, 0, 2)).reshape(SSM_GROUP, N_STATE)
    return pl.pallas_call(
        _ssm_prep_kernel,
        out_shape=[jax.ShapeDtypeStruct((N_BLOCKS, 2 * BLOCK_CH, 2 * BLOCK_ST), bf16),
                   jax.ShapeDtypeStruct((2, N_STATE), f32),
                   jax.ShapeDtypeStruct((N_BLOCKS, BLOCK_ST, 2 * BLOCK_CH), bf16),
                   jax.ShapeDtypeStruct((N_BLOCKS, BLOCK_ST, 2 * BLOCK_CH), bf16),
                   jax.ShapeDtypeStruct((N_BLOCKS, 2 * BLOCK_CH, 2 * BLOCK_CH), bf16)],
        name="ssm_prep",
    )(flat(lam_re), flat(lam_im), flat(ldt), b_t(b_re), b_t(b_im), c_t(c_re), c_t(c_im))


def _layer(x, w_in, b_in, lam_re, lam_im, log_dt, b_re, b_im, c_re, c_im, d_skip, glu_w, glu_b,
           w_ssm_out, conv_w, w_conv_out, w_o, ln1_g, ln1_b, w_gate, w_up, w_down, ln2_g, ln2_b):
    row = lambda v: v.reshape(1, -1).astype(f32)
    w32 = lambda w: w.astype(f32)
    mixer_in = (w32(w_in), row(b_in),
                *_ssm_params(lam_re, lam_im, log_dt, b_re, b_im, c_re, c_im), row(d_skip),
                w32(glu_w), row(glu_b), w32(w_ssm_out), conv_w.astype(f32),
                w32(w_conv_out), w32(w_o), row(ln1_g), row(ln1_b))
    ffn_in = (w32(w_gate), w32(w_up), w32(w_down), row(ln2_g), row(ln2_b))
    assert len(mixer_in) == N_MIXER_PARAMS and len(ffn_in) == N_FFN_PARAMS
    params = mixer_in + ffn_in
    param_specs = [pl.BlockSpec(memory_space=pl.ANY) if p in CAST_PARAMS else _const_spec(a.shape)
                   for p, a in enumerate(params)]
    stage_cols = max(params[p].shape[1] for p in CAST_PARAMS)
    return pl.pallas_call(
        _layer_kernel,
        grid=(N_TILES + 1,),
        in_specs=[pl.BlockSpec(memory_space=pl.ANY)] + param_specs,
        out_specs=pl.BlockSpec(memory_space=pl.ANY),
        out_shape=jax.ShapeDtypeStruct((BATCH, SEQ, D_MODEL), f32),
        scratch_shapes=[pltpu.VMEM((2, T_TILE, BATCH, D_MODEL), f32),
                        pltpu.SemaphoreType.DMA((2, BATCH)),
                        pltpu.VMEM((2, T_TILE, BATCH, D_MODEL), f32),
                        pltpu.SemaphoreType.DMA((2, BATCH)),
                        pltpu.VMEM((R_TILE, D_MODEL), f32),
                        pltpu.VMEM((HALF + SUBLANES, N_STATE), f32),
                        pltpu.VMEM((HALF + SUBLANES, N_STATE), f32),
                        pltpu.VMEM((HALF + SUBLANES, CONV_WIDTH), f32),
                        pltpu.VMEM((HALF + SUBLANES, CONV_WIDTH), f32),
                        pltpu.VMEM((2, CAST_ROWS, stage_cols), f32),
                        pltpu.SemaphoreType.DMA((2,))]
                       + [pltpu.VMEM(params[p].shape, bf16) for p in CAST_PARAMS],
        compiler_params=pltpu.CompilerParams(dimension_semantics=("arbitrary",),
                                             vmem_limit_bytes=VMEM_LIMIT_BYTES),
        name="layer",
    )(x, *params)


def kernel(x, w_in, b_in, ssm_lambda_re, ssm_lambda_im, ssm_log_dt, ssm_b_re, ssm_b_im, ssm_c_re, ssm_c_im, ssm_d, glu_w, glu_b, w_ssm_out, conv_w, w_conv_out, w_o, ln1_g, ln1_b, w_gate, w_up, w_down, ln2_g, ln2_b):
    assert x.shape == (BATCH, SEQ, D_MODEL) and BATCH == SUBLANES
    for l in range(w_in.shape[0]):
        x = _layer(x, w_in[l], b_in[l], ssm_lambda_re[l], ssm_lambda_im[l], ssm_log_dt[l],
                   ssm_b_re[l], ssm_b_im[l], ssm_c_re[l], ssm_c_im[l], ssm_d[l], glu_w[l],
                   glu_b[l], w_ssm_out[l], conv_w[l], w_conv_out[l], w_o[l], ln1_g[l], ln1_b[l],
                   w_gate[l], w_up[l], w_down[l], ln2_g[l], ln2_b[l])
    return x
```

```python
import jax
import jax.numpy as jnp
from jax import lax
from jax.experimental import pallas as pl
from jax.experimental.pallas import tpu as pltpu

D_MODEL = 1024
BATCH = 8
SEQ = 4096
SSM_WIDTH = D_MODEL // 2
SSM_GROUP = 16
SSM_GROUPS = SSM_WIDTH // SSM_GROUP
SSM_STATE = 64
CONV_WIDTH = D_MODEL // 2
CONV_K = 3
FFN_HIDDEN = 2816
IN_COLS = SSM_WIDTH + 3 * CONV_WIDTH + 2 * D_MODEL
DEPTH = 1
ALPHA = (2.0 * DEPTH) ** 0.25
LN_EPS = 1e-5

SUBLANES = 8
VMEM_LIMIT_BYTES = 60 * 1024 * 1024

GROUPS_PER_BLOCK = 8
N_BLOCKS = SSM_GROUPS // GROUPS_PER_BLOCK
BLOCK_CH = GROUPS_PER_BLOCK * SSM_GROUP
BLOCK_ST = GROUPS_PER_BLOCK * SSM_STATE
N_STATE = SSM_GROUPS * SSM_STATE

PHASES = 2
T_TILE = 32
R_TILE = T_TILE * BATCH
H_TILE = T_TILE // PHASES
HALF = H_TILE * SUBLANES
N_TILES = SEQ // T_TILE
assert N_TILES >= 2 and SEQ % T_TILE == 0 and T_TILE % PHASES == 0

f32 = jnp.float32
bf16 = jnp.bfloat16


def _layer_norm(y, g, b):
    mu = jnp.mean(y, axis=-1, keepdims=True)
    var = jnp.mean(jnp.square(y - mu), axis=-1, keepdims=True)
    return (y - mu) * lax.rsqrt(var + LN_EPS) * g + b


def _tile_copies(hbm, buf, sem, tile, slot, to_vmem):
    copies = []
    for b in range(BATCH):
        h = hbm.at[b, pl.ds(tile * T_TILE, T_TILE), :]
        v = buf.at[slot, :, b, :]
        src, dst = (h, v) if to_vmem else (v, h)
        copies.append(pltpu.make_async_copy(src, dst, sem.at[slot]))
    return copies


def _tile_wait(buf, sem, slot):
    pltpu.make_async_copy(buf.at[slot], buf.at[slot], sem.at[slot]).wait()


def _split_phases(x3):
    x4 = x3.reshape(H_TILE, PHASES, SUBLANES, x3.shape[-1])
    return jnp.concatenate([x4[:, ph].reshape(HALF, x3.shape[-1]) for ph in range(PHASES)], axis=0)


def _merge_phases(y):
    parts = [y[ph * HALF:(ph + 1) * HALF].reshape(H_TILE, SUBLANES, y.shape[-1])
             for ph in range(PHASES)]
    return jnp.stack(parts, axis=1).reshape(T_TILE, SUBLANES, y.shape[-1])


def _layer_tile(pre1, x, mixer_refs, ffn_refs, xs_re, xs_im, ve_ext, vo_ext):
    (win_ref, bin_ref, bmat_ref, a2_ref, wre_ref, wim_ref, kd_ref, dskip_ref, gluw_ref, glub_ref,
     wssm_ref, convw_ref, wconv_ref, wo_ref, ln1g_ref, ln1b_ref) = mixer_refs
    wg_ref, wu_ref, wd_ref, ln2g_ref, ln2b_ref = ffn_refs
    dot = lambda a, w: jnp.dot(a, w, preferred_element_type=f32)
    x1 = _layer_norm(pre1, ln1g_ref[...], ln1b_ref[...])
    x1b = x1.astype(bf16)
    xb = x.astype(bf16)

    def proj(lo, hi):
        return dot(xb, win_ref[:, lo:hi]) + bin_ref[:, lo:hi]

    u = proj(0, SSM_WIDTH)
    h = proj(SSM_WIDTH, SSM_WIDTH + CONV_WIDTH)
    ub = u.astype(bf16)
    u2 = []
    for j in range(N_BLOCKS):
        cs = slice(j * BLOCK_CH, (j + 1) * BLOCK_CH)
        u2.append(jnp.concatenate([ub[:HALF, cs], ub[HALF:, cs]], axis=-1))
        bu = dot(u2[j], bmat_ref[j])
        xs_re[SUBLANES:SUBLANES + HALF, j * BLOCK_ST:(j + 1) * BLOCK_ST] = bu[:, :BLOCK_ST]
        xs_im[SUBLANES:SUBLANES + HALF, j * BLOCK_ST:(j + 1) * BLOCK_ST] = bu[:, BLOCK_ST:]

    o4 = SSM_WIDTH + 3 * CONV_WIDTH
    c_gate = proj(SSM_WIDTH + CONV_WIDTH, SSM_WIDTH + 2 * CONV_WIDTH)
    b_gate = proj(SSM_WIDTH + 2 * CONV_WIDTH, o4)
    gate_a = proj(o4, o4 + D_MODEL)
    gate_b = proj(o4 + D_MODEL, o4 + 2 * D_MODEL)

    for j in range(N_BLOCKS):
        sl = slice(j * BLOCK_ST, (j + 1) * BLOCK_ST)
        ar = jnp.broadcast_to(a2_ref[0:1, sl], (SUBLANES, BLOCK_ST))
        ai = jnp.broadcast_to(a2_ref[1:2, sl], (SUBLANES, BLOCK_ST))
        sr, si = xs_re[0:SUBLANES, sl], xs_im[0:SUBLANES, sl]
        for k in range(H_TILE):
            r = slice((k + 1) * SUBLANES, (k + 2) * SUBLANES)
            sr, si = (ar * sr - ai * si + xs_re[r, sl], ar * si + ai * sr + xs_im[r, sl])
            xs_re[r, sl] = sr
            xs_im[r, sl] = si

    gate = dot(x1b, wg_ref[...])

    y_e, y_o = [], []
    for j in range(N_BLOCKS):
        sl = slice(j * BLOCK_ST, (j + 1) * BLOCK_ST)
        y2 = (dot(xs_re[0:HALF, sl].astype(bf16), wre_ref[j])
              + dot(xs_im[0:HALF, sl].astype(bf16), wim_ref[j]) + dot(u2[j], kd_ref[j]))
        y_e.append(y2[:, :BLOCK_CH])
        y_o.append(y2[:, BLOCK_CH:])
    xs_re[0:SUBLANES, :] = xs_re[HALF:HALF + SUBLANES, :]
    xs_im[0:SUBLANES, :] = xs_im[HALF:HALF + SUBLANES, :]
    y_a = jnp.concatenate([jnp.concatenate(y_e, axis=-1), jnp.concatenate(y_o, axis=-1)], axis=0)
    y_a = y_a + dskip_ref[...] * u

    up = dot(x1b, wu_ref[...])

    v = c_gate * h
    ve_ext[SUBLANES:SUBLANES + HALF, :] = v[:HALF]
    vo_ext[SUBLANES:SUBLANES + HALF, :] = v[HALF:]
    w0, w1, w2 = convw_ref[0:1, :], convw_ref[1:2, :], convw_ref[2:3, :]
    z_e = w0 * ve_ext[0:HALF, :] + w1 * vo_ext[0:HALF, :] + w2 * v[:HALF]
    z_o = w0 * vo_ext[0:HALF, :] + w1 * v[:HALF] + w2 * v[HALF:]
    ve_ext[0:SUBLANES, :] = ve_ext[HALF:HALF + SUBLANES, :]
    vo_ext[0:SUBLANES, :] = vo_ext[HALF:HALF + SUBLANES, :]
    bz = (b_gate * jnp.concatenate([z_e, z_o], axis=0)).astype(bf16)

    g = jax.nn.gelu(y_a)
    glu = dot(g.astype(bf16), gluw_ref[...]) + glub_ref[...]
    y_b = dot(bz, wconv_ref[...])

    hid = (jax.nn.silu(gate) * up).astype(bf16)
    ffn_lo = dot(hid, wd_ref[:, 0:D_MODEL // 2])
    y_a = dot((g * jax.nn.sigmoid(glu)).astype(bf16), wssm_ref[...])
    ffn_hi = dot(hid, wd_ref[:, D_MODEL // 2:D_MODEL])
    ffn = jnp.concatenate([ffn_lo, ffn_hi], axis=-1)
    merged = jax.nn.sigmoid(gate_a) * y_a + jax.nn.sigmoid(gate_b) * y_b
    mix = dot(merged.astype(bf16), wo_ref[...])
    out = _layer_norm(ALPHA * x1 + ffn, ln2g_ref[...], ln2b_ref[...])
    return out, ALPHA * x + mix


N_MIXER_PARAMS = 16
N_FFN_PARAMS = 5
CAST_PARAMS = (0, 8, 10, 12, 13, 16, 17, 18)
CAST_ROWS = 128


def _load_weights_bf16(hbm_refs, vmem_refs, stage, sem):
    chunks = []
    for w_hbm, w_vmem in zip(hbm_refs, vmem_refs):
        rows, cols = w_hbm.shape
        assert rows % CAST_ROWS == 0 and cols <= stage.shape[-1]
        for r0 in range(0, rows, CAST_ROWS):
            chunks.append((w_hbm, w_vmem, r0, cols))

    def copy(k):
        w_hbm, _, r0, cols = chunks[k]
        return pltpu.make_async_copy(w_hbm.at[pl.ds(r0, CAST_ROWS), :],
                                     stage.at[k % 2, :, pl.ds(0, cols)], sem.at[k % 2])

    copy(0).start()
    for k in range(len(chunks)):
        if k + 1 < len(chunks):
            copy(k + 1).start()
        copy(k).wait()
        _, w_vmem, r0, cols = chunks[k]
        w_vmem[pl.ds(r0, CAST_ROWS), :] = stage[k % 2, :, 0:cols].astype(bf16)


def _layer_kernel(x_hbm, *refs):
    n_params = N_MIXER_PARAMS + N_FFN_PARAMS
    params = list(refs[:n_params])
    o_hbm, xbuf, xsem, obuf, osem, pre1_ref, xs_re, xs_im, ve_ext, vo_ext, stage, wsem = \
        refs[n_params:n_params + 12]
    w_vmem = refs[n_params + 12:]
    w_hbm = [params[p] for p in CAST_PARAMS]
    for p, w in zip(CAST_PARAMS, w_vmem):
        params[p] = w
    mixer_refs = params[:N_MIXER_PARAMS]
    ffn_refs = params[N_MIXER_PARAMS:]
    i = pl.program_id(0)
    slot = i % 2
    ffn_tile = i - 1
    ffn_slot = (i + 1) % 2

    @pl.when(i == 0)
    def _():
        for cp in _tile_copies(x_hbm, xbuf, xsem, 0, 0, True):
            cp.start()
        zeros = jnp.zeros((SUBLANES, N_STATE), f32)
        xs_re[0:SUBLANES, :] = zeros
        xs_im[0:SUBLANES, :] = zeros
        ve_ext[0:SUBLANES, :] = jnp.zeros((SUBLANES, CONV_WIDTH), f32)
        vo_ext[0:SUBLANES, :] = jnp.zeros((SUBLANES, CONV_WIDTH), f32)
        pre1_ref[...] = jnp.zeros_like(pre1_ref)
        _load_weights_bf16(w_hbm, w_vmem, stage, wsem)

    @pl.when(i + 1 < N_TILES)
    def _():
        for cp in _tile_copies(x_hbm, xbuf, xsem, i + 1, 1 - slot, True):
            cp.start()

    @pl.when(i < N_TILES)
    def _():
        _tile_wait(xbuf, xsem, slot)

    @pl.when(ffn_tile >= 2)
    def _():
        _tile_wait(obuf, osem, ffn_slot)

    y, pre1 = _layer_tile(pre1_ref[...], _split_phases(xbuf[slot]), mixer_refs, ffn_refs,
                          xs_re, xs_im, ve_ext, vo_ext)
    obuf[ffn_slot] = _merge_phases(y)
    pre1_ref[...] = pre1

    @pl.when(i > 0)
    def _():
        for cp in _tile_copies(o_hbm, obuf, osem, ffn_tile, ffn_slot, False):
            cp.start()

    @pl.when(i == N_TILES)
    def _():
        _tile_wait(obuf, osem, 1 - ffn_slot)
        _tile_wait(obuf, osem, ffn_slot)


def _const_spec(shape):
    nd = len(shape)
    return pl.BlockSpec(shape, lambda i, nd=nd: (0,) * nd, pipeline_mode=pl.Buffered(1))


def _ssm_prep_kernel(lre_ref, lim_ref, ldt_ref, bre_ref, bim_ref, cre_ref, cim_ref,
                     bmat_ref, a2_ref, wre_ref, wim_ref, kd_ref):
    lam_re, lam_im = lre_ref[...], lim_ref[...]
    dt = jnp.exp(ldt_ref[...])
    mag = jnp.exp(lam_re * dt)
    a_re = mag * jnp.cos(lam_im * dt)
    a_im = mag * jnp.sin(lam_im * dt)
    den = lam_re * lam_re + lam_im * lam_im
    num_re = a_re - 1.0
    fr = (num_re * lam_re + a_im * lam_im) / den
    fi = (a_im * lam_re - num_re * lam_im) / den
    a2_re = a_re * a_re - a_im * a_im
    a2_im = 2.0 * a_re * a_im
    a2_ref[0:1, :] = a2_re
    a2_ref[1:2, :] = a2_im

    def cmul(xr, xi, yr, yi):
        return xr * yr - xi * yi, xr * yi + xi * yr

    b_re, b_im, c_re, c_im = bre_ref[...], bim_ref[...], cre_ref[...], cim_ref[...]
    bb_re, bb_im = cmul(fr, fi, b_re, b_im)
    abb_re, abb_im = cmul(a_re, a_im, bb_re, bb_im)
    ca_re, ca_im = cmul(a_re, a_im, c_re, c_im)
    ca2_re, ca2_im = cmul(a2_re, a2_im, c_re, c_im)

    row_group = lax.broadcasted_iota(jnp.int32, (BLOCK_CH, BLOCK_ST), 0) // SSM_GROUP
    col_group = lax.broadcasted_iota(jnp.int32, (BLOCK_CH, BLOCK_ST), 1) // SSM_STATE
    diag = row_group == col_group

    for j in range(N_BLOCKS):
        def bd(m):
            blk = m[:, j * BLOCK_ST:(j + 1) * BLOCK_ST]
            return jnp.where(diag, jnp.concatenate([blk] * GROUPS_PER_BLOCK, axis=0), 0.0)

        def nt(x, y):
            return lax.dot_general(x, y, (((1,), (1,)), ((), ())), precision=lax.Precision.HIGHEST,
                                   preferred_element_type=f32)

        bbr, bbi, abr, abi = bd(bb_re), bd(bb_im), bd(abb_re), bd(abb_im)
        cr, ci = bd(c_re), bd(c_im)
        bmat_ref[j, 0:BLOCK_CH, 0:BLOCK_ST] = abr.astype(bf16)
        bmat_ref[j, 0:BLOCK_CH, BLOCK_ST:2 * BLOCK_ST] = abi.astype(bf16)
        bmat_ref[j, BLOCK_CH:2 * BLOCK_CH, 0:BLOCK_ST] = bbr.astype(bf16)
        bmat_ref[j, BLOCK_CH:2 * BLOCK_CH, BLOCK_ST:2 * BLOCK_ST] = bbi.astype(bf16)
        wre_ref[j, :, 0:BLOCK_CH] = bd(ca_re).T.astype(bf16)
        wre_ref[j, :, BLOCK_CH:2 * BLOCK_CH] = bd(ca2_re).T.astype(bf16)
        wim_ref[j, :, 0:BLOCK_CH] = (-bd(ca_im)).T.astype(bf16)
        wim_ref[j, :, BLOCK_CH:2 * BLOCK_CH] = (-bd(ca2_im)).T.astype(bf16)
        k0 = nt(bbr, cr) - nt(bbi, ci)
        k1 = nt(abr, cr) - nt(abi, ci)
        kd_ref[j, 0:BLOCK_CH, 0:BLOCK_CH] = k0.astype(bf16)
        kd_ref[j, 0:BLOCK_CH, BLOCK_CH:2 * BLOCK_CH] = k1.astype(bf16)
        kd_ref[j, BLOCK_CH:2 * BLOCK_CH, 0:BLOCK_CH] = jnp.zeros((BLOCK_CH, BLOCK_CH), bf16)
        kd_ref[j, BLOCK_CH:2 * BLOCK_CH, BLOCK_CH:2 * BLOCK_CH] = k0.astype(bf16)


def _ssm_params(lam_re, lam_im, log_dt, b_re, b_im, c_re, c_im):
    flat = lambda m: m.astype(f32).reshape(1, N_STATE)
    ldt = jnp.broadcast_to(log_dt.astype(f32)[:, None], (SSM_GROUPS, SSM_STATE))
    b_t = lambda m: jnp.transpose(m.astype(f32), (2, 0, 1)).reshape(SSM_GROUP, N_STATE)
    c_t = lambda m: jnp.transpose(m.astype(f32), (1, 0, 2)).reshape(SSM_GROUP, N_STATE)
    return pl.pallas_call(
        _ssm_prep_kernel,
        out_shape=[jax.ShapeDtypeStruct((N_BLOCKS, 2 * BLOCK_CH, 2 * BLOCK_ST), bf16),
                   jax.ShapeDtypeStruct((2, N_STATE), f32),
                   jax.ShapeDtypeStruct((N_BLOCKS, BLOCK_ST, 2 * BLOCK_CH), bf16),
                   jax.ShapeDtypeStruct((N_BLOCKS, BLOCK_ST, 2 * BLOCK_CH), bf16),
                   jax.ShapeDtypeStruct((N_BLOCKS, 2 * BLOCK_CH, 2 * BLOCK_CH), bf16)],
        name="ssm_prep",
    )(flat(lam_re), flat(lam_im), flat(ldt), b_t(b_re), b_t(b_im), c_t(c_re), c_t(c_im))


def _layer(x, w_in, b_in, lam_re, lam_im, log_dt, b_re, b_im, c_re, c_im, d_skip, glu_w, glu_b,
           w_ssm_out, conv_w, w_conv_out, w_o, ln1_g, ln1_b, w_gate, w_up, w_down, ln2_g, ln2_b):
    row = lambda v: v.reshape(1, -1).astype(f32)
    w32 = lambda w: w.astype(f32)
    mixer_in = (w32(w_in), row(b_in),
                *_ssm_params(lam_re, lam_im, log_dt, b_re, b_im, c_re, c_im), row(d_skip),
                w32(glu_w), row(glu_b), w32(w_ssm_out), conv_w.astype(f32),
                w32(w_conv_out), w32(w_o), row(ln1_g), row(ln1_b))
    ffn_in = (w32(w_gate), w32(w_up), w32(w_down), row(ln2_g), row(ln2_b))
    assert len(mixer_in) == N_MIXER_PARAMS and len(ffn_in) == N_FFN_PARAMS
    params = mixer_in + ffn_in
    param_specs = [pl.BlockSpec(memory_space=pl.ANY) if p in CAST_PARAMS else _const_spec(a.shape)
                   for p, a in enumerate(params)]
    stage_cols = max(params[p].shape[1] for p in CAST_PARAMS)
    return pl.pallas_call(
        _layer_kernel,
        grid=(N_TILES + 1,),
        in_specs=[pl.BlockSpec(memory_space=pl.ANY)] + param_specs,
        out_specs=pl.BlockSpec(memory_space=pl.ANY),
        out_shape=jax.ShapeDtypeStruct((BATCH, SEQ, D_MODEL), f32),
        scratch_shapes=[pltpu.VMEM((2, T_TILE, BATCH, D_MODEL), f32),
                        pltpu.SemaphoreType.DMA((2,)),
                        pltpu.VMEM((2, T_TILE, BATCH, D_MODEL), f32),
                        pltpu.SemaphoreType.DMA((2,)),
                        pltpu.VMEM((R_TILE, D_MODEL), f32),
                        pltpu.VMEM((HALF + SUBLANES, N_STATE), f32),
                        pltpu.VMEM((HALF + SUBLANES, N_STATE), f32),
                        pltpu.VMEM((HALF + SUBLANES, CONV_WIDTH), f32),
                        pltpu.VMEM((HALF + SUBLANES, CONV_WIDTH), f32),
                        pltpu.VMEM((2, CAST_ROWS, stage_cols), f32),
                        pltpu.SemaphoreType.DMA((2,))]
                       + [pltpu.VMEM(params[p].shape, bf16) for p in CAST_PARAMS],
        compiler_params=pltpu.CompilerParams(dimension_semantics=("arbitrary",),
                                             vmem_limit_bytes=VMEM_LIMIT_BYTES),
        name="layer",
    )(x, *params)


def kernel(x, w_in, b_in, ssm_lambda_re, ssm_lambda_im, ssm_log_dt, ssm_b_re, ssm_b_im, ssm_c_re, ssm_c_im, ssm_d, glu_w, glu_b, w_ssm_out, conv_w, w_conv_out, w_o, ln1_g, ln1_b, w_gate, w_up, w_down, ln2_g, ln2_b):
    assert x.shape == (BATCH, SEQ, D_MODEL) and BATCH == SUBLANES
    for l in range(w_in.shape[0]):
        x = _layer(x, w_in[l], b_in[l], ssm_lambda_re[l], ssm_lambda_im[l], ssm_log_dt[l],
                   ssm_b_re[l], ssm_b_im[l], ssm_c_re[l], ssm_c_im[l], ssm_d[l], glu_w[l],
                   glu_b[l], w_ssm_out[l], conv_w[l], w_conv_out[l], w_o[l], ln1_g[l], ln1_b[l],
                   w_gate[l], w_up[l], w_down[l], ln2_g[l], ln2_b[l])
    return x
```

```python
import jax
import jax.numpy as jnp
from jax import lax
from jax.experimental import pallas as pl
from jax.experimental.pallas import tpu as pltpu

D_MODEL = 1024
BATCH = 8
SEQ = 4096
SSM_WIDTH = D_MODEL // 2
SSM_GROUP = 16
SSM_GROUPS = SSM_WIDTH // SSM_GROUP
SSM_STATE = 64
CONV_WIDTH = D_MODEL // 2
CONV_K = 3
FFN_HIDDEN = 2816
IN_COLS = SSM_WIDTH + 3 * CONV_WIDTH + 2 * D_MODEL
DEPTH = 1
ALPHA = (2.0 * DEPTH) ** 0.25
LN_EPS = 1e-5

SUBLANES = 8
VMEM_LIMIT_BYTES = 60 * 1024 * 1024

GROUPS_PER_BLOCK = 8
N_BLOCKS = SSM_GROUPS // GROUPS_PER_BLOCK
BLOCK_CH = GROUPS_PER_BLOCK * SSM_GROUP
BLOCK_ST = GROUPS_PER_BLOCK * SSM_STATE
N_STATE = SSM_GROUPS * SSM_STATE

PHASES = 2
T_TILE = 32
R_TILE = T_TILE * BATCH
H_TILE = T_TILE // PHASES
HALF = H_TILE * SUBLANES
N_TILES = SEQ // T_TILE
assert N_TILES >= 2 and SEQ % T_TILE == 0 and T_TILE % PHASES == 0

f32 = jnp.float32
bf16 = jnp.bfloat16


def _layer_norm(y, g, b):
    mu = jnp.mean(y, axis=-1, keepdims=True)
    var = jnp.mean(jnp.square(y - mu), axis=-1, keepdims=True)
    return (y - mu) * lax.rsqrt(var + LN_EPS) * g + b


def _tile_copies(hbm, buf, sem, tile, slot, to_vmem):
    copies = []
    for b in range(BATCH):
        h = hbm.at[b, pl.ds(tile * T_TILE, T_TILE), :]
        v = buf.at[slot, :, b, :]
        src, dst = (h, v) if to_vmem else (v, h)
        copies.append(pltpu.make_async_copy(src, dst, sem.at[slot]))
    return copies


def _tile_wait(buf, sem, slot):
    pltpu.make_async_copy(buf.at[slot], buf.at[slot], sem.at[slot]).wait()


def _split_phases(x3):
    x4 = x3.reshape(H_TILE, PHASES, SUBLANES, x3.shape[-1])
    return jnp.concatenate([x4[:, ph].reshape(HALF, x3.shape[-1]) for ph in range(PHASES)], axis=0)


def _merge_phases(y):
    parts = [y[ph * HALF:(ph + 1) * HALF].reshape(H_TILE, SUBLANES, y.shape[-1])
             for ph in range(PHASES)]
    return jnp.stack(parts, axis=1).reshape(T_TILE, SUBLANES, y.shape[-1])


def _layer_tile(pre1, x, mixer_refs, ffn_refs, xs_re, xs_im, ve_ext, vo_ext):
    (win_ref, bin_ref, bmat_ref, a2_ref, wre_ref, wim_ref, kd_ref, dskip_ref, gluw_ref, glub_ref,
     wssm_ref, convw_ref, wconv_ref, wo_ref, ln1g_ref, ln1b_ref) = mixer_refs
    wg_ref, wu_ref, wd_ref, ln2g_ref, ln2b_ref = ffn_refs
    dot = lambda a, w: jnp.dot(a, w, preferred_element_type=f32)
    x1 = _layer_norm(pre1, ln1g_ref[...], ln1b_ref[...])
    x1b = x1.astype(bf16)
    xb = x.astype(bf16)

    def proj(lo, hi):
        return dot(xb, win_ref[:, lo:hi]) + bin_ref[:, lo:hi]

    u = proj(0, SSM_WIDTH)
    h = proj(SSM_WIDTH, SSM_WIDTH + CONV_WIDTH)
    ub = u.astype(bf16)
    u2 = []
    for j in range(N_BLOCKS):
        cs = slice(j * BLOCK_CH, (j + 1) * BLOCK_CH)
        u2.append(jnp.concatenate([ub[:HALF, cs], ub[HALF:, cs]], axis=-1))
        bu = dot(u2[j], bmat_ref[j])
        xs_re[SUBLANES:SUBLANES + HALF, j * BLOCK_ST:(j + 1) * BLOCK_ST] = bu[:, :BLOCK_ST]
        xs_im[SUBLANES:SUBLANES + HALF, j * BLOCK_ST:(j + 1) * BLOCK_ST] = bu[:, BLOCK_ST:]

    o4 = SSM_WIDTH + 3 * CONV_WIDTH
    c_gate = proj(SSM_WIDTH + CONV_WIDTH, SSM_WIDTH + 2 * CONV_WIDTH)
    b_gate = proj(SSM_WIDTH + 2 * CONV_WIDTH, o4)
    gate_a = proj(o4, o4 + D_MODEL)
    gate_b = proj(o4 + D_MODEL, o4 + 2 * D_MODEL)

    for j in range(N_BLOCKS):
        sl = slice(j * BLOCK_ST, (j + 1) * BLOCK_ST)
        ar = jnp.broadcast_to(a2_ref[0:1, sl], (SUBLANES, BLOCK_ST))
        ai = jnp.broadcast_to(a2_ref[1:2, sl], (SUBLANES, BLOCK_ST))
        sr, si = xs_re[0:SUBLANES, sl], xs_im[0:SUBLANES, sl]
        for k in range(H_TILE):
            r = slice((k + 1) * SUBLANES, (k + 2) * SUBLANES)
            sr, si = (ar * sr - ai * si + xs_re[r, sl], ar * si + ai * sr + xs_im[r, sl])
            xs_re[r, sl] = sr
            xs_im[r, sl] = si

    gate = dot(x1b, wg_ref[...])

    y_e, y_o = [], []
    for j in range(N_BLOCKS):
        sl = slice(j * BLOCK_ST, (j + 1) * BLOCK_ST)
        y2 = (dot(xs_re[0:HALF, sl].astype(bf16), wre_ref[j])
              + dot(xs_im[0:HALF, sl].astype(bf16), wim_ref[j]) + dot(u2[j], kd_ref[j]))
        y_e.append(y2[:, :BLOCK_CH])
        y_o.append(y2[:, BLOCK_CH:])
    xs_re[0:SUBLANES, :] = xs_re[HALF:HALF + SUBLANES, :]
    xs_im[0:SUBLANES, :] = xs_im[HALF:HALF + SUBLANES, :]
    y_a = jnp.concatenate([jnp.concatenate(y_e, axis=-1), jnp.concatenate(y_o, axis=-1)], axis=0)
    y_a = y_a + dskip_ref[...] * u

    up = dot(x1b, wu_ref[...])

    v = c_gate * h
    ve_ext[SUBLANES:SUBLANES + HALF, :] = v[:HALF]
    vo_ext[SUBLANES:SUBLANES + HALF, :] = v[HALF:]
    w0, w1, w2 = convw_ref[0:1, :], convw_ref[1:2, :], convw_ref[2:3, :]
    z_e = w0 * ve_ext[0:HALF, :] + w1 * vo_ext[0:HALF, :] + w2 * v[:HALF]
    z_o = w0 * vo_ext[0:HALF, :] + w1 * v[:HALF] + w2 * v[HALF:]
    ve_ext[0:SUBLANES, :] = ve_ext[HALF:HALF + SUBLANES, :]
    vo_ext[0:SUBLANES, :] = vo_ext[HALF:HALF + SUBLANES, :]
    bz = (b_gate * jnp.concatenate([z_e, z_o], axis=0)).astype(bf16)

    g = jax.nn.gelu(y_a)
    glu = dot(g.astype(bf16), gluw_ref[...]) + glub_ref[...]
    y_b = dot(bz, wconv_ref[...])

    hid = (jax.nn.silu(gate) * up).astype(bf16)
    ffn_lo = dot(hid, wd_ref[:, 0:D_MODEL // 2])
    y_a = dot((g * jax.nn.sigmoid(glu)).astype(bf16), wssm_ref[...])
    ffn_hi = dot(hid, wd_ref[:, D_MODEL // 2:D_MODEL])
    ffn = jnp.concatenate([ffn_lo, ffn_hi], axis=-1)
    merged = jax.nn.sigmoid(gate_a) * y_a + jax.nn.sigmoid(gate_b) * y_b
    mix = dot(merged.astype(bf16), wo_ref[...])
    out = _layer_norm(ALPHA * x1 + ffn, ln2g_ref[...], ln2b_ref[...])
    return out, ALPHA * x + mix


N_MIXER_PARAMS = 16
N_FFN_PARAMS = 5
CAST_PARAMS = (0, 8, 10, 12, 13, 16, 17, 18)
CAST_ROWS = 128


def _load_weights_bf16(hbm_refs, vmem_refs, stage, sem):
    chunks = []
    for w_hbm, w_vmem in zip(hbm_refs, vmem_refs):
        rows, cols = w_hbm.shape
        assert rows % CAST_ROWS == 0 and cols <= stage.shape[-1]
        for r0 in range(0, rows, CAST_ROWS):
            chunks.append((w_hbm, w_vmem, r0, cols))

    def copy(k):
        w_hbm, _, r0, cols = chunks[k]
        return pltpu.make_async_copy(w_hbm.at[pl.ds(r0, CAST_ROWS), :],
                                     stage.at[k % 2, :, pl.ds(0, cols)], sem.at[k % 2])

    copy(0).start()
    for k in range(len(chunks)):
        if k + 1 < len(chunks):
            copy(k + 1).start()
        copy(k).wait()
        _, w_vmem, r0, cols = chunks[k]
        w_vmem[pl.ds(r0, CAST_ROWS), :] = stage[k % 2, :, 0:cols].astype(bf16)


def _layer_kernel(x_hbm, *refs):
    n_params = N_MIXER_PARAMS + N_FFN_PARAMS
    params = list(refs[:n_params])
    o_hbm, xbuf, xsem, obuf, osem, pre1_ref, xs_re, xs_im, ve_ext, vo_ext, stage, wsem = \
        refs[n_params:n_params + 12]
    w_vmem = refs[n_params + 12:]
    w_hbm = [params[p] for p in CAST_PARAMS]
    for p, w in zip(CAST_PARAMS, w_vmem):
        params[p] = w
    mixer_refs = params[:N_MIXER_PARAMS]
    ffn_refs = params[N_MIXER_PARAMS:]
    i = pl.program_id(0)
    slot = i % 2
    ffn_tile = i - 1
    ffn_slot = (i + 1) % 2

    @pl.when(i == 0)
    def _():
        for cp in _tile_copies(x_hbm, xbuf, xsem, 0, 0, True):
            cp.start()
        zeros = jnp.zeros((SUBLANES, N_STATE), f32)
        xs_re[0:SUBLANES, :] = zeros
        xs_im[0:SUBLANES, :] = zeros
        ve_ext[0:SUBLANES, :] = jnp.zeros((SUBLANES, CONV_WIDTH), f32)
        vo_ext[0:SUBLANES, :] = jnp.zeros((SUBLANES, CONV_WIDTH), f32)
        pre1_ref[...] = jnp.zeros_like(pre1_ref)
        _load_weights_bf16(w_hbm, w_vmem, stage, wsem)

    @pl.when(i + 1 < N_TILES)
    def _():
        for cp in _tile_copies(x_hbm, xbuf, xsem, i + 1, 1 - slot, True):
            cp.start()

    @pl.when(i < N_TILES)
    def _():
        _tile_wait(xbuf, xsem, slot)

    @pl.when(ffn_tile >= 2)
    def _():
        _tile_wait(obuf, osem, ffn_slot)

    y, pre1 = _layer_tile(pre1_ref[...], _split_phases(xbuf[slot]), mixer_refs, ffn_refs,
                          xs_re, xs_im, ve_ext, vo_ext)
    obuf[ffn_slot] = _merge_phases(y)
    pre1_ref[...] = pre1

    @pl.when(i > 0)
    def _():
        for cp in _tile_copies(o_hbm, obuf, osem, ffn_tile, ffn_slot, False):
            cp.start()

    @pl.when(i == N_TILES)
    def _():
        _tile_wait(obuf, osem, 1 - ffn_slot)
        _tile_wait(obuf, osem, ffn_slot)


def _const_spec(shape):
    nd = len(shape)
    return pl.BlockSpec(shape, lambda i, nd=nd: (0,) * nd, pipeline_mode=pl.Buffered(1))


def _ssm_prep_kernel(vec_ref, b_ref, c_ref, bmat_ref, a2_ref, wre_ref, wim_ref, kd_ref):
    lam_re, lam_im = vec_ref[0:1, :], vec_ref[1:2, :]
    dt = jnp.exp(vec_ref[2:3, :])
    mag = jnp.exp(lam_re * dt)
    a_re = mag * jnp.cos(lam_im * dt)
    a_im = mag * jnp.sin(lam_im * dt)
    den = lam_re * lam_re + lam_im * lam_im
    num_re = a_re - 1.0
    fr = (num_re * lam_re + a_im * lam_im) / den
    fi = (a_im * lam_re - num_re * lam_im) / den
    a2_re = a_re * a_re - a_im * a_im
    a2_im = 2.0 * a_re * a_im
    a2_ref[0:1, :] = a2_re
    a2_ref[1:2, :] = a2_im

    def cmul(xr, xi, yr, yi):
        return xr * yr - xi * yi, xr * yi + xi * yr

    b_re, b_im, c_re, c_im = b_ref[0], b_ref[1], c_ref[0], c_ref[1]
    bb_re, bb_im = cmul(fr, fi, b_re, b_im)
    abb_re, abb_im = cmul(a_re, a_im, bb_re, bb_im)
    ca_re, ca_im = cmul(a_re, a_im, c_re, c_im)
    ca2_re, ca2_im = cmul(a2_re, a2_im, c_re, c_im)

    row_group = lax.broadcasted_iota(jnp.int32, (BLOCK_CH, BLOCK_ST), 0) // SSM_GROUP
    col_group = lax.broadcasted_iota(jnp.int32, (BLOCK_CH, BLOCK_ST), 1) // SSM_STATE
    diag = row_group == col_group

    for j in range(N_BLOCKS):
        def bd(m):
            blk = m[:, j * BLOCK_ST:(j + 1) * BLOCK_ST]
            return jnp.where(diag, jnp.concatenate([blk] * GROUPS_PER_BLOCK, axis=0), 0.0)

        def nt(x, y):
            return lax.dot_general(x, y, (((1,), (1,)), ((), ())), precision=lax.Precision.HIGHEST,
                                   preferred_element_type=f32)

        bbr, bbi, abr, abi = bd(bb_re), bd(bb_im), bd(abb_re), bd(abb_im)
        cr, ci = bd(c_re), bd(c_im)
        bmat_ref[j, 0:BLOCK_CH, 0:BLOCK_ST] = abr.astype(bf16)
        bmat_ref[j, 0:BLOCK_CH, BLOCK_ST:2 * BLOCK_ST] = abi.astype(bf16)
        bmat_ref[j, BLOCK_CH:2 * BLOCK_CH, 0:BLOCK_ST] = bbr.astype(bf16)
        bmat_ref[j, BLOCK_CH:2 * BLOCK_CH, BLOCK_ST:2 * BLOCK_ST] = bbi.astype(bf16)
        wre_ref[j, :, 0:BLOCK_CH] = bd(ca_re).T.astype(bf16)
        wre_ref[j, :, BLOCK_CH:2 * BLOCK_CH] = bd(ca2_re).T.astype(bf16)
        wim_ref[j, :, 0:BLOCK_CH] = (-bd(ca_im)).T.astype(bf16)
        wim_ref[j, :, BLOCK_CH:2 * BLOCK_CH] = (-bd(ca2_im)).T.astype(bf16)
        k0 = nt(bbr, cr) - nt(bbi, ci)
        k1 = nt(abr, cr) - nt(abi, ci)
        kd_ref[j, 0:BLOCK_CH, 0:BLOCK_CH] = k0.astype(bf16)
        kd_ref[j, 0:BLOCK_CH, BLOCK_CH:2 * BLOCK_CH] = k1.astype(bf16)
        kd_ref[j, BLOCK_CH:2 * BLOCK_CH, 0:BLOCK_CH] = jnp.zeros((BLOCK_CH, BLOCK_CH), bf16)
        kd_ref[j, BLOCK_CH:2 * BLOCK_CH, BLOCK_CH:2 * BLOCK_CH] = k0.astype(bf16)


def _ssm_params(lam_re, lam_im, log_dt, b_re, b_im, c_re, c_im):
    ldt = jnp.broadcast_to(log_dt[:, None], (SSM_GROUPS, SSM_STATE))
    vec = jnp.stack([lam_re, lam_im, ldt]).astype(f32).reshape(3, N_STATE)
    b_t = jnp.transpose(jnp.stack([b_re, b_im]).astype(f32), (0, 3, 1, 2))
    c_t = jnp.transpose(jnp.stack([c_re, c_im]).astype(f32), (0, 2, 1, 3))
    return pl.pallas_call(
        _ssm_prep_kernel,
        out_shape=[jax.ShapeDtypeStruct((N_BLOCKS, 2 * BLOCK_CH, 2 * BLOCK_ST), bf16),
                   jax.ShapeDtypeStruct((2, N_STATE), f32),
                   jax.ShapeDtypeStruct((N_BLOCKS, BLOCK_ST, 2 * BLOCK_CH), bf16),
                   jax.ShapeDtypeStruct((N_BLOCKS, BLOCK_ST, 2 * BLOCK_CH), bf16),
                   jax.ShapeDtypeStruct((N_BLOCKS, 2 * BLOCK_CH, 2 * BLOCK_CH), bf16)],
        name="ssm_prep",
    )(vec, b_t.reshape(2, SSM_GROUP, N_STATE), c_t.reshape(2, SSM_GROUP, N_STATE))


def _layer(x, w_in, b_in, lam_re, lam_im, log_dt, b_re, b_im, c_re, c_im, d_skip, glu_w, glu_b,
           w_ssm_out, conv_w, w_conv_out, w_o, ln1_g, ln1_b, w_gate, w_up, w_down, ln2_g, ln2_b):
    row = lambda v: v.reshape(1, -1).astype(f32)
    w32 = lambda w: w.astype(f32)
    mixer_in = (w32(w_in), row(b_in),
                *_ssm_params(lam_re, lam_im, log_dt, b_re, b_im, c_re, c_im), row(d_skip),
                w32(glu_w), row(glu_b), w32(w_ssm_out), conv_w.astype(f32),
                w32(w_conv_out), w32(w_o), row(ln1_g), row(ln1_b))
    ffn_in = (w32(w_gate), w32(w_up), w32(w_down), row(ln2_g), row(ln2_b))
    assert len(mixer_in) == N_MIXER_PARAMS and len(ffn_in) == N_FFN_PARAMS
    params = mixer_in + ffn_in
    param_specs = [pl.BlockSpec(memory_space=pl.ANY) if p in CAST_PARAMS else _const_spec(a.shape)
                   for p, a in enumerate(params)]
    stage_cols = max(params[p].shape[1] for p in CAST_PARAMS)
    return pl.pallas_call(
        _layer_kernel,
        grid=(N_TILES + 1,),
        in_specs=[pl.BlockSpec(memory_space=pl.ANY)] + param_specs,
        out_specs=pl.BlockSpec(memory_space=pl.ANY),
        out_shape=jax.ShapeDtypeStruct((BATCH, SEQ, D_MODEL), f32),
        scratch_shapes=[pltpu.VMEM((2, T_TILE, BATCH, D_MODEL), f32),
                        pltpu.SemaphoreType.DMA((2,)),
                        pltpu.VMEM((2, T_TILE, BATCH, D_MODEL), f32),
                        pltpu.SemaphoreType.DMA((2,)),
                        pltpu.VMEM((R_TILE, D_MODEL), f32),
                        pltpu.VMEM((HALF + SUBLANES, N_STATE), f32),
                        pltpu.VMEM((HALF + SUBLANES, N_STATE), f32),
                        pltpu.VMEM((HALF + SUBLANES, CONV_WIDTH), f32),
                        pltpu.VMEM((HALF + SUBLANES, CONV_WIDTH), f32),
                        pltpu.VMEM((2, CAST_ROWS, stage_cols), f32),
                        pltpu.SemaphoreType.DMA((2,))]
                       + [pltpu.VMEM(params[p].shape, bf16) for p in CAST_PARAMS],
        compiler_params=pltpu.CompilerParams(dimension_semantics=("arbitrary",),
                                             vmem_limit_bytes=VMEM_LIMIT_BYTES),
        name="layer",
    )(x, *params)


def kernel(x, w_in, b_in, ssm_lambda_re, ssm_lambda_im, ssm_log_dt, ssm_b_re, ssm_b_im, ssm_c_re, ssm_c_im, ssm_d, glu_w, glu_b, w_ssm_out, conv_w, w_conv_out, w_o, ln1_g, ln1_b, w_gate, w_up, w_down, ln2_g, ln2_b):
    assert x.shape == (BATCH, SEQ, D_MODEL) and BATCH == SUBLANES
    for l in range(w_in.shape[0]):
        x = _layer(x, w_in[l], b_in[l], ssm_lambda_re[l], ssm_lambda_im[l], ssm_log_dt[l],
                   ssm_b_re[l], ssm_b_im[l], ssm_c_re[l], ssm_c_im[l], ssm_d[l], glu_w[l],
                   glu_b[l], w_ssm_out[l], conv_w[l], w_conv_out[l], w_o[l], ln1_g[l], ln1_b[l],
                   w_gate[l], w_up[l], w_down[l], ln2_g[l], ln2_b[l])
    return x
```

```python
import jax
import jax.numpy as jnp
from jax import lax
from jax.experimental import pallas as pl
from jax.experimental.pallas import tpu as pltpu

D_MODEL = 1024
BATCH = 8
SEQ = 4096
SSM_WIDTH = D_MODEL // 2
SSM_GROUP = 16
SSM_GROUPS = SSM_WIDTH // SSM_GROUP
SSM_STATE = 64
CONV_WIDTH = D_MODEL // 2
CONV_K = 3
FFN_HIDDEN = 2816
IN_COLS = SSM_WIDTH + 3 * CONV_WIDTH + 2 * D_MODEL
DEPTH = 1
ALPHA = (2.0 * DEPTH) ** 0.25
LN_EPS = 1e-5

SUBLANES = 8
VMEM_LIMIT_BYTES = 60 * 1024 * 1024

GROUPS_PER_BLOCK = 8
N_BLOCKS = SSM_GROUPS // GROUPS_PER_BLOCK
BLOCK_CH = GROUPS_PER_BLOCK * SSM_GROUP
BLOCK_ST = GROUPS_PER_BLOCK * SSM_STATE
N_STATE = SSM_GROUPS * SSM_STATE

PHASES = 2
T_TILE = 32
R_TILE = T_TILE * BATCH
H_TILE = T_TILE // PHASES
HALF = H_TILE * SUBLANES
N_TILES = SEQ // T_TILE
assert N_TILES >= 2 and SEQ % T_TILE == 0 and T_TILE % PHASES == 0

f32 = jnp.float32
bf16 = jnp.bfloat16


def _layer_norm(y, g, b):
    mu = jnp.mean(y, axis=-1, keepdims=True)
    var = jnp.mean(jnp.square(y - mu), axis=-1, keepdims=True)
    return (y - mu) * lax.rsqrt(var + LN_EPS) * g + b


def _tile_copies(hbm, buf, sem, tile, slot, to_vmem):
    copies = []
    for b in range(BATCH):
        h = hbm.at[b, pl.ds(tile * T_TILE, T_TILE), :]
        v = buf.at[slot, :, b, :]
        src, dst = (h, v) if to_vmem else (v, h)
        copies.append(pltpu.make_async_copy(src, dst, sem.at[slot]))
    return copies


def _tile_wait(buf, sem, slot):
    pltpu.make_async_copy(buf.at[slot], buf.at[slot], sem.at[slot]).wait()


def _split_phases(x3):
    x4 = x3.reshape(H_TILE, PHASES, SUBLANES, x3.shape[-1])
    return jnp.concatenate([x4[:, ph].reshape(HALF, x3.shape[-1]) for ph in range(PHASES)], axis=0)


def _merge_phases(y):
    parts = [y[ph * HALF:(ph + 1) * HALF].reshape(H_TILE, SUBLANES, y.shape[-1])
             for ph in range(PHASES)]
    return jnp.stack(parts, axis=1).reshape(T_TILE, SUBLANES, y.shape[-1])


def _layer_tile(pre1, x, mixer_refs, ssm_refs, ffn_refs, xs_re, xs_im, ve_ext, vo_ext):
    (win_ref, bin_ref, dskip_ref, gluw_ref, glub_ref, wssm_ref, convw_ref, wconv_ref, wo_ref,
     ln1g_ref, ln1b_ref) = mixer_refs[:2] + mixer_refs[5:]
    bmat_ref, a2_ref, wre_ref, wim_ref, kd_ref = ssm_refs
    wg_ref, wu_ref, wd_ref, ln2g_ref, ln2b_ref = ffn_refs
    dot = lambda a, w: jnp.dot(a, w, preferred_element_type=f32)
    x1 = _layer_norm(pre1, ln1g_ref[...], ln1b_ref[...])
    x1b = x1.astype(bf16)
    xb = x.astype(bf16)

    def proj(lo, hi):
        return dot(xb, win_ref[:, lo:hi]) + bin_ref[:, lo:hi]

    u = proj(0, SSM_WIDTH)
    h = proj(SSM_WIDTH, SSM_WIDTH + CONV_WIDTH)
    ub = u.astype(bf16)
    u2 = []
    for j in range(N_BLOCKS):
        cs = slice(j * BLOCK_CH, (j + 1) * BLOCK_CH)
        u2.append(jnp.concatenate([ub[:HALF, cs], ub[HALF:, cs]], axis=-1))
        bu = dot(u2[j], bmat_ref[j])
        xs_re[SUBLANES:SUBLANES + HALF, j * BLOCK_ST:(j + 1) * BLOCK_ST] = bu[:, :BLOCK_ST]
        xs_im[SUBLANES:SUBLANES + HALF, j * BLOCK_ST:(j + 1) * BLOCK_ST] = bu[:, BLOCK_ST:]

    o4 = SSM_WIDTH + 3 * CONV_WIDTH
    c_gate = proj(SSM_WIDTH + CONV_WIDTH, SSM_WIDTH + 2 * CONV_WIDTH)
    b_gate = proj(SSM_WIDTH + 2 * CONV_WIDTH, o4)
    gate_a = proj(o4, o4 + D_MODEL)
    gate_b = proj(o4 + D_MODEL, o4 + 2 * D_MODEL)

    for j in range(N_BLOCKS):
        sl = slice(j * BLOCK_ST, (j + 1) * BLOCK_ST)
        ar = jnp.broadcast_to(a2_ref[0:1, sl], (SUBLANES, BLOCK_ST))
        ai = jnp.broadcast_to(a2_ref[1:2, sl], (SUBLANES, BLOCK_ST))
        sr, si = xs_re[0:SUBLANES, sl], xs_im[0:SUBLANES, sl]
        for k in range(H_TILE):
            r = slice((k + 1) * SUBLANES, (k + 2) * SUBLANES)
            sr, si = (ar * sr - ai * si + xs_re[r, sl], ar * si + ai * sr + xs_im[r, sl])
            xs_re[r, sl] = sr
            xs_im[r, sl] = si

    gate = dot(x1b, wg_ref[...])

    y_e, y_o = [], []
    for j in range(N_BLOCKS):
        sl = slice(j * BLOCK_ST, (j + 1) * BLOCK_ST)
        y2 = (dot(xs_re[0:HALF, sl].astype(bf16), wre_ref[j])
              + dot(xs_im[0:HALF, sl].astype(bf16), wim_ref[j]) + dot(u2[j], kd_ref[j]))
        y_e.append(y2[:, :BLOCK_CH])
        y_o.append(y2[:, BLOCK_CH:])
    xs_re[0:SUBLANES, :] = xs_re[HALF:HALF + SUBLANES, :]
    xs_im[0:SUBLANES, :] = xs_im[HALF:HALF + SUBLANES, :]
    y_a = jnp.concatenate([jnp.concatenate(y_e, axis=-1), jnp.concatenate(y_o, axis=-1)], axis=0)
    y_a = y_a + dskip_ref[...] * u

    up = dot(x1b, wu_ref[...])

    v = c_gate * h
    ve_ext[SUBLANES:SUBLANES + HALF, :] = v[:HALF]
    vo_ext[SUBLANES:SUBLANES + HALF, :] = v[HALF:]
    w0, w1, w2 = convw_ref[0:1, :], convw_ref[1:2, :], convw_ref[2:3, :]
    z_e = w0 * ve_ext[0:HALF, :] + w1 * vo_ext[0:HALF, :] + w2 * v[:HALF]
    z_o = w0 * vo_ext[0:HALF, :] + w1 * v[:HALF] + w2 * v[HALF:]
    ve_ext[0:SUBLANES, :] = ve_ext[HALF:HALF + SUBLANES, :]
    vo_ext[0:SUBLANES, :] = vo_ext[HALF:HALF + SUBLANES, :]
    bz = (b_gate * jnp.concatenate([z_e, z_o], axis=0)).astype(bf16)

    g = jax.nn.gelu(y_a)
    glu = dot(g.astype(bf16), gluw_ref[...]) + glub_ref[...]
    y_b = dot(bz, wconv_ref[...])

    hid = (jax.nn.silu(gate) * up).astype(bf16)
    ffn_lo = dot(hid, wd_ref[:, 0:D_MODEL // 2])
    y_a = dot((g * jax.nn.sigmoid(glu)).astype(bf16), wssm_ref[...])
    ffn_hi = dot(hid, wd_ref[:, D_MODEL // 2:D_MODEL])
    ffn = jnp.concatenate([ffn_lo, ffn_hi], axis=-1)
    merged = jax.nn.sigmoid(gate_a) * y_a + jax.nn.sigmoid(gate_b) * y_b
    mix = dot(merged.astype(bf16), wo_ref[...])
    out = _layer_norm(ALPHA * x1 + ffn, ln2g_ref[...], ln2b_ref[...])
    return out, ALPHA * x + mix


N_MIXER_PARAMS = 14
N_FFN_PARAMS = 5
CAST_PARAMS = (0, 6, 8, 10, 11, 14, 15, 16)
CAST_ROWS = 128


def _load_weights_bf16(hbm_refs, vmem_refs, stage, sem):
    chunks = []
    for w_hbm, w_vmem in zip(hbm_refs, vmem_refs):
        rows, cols = w_hbm.shape
        assert rows % CAST_ROWS == 0 and cols <= stage.shape[-1]
        for r0 in range(0, rows, CAST_ROWS):
            chunks.append((w_hbm, w_vmem, r0, cols))

    def copy(k):
        w_hbm, _, r0, cols = chunks[k]
        return pltpu.make_async_copy(w_hbm.at[pl.ds(r0, CAST_ROWS), :],
                                     stage.at[k % 2, :, pl.ds(0, cols)], sem.at[k % 2])

    copy(0).start()
    for k in range(len(chunks)):
        if k + 1 < len(chunks):
            copy(k + 1).start()
        copy(k).wait()
        _, w_vmem, r0, cols = chunks[k]
        w_vmem[pl.ds(r0, CAST_ROWS), :] = stage[k % 2, :, 0:cols].astype(bf16)


def _layer_kernel(x_hbm, *refs):
    n_params = N_MIXER_PARAMS + N_FFN_PARAMS
    params = list(refs[:n_params])
    o_hbm, xbuf, xsem, obuf, osem, pre1_ref, xs_re, xs_im, ve_ext, vo_ext, stage, wsem = \
        refs[n_params:n_params + 12]
    ssm_refs = refs[n_params + 12:n_params + 17]
    w_vmem = refs[n_params + 17:]
    w_hbm = [params[p] for p in CAST_PARAMS]
    for p, w in zip(CAST_PARAMS, w_vmem):
        params[p] = w
    mixer_refs = params[:N_MIXER_PARAMS]
    ffn_refs = params[N_MIXER_PARAMS:]
    i = pl.program_id(0)
    slot = i % 2
    ffn_tile = i - 1
    ffn_slot = (i + 1) % 2

    @pl.when(i == 0)
    def _():
        for cp in _tile_copies(x_hbm, xbuf, xsem, 0, 0, True):
            cp.start()
        zeros = jnp.zeros((SUBLANES, N_STATE), f32)
        xs_re[0:SUBLANES, :] = zeros
        xs_im[0:SUBLANES, :] = zeros
        ve_ext[0:SUBLANES, :] = jnp.zeros((SUBLANES, CONV_WIDTH), f32)
        vo_ext[0:SUBLANES, :] = jnp.zeros((SUBLANES, CONV_WIDTH), f32)
        pre1_ref[...] = jnp.zeros_like(pre1_ref)
        _ssm_prep(*mixer_refs[2:5], *ssm_refs)
        _load_weights_bf16(w_hbm, w_vmem, stage, wsem)

    @pl.when(i + 1 < N_TILES)
    def _():
        for cp in _tile_copies(x_hbm, xbuf, xsem, i + 1, 1 - slot, True):
            cp.start()

    @pl.when(i < N_TILES)
    def _():
        _tile_wait(xbuf, xsem, slot)

    @pl.when(ffn_tile >= 2)
    def _():
        _tile_wait(obuf, osem, ffn_slot)

    y, pre1 = _layer_tile(pre1_ref[...], _split_phases(xbuf[slot]), mixer_refs, ssm_refs, ffn_refs,
                          xs_re, xs_im, ve_ext, vo_ext)
    obuf[ffn_slot] = _merge_phases(y)
    pre1_ref[...] = pre1

    @pl.when(i > 0)
    def _():
        for cp in _tile_copies(o_hbm, obuf, osem, ffn_tile, ffn_slot, False):
            cp.start()

    @pl.when(i == N_TILES)
    def _():
        _tile_wait(obuf, osem, 1 - ffn_slot)
        _tile_wait(obuf, osem, ffn_slot)


def _const_spec(shape):
    nd = len(shape)
    return pl.BlockSpec(shape, lambda i, nd=nd: (0,) * nd, pipeline_mode=pl.Buffered(1))


def _ssm_prep(vec_ref, b_ref, c_ref, bmat_ref, a2_ref, wre_ref, wim_ref, kd_ref):
    lam_re, lam_im = vec_ref[0:1, :], vec_ref[1:2, :]
    dt = jnp.exp(vec_ref[2:3, :])
    mag = jnp.exp(lam_re * dt)
    a_re = mag * jnp.cos(lam_im * dt)
    a_im = mag * jnp.sin(lam_im * dt)
    den = lam_re * lam_re + lam_im * lam_im
    num_re = a_re - 1.0
    fr = (num_re * lam_re + a_im * lam_im) / den
    fi = (a_im * lam_re - num_re * lam_im) / den
    a2_re = a_re * a_re - a_im * a_im
    a2_im = 2.0 * a_re * a_im
    a2_ref[0:1, :] = a2_re
    a2_ref[1:2, :] = a2_im

    def cmul(xr, xi, yr, yi):
        return xr * yr - xi * yi, xr * yi + xi * yr

    b_re, b_im, c_re, c_im = b_ref[0], b_ref[1], c_ref[0], c_ref[1]
    bb_re, bb_im = cmul(fr, fi, b_re, b_im)
    abb_re, abb_im = cmul(a_re, a_im, bb_re, bb_im)
    ca_re, ca_im = cmul(a_re, a_im, c_re, c_im)
    ca2_re, ca2_im = cmul(a2_re, a2_im, c_re, c_im)

    row_group = lax.broadcasted_iota(jnp.int32, (BLOCK_CH, BLOCK_ST), 0) // SSM_GROUP
    col_group = lax.broadcasted_iota(jnp.int32, (BLOCK_CH, BLOCK_ST), 1) // SSM_STATE
    diag = row_group == col_group

    for j in range(N_BLOCKS):
        def bd(m):
            blk = m[:, j * BLOCK_ST:(j + 1) * BLOCK_ST]
            return jnp.where(diag, jnp.concatenate([blk] * GROUPS_PER_BLOCK, axis=0), 0.0)

        def nt(x, y):
            return lax.dot_general(x, y, (((1,), (1,)), ((), ())), precision=lax.Precision.HIGHEST,
                                   preferred_element_type=f32)

        bbr, bbi, abr, abi = bd(bb_re), bd(bb_im), bd(abb_re), bd(abb_im)
        cr, ci = bd(c_re), bd(c_im)
        bmat_ref[j, 0:BLOCK_CH, 0:BLOCK_ST] = abr.astype(bf16)
        bmat_ref[j, 0:BLOCK_CH, BLOCK_ST:2 * BLOCK_ST] = abi.astype(bf16)
        bmat_ref[j, BLOCK_CH:2 * BLOCK_CH, 0:BLOCK_ST] = bbr.astype(bf16)
        bmat_ref[j, BLOCK_CH:2 * BLOCK_CH, BLOCK_ST:2 * BLOCK_ST] = bbi.astype(bf16)
        wre_ref[j, :, 0:BLOCK_CH] = bd(ca_re).T.astype(bf16)
        wre_ref[j, :, BLOCK_CH:2 * BLOCK_CH] = bd(ca2_re).T.astype(bf16)
        wim_ref[j, :, 0:BLOCK_CH] = (-bd(ca_im)).T.astype(bf16)
        wim_ref[j, :, BLOCK_CH:2 * BLOCK_CH] = (-bd(ca2_im)).T.astype(bf16)
        k0 = nt(bbr, cr) - nt(bbi, ci)
        k1 = nt(abr, cr) - nt(abi, ci)
        kd_ref[j, 0:BLOCK_CH, 0:BLOCK_CH] = k0.astype(bf16)
        kd_ref[j, 0:BLOCK_CH, BLOCK_CH:2 * BLOCK_CH] = k1.astype(bf16)
        kd_ref[j, BLOCK_CH:2 * BLOCK_CH, 0:BLOCK_CH] = jnp.zeros((BLOCK_CH, BLOCK_CH), bf16)
        kd_ref[j, BLOCK_CH:2 * BLOCK_CH, BLOCK_CH:2 * BLOCK_CH] = k0.astype(bf16)


def _ssm_inputs(lam_re, lam_im, log_dt, b_re, b_im, c_re, c_im):
    ldt = jnp.broadcast_to(log_dt[:, None], (SSM_GROUPS, SSM_STATE))
    vec = jnp.stack([lam_re, lam_im, ldt]).astype(f32).reshape(3, N_STATE)
    b_t = jnp.transpose(jnp.stack([b_re, b_im]).astype(f32), (0, 3, 1, 2))
    c_t = jnp.transpose(jnp.stack([c_re, c_im]).astype(f32), (0, 2, 1, 3))
    return vec, b_t.reshape(2, SSM_GROUP, N_STATE), c_t.reshape(2, SSM_GROUP, N_STATE)


def _layer(x, w_in, b_in, lam_re, lam_im, log_dt, b_re, b_im, c_re, c_im, d_skip, glu_w, glu_b,
           w_ssm_out, conv_w, w_conv_out, w_o, ln1_g, ln1_b, w_gate, w_up, w_down, ln2_g, ln2_b):
    row = lambda v: v.reshape(1, -1).astype(f32)
    w32 = lambda w: w.astype(f32)
    mixer_in = (w32(w_in), row(b_in),
                *_ssm_inputs(lam_re, lam_im, log_dt, b_re, b_im, c_re, c_im), row(d_skip),
                w32(glu_w), row(glu_b), w32(w_ssm_out), conv_w.astype(f32),
                w32(w_conv_out), w32(w_o), row(ln1_g), row(ln1_b))
    ffn_in = (w32(w_gate), w32(w_up), w32(w_down), row(ln2_g), row(ln2_b))
    assert len(mixer_in) == N_MIXER_PARAMS and len(ffn_in) == N_FFN_PARAMS
    params = mixer_in + ffn_in
    param_specs = [pl.BlockSpec(memory_space=pl.ANY) if p in CAST_PARAMS else _const_spec(a.shape)
                   for p, a in enumerate(params)]
    stage_cols = max(params[p].shape[1] for p in CAST_PARAMS)
    return pl.pallas_call(
        _layer_kernel,
        grid=(N_TILES + 1,),
        in_specs=[pl.BlockSpec(memory_space=pl.ANY)] + param_specs,
        out_specs=pl.BlockSpec(memory_space=pl.ANY),
        out_shape=jax.ShapeDtypeStruct((BATCH, SEQ, D_MODEL), f32),
        scratch_shapes=[pltpu.VMEM((2, T_TILE, BATCH, D_MODEL), f32),
                        pltpu.SemaphoreType.DMA((2,)),
                        pltpu.VMEM((2, T_TILE, BATCH, D_MODEL), f32),
                        pltpu.SemaphoreType.DMA((2,)),
                        pltpu.VMEM((R_TILE, D_MODEL), f32),
                        pltpu.VMEM((HALF + SUBLANES, N_STATE), f32),
                        pltpu.VMEM((HALF + SUBLANES, N_STATE), f32),
                        pltpu.VMEM((HALF + SUBLANES, CONV_WIDTH), f32),
                        pltpu.VMEM((HALF + SUBLANES, CONV_WIDTH), f32),
                        pltpu.VMEM((2, CAST_ROWS, stage_cols), f32),
                        pltpu.SemaphoreType.DMA((2,)),
                        pltpu.VMEM((N_BLOCKS, 2 * BLOCK_CH, 2 * BLOCK_ST), bf16),
                        pltpu.VMEM((2, N_STATE), f32),
                        pltpu.VMEM((N_BLOCKS, BLOCK_ST, 2 * BLOCK_CH), bf16),
                        pltpu.VMEM((N_BLOCKS, BLOCK_ST, 2 * BLOCK_CH), bf16),
                        pltpu.VMEM((N_BLOCKS, 2 * BLOCK_CH, 2 * BLOCK_CH), bf16)]
                       + [pltpu.VMEM(params[p].shape, bf16) for p in CAST_PARAMS],
        compiler_params=pltpu.CompilerParams(dimension_semantics=("arbitrary",),
                                             vmem_limit_bytes=VMEM_LIMIT_BYTES),
        name="layer",
    )(x, *params)


def kernel(x, w_in, b_in, ssm_lambda_re, ssm_lambda_im, ssm_log_dt, ssm_b_re, ssm_b_im, ssm_c_re, ssm_c_im, ssm_d, glu_w, glu_b, w_ssm_out, conv_w, w_conv_out, w_o, ln1_g, ln1_b, w_gate, w_up, w_down, ln2_g, ln2_b):
    assert x.shape == (BATCH, SEQ, D_MODEL) and BATCH == SUBLANES
    for l in range(w_in.shape[0]):
        x = _layer(x, w_in[l], b_in[l], ssm_lambda_re[l], ssm_lambda_im[l], ssm_log_dt[l],
                   ssm_b_re[l], ssm_b_im[l], ssm_c_re[l], ssm_c_im[l], ssm_d[l], glu_w[l],
                   glu_b[l], w_ssm_out[l], conv_w[l], w_conv_out[l], w_o[l], ln1_g[l], ln1_b[l],
                   w_gate[l], w_up[l], w_down[l], ln2_g[l], ln2_b[l])
    return x
```

```python
import jax
import jax.numpy as jnp
from jax import lax
from jax.experimental import pallas as pl
from jax.experimental.pallas import tpu as pltpu

D_MODEL = 1024
BATCH = 8
SEQ = 4096
SSM_WIDTH = D_MODEL // 2
SSM_GROUP = 16
SSM_GROUPS = SSM_WIDTH // SSM_GROUP
SSM_STATE = 64
CONV_WIDTH = D_MODEL // 2
CONV_K = 3
FFN_HIDDEN = 2816
IN_COLS = SSM_WIDTH + 3 * CONV_WIDTH + 2 * D_MODEL
DEPTH = 1
ALPHA = (2.0 * DEPTH) ** 0.25
LN_EPS = 1e-5

SUBLANES = 8
VMEM_LIMIT_BYTES = 60 * 1024 * 1024

GROUPS_PER_BLOCK = 8
N_BLOCKS = SSM_GROUPS // GROUPS_PER_BLOCK
BLOCK_CH = GROUPS_PER_BLOCK * SSM_GROUP
BLOCK_ST = GROUPS_PER_BLOCK * SSM_STATE
N_STATE = SSM_GROUPS * SSM_STATE

PHASES = 2
T_TILE = 32
R_TILE = T_TILE * BATCH
H_TILE = T_TILE // PHASES
HALF = H_TILE * SUBLANES
N_TILES = SEQ // T_TILE
DMA_TILES = 2
G_STEPS = DMA_TILES * T_TILE
N_GROUPS = N_TILES // DMA_TILES
assert SEQ % G_STEPS == 0 and N_GROUPS >= 2 and T_TILE % PHASES == 0

f32 = jnp.float32
bf16 = jnp.bfloat16


def _layer_norm(y, g, b):
    mu = jnp.mean(y, axis=-1, keepdims=True)
    var = jnp.mean(jnp.square(y - mu), axis=-1, keepdims=True)
    return (y - mu) * lax.rsqrt(var + LN_EPS) * g + b


def _group_copies(hbm, buf, sem, group, slot, to_vmem):
    copies = []
    for b in range(BATCH):
        h = hbm.at[b, pl.ds(group * G_STEPS, G_STEPS), :]
        v = buf.at[slot, :, b, :]
        src, dst = (h, v) if to_vmem else (v, h)
        copies.append(pltpu.make_async_copy(src, dst, sem.at[slot]))
    return copies


def _group_wait(buf, sem, slot):
    pltpu.make_async_copy(buf.at[slot], buf.at[slot], sem.at[slot]).wait()


def _split_phases(x3):
    x4 = x3.reshape(H_TILE, PHASES, SUBLANES, x3.shape[-1])
    return jnp.concatenate([x4[:, ph].reshape(HALF, x3.shape[-1]) for ph in range(PHASES)], axis=0)


def _merge_phases(y):
    parts = [y[ph * HALF:(ph + 1) * HALF].reshape(H_TILE, SUBLANES, y.shape[-1])
             for ph in range(PHASES)]
    return jnp.stack(parts, axis=1).reshape(T_TILE, SUBLANES, y.shape[-1])


def _layer_tile(pre1, x, mixer_refs, ssm_refs, ffn_refs, xs_re, xs_im, ve_ext, vo_ext):
    (win_ref, bin_ref, dskip_ref, gluw_ref, glub_ref, wssm_ref, convw_ref, wconv_ref, wo_ref,
     ln1g_ref, ln1b_ref) = mixer_refs[:2] + mixer_refs[5:]
    bmat_ref, a2_ref, wre_ref, wim_ref, kd_ref = ssm_refs
    wg_ref, wu_ref, wd_ref, ln2g_ref, ln2b_ref = ffn_refs
    dot = lambda a, w: jnp.dot(a, w, preferred_element_type=f32)
    x1 = _layer_norm(pre1, ln1g_ref[...], ln1b_ref[...])
    x1b = x1.astype(bf16)
    xb = x.astype(bf16)

    def proj(lo, hi):
        return dot(xb, win_ref[:, lo:hi]) + bin_ref[:, lo:hi]

    u = proj(0, SSM_WIDTH)
    h = proj(SSM_WIDTH, SSM_WIDTH + CONV_WIDTH)
    ub = u.astype(bf16)
    u2 = []
    for j in range(N_BLOCKS):
        cs = slice(j * BLOCK_CH, (j + 1) * BLOCK_CH)
        u2.append(jnp.concatenate([ub[:HALF, cs], ub[HALF:, cs]], axis=-1))
        bu = dot(u2[j], bmat_ref[j])
        xs_re[SUBLANES:SUBLANES + HALF, j * BLOCK_ST:(j + 1) * BLOCK_ST] = bu[:, :BLOCK_ST]
        xs_im[SUBLANES:SUBLANES + HALF, j * BLOCK_ST:(j + 1) * BLOCK_ST] = bu[:, BLOCK_ST:]

    o4 = SSM_WIDTH + 3 * CONV_WIDTH
    c_gate = proj(SSM_WIDTH + CONV_WIDTH, SSM_WIDTH + 2 * CONV_WIDTH)
    b_gate = proj(SSM_WIDTH + 2 * CONV_WIDTH, o4)
    gate_a = proj(o4, o4 + D_MODEL)
    gate_b = proj(o4 + D_MODEL, o4 + 2 * D_MODEL)

    for j in range(N_BLOCKS):
        sl = slice(j * BLOCK_ST, (j + 1) * BLOCK_ST)
        ar = jnp.broadcast_to(a2_ref[0:1, sl], (SUBLANES, BLOCK_ST))
        ai = jnp.broadcast_to(a2_ref[1:2, sl], (SUBLANES, BLOCK_ST))
        sr, si = xs_re[0:SUBLANES, sl], xs_im[0:SUBLANES, sl]
        for k in range(H_TILE):
            r = slice((k + 1) * SUBLANES, (k + 2) * SUBLANES)
            sr, si = (ar * sr - ai * si + xs_re[r, sl], ar * si + ai * sr + xs_im[r, sl])
            xs_re[r, sl] = sr
            xs_im[r, sl] = si

    gate = dot(x1b, wg_ref[...])

    y_e, y_o = [], []
    for j in range(N_BLOCKS):
        sl = slice(j * BLOCK_ST, (j + 1) * BLOCK_ST)
        y2 = (dot(xs_re[0:HALF, sl].astype(bf16), wre_ref[j])
              + dot(xs_im[0:HALF, sl].astype(bf16), wim_ref[j]) + dot(u2[j], kd_ref[j]))
        y_e.append(y2[:, :BLOCK_CH])
        y_o.append(y2[:, BLOCK_CH:])
    xs_re[0:SUBLANES, :] = xs_re[HALF:HALF + SUBLANES, :]
    xs_im[0:SUBLANES, :] = xs_im[HALF:HALF + SUBLANES, :]
    y_a = jnp.concatenate([jnp.concatenate(y_e, axis=-1), jnp.concatenate(y_o, axis=-1)], axis=0)
    y_a = y_a + dskip_ref[...] * u

    up = dot(x1b, wu_ref[...])

    v = c_gate * h
    ve_ext[SUBLANES:SUBLANES + HALF, :] = v[:HALF]
    vo_ext[SUBLANES:SUBLANES + HALF, :] = v[HALF:]
    w0, w1, w2 = convw_ref[0:1, :], convw_ref[1:2, :], convw_ref[2:3, :]
    z_e = w0 * ve_ext[0:HALF, :] + w1 * vo_ext[0:HALF, :] + w2 * v[:HALF]
    z_o = w0 * vo_ext[0:HALF, :] + w1 * v[:HALF] + w2 * v[HALF:]
    ve_ext[0:SUBLANES, :] = ve_ext[HALF:HALF + SUBLANES, :]
    vo_ext[0:SUBLANES, :] = vo_ext[HALF:HALF + SUBLANES, :]
    bz = (b_gate * jnp.concatenate([z_e, z_o], axis=0)).astype(bf16)

    g = jax.nn.gelu(y_a)
    glu = dot(g.astype(bf16), gluw_ref[...]) + glub_ref[...]
    y_b = dot(bz, wconv_ref[...])

    hid = (jax.nn.silu(gate) * up).astype(bf16)
    ffn_lo = dot(hid, wd_ref[:, 0:D_MODEL // 2])
    y_a = dot((g * jax.nn.sigmoid(glu)).astype(bf16), wssm_ref[...])
    ffn_hi = dot(hid, wd_ref[:, D_MODEL // 2:D_MODEL])
    ffn = jnp.concatenate([ffn_lo, ffn_hi], axis=-1)
    merged = jax.nn.sigmoid(gate_a) * y_a + jax.nn.sigmoid(gate_b) * y_b
    mix = dot(merged.astype(bf16), wo_ref[...])
    out = _layer_norm(ALPHA * x1 + ffn, ln2g_ref[...], ln2b_ref[...])
    return out, ALPHA * x + mix


N_MIXER_PARAMS = 14
N_FFN_PARAMS = 5
CAST_PARAMS = (0, 6, 8, 10, 11, 14, 15, 16)
CAST_ROWS = 128


def _load_weights_bf16(hbm_refs, vmem_refs, stage, sem):
    chunks = []
    for w_hbm, w_vmem in zip(hbm_refs, vmem_refs):
        rows, cols = w_hbm.shape
        assert rows % CAST_ROWS == 0 and cols <= stage.shape[-1]
        for r0 in range(0, rows, CAST_ROWS):
            chunks.append((w_hbm, w_vmem, r0, cols))

    def copy(k):
        w_hbm, _, r0, cols = chunks[k]
        return pltpu.make_async_copy(w_hbm.at[pl.ds(r0, CAST_ROWS), :],
                                     stage.at[k % 2, :, pl.ds(0, cols)], sem.at[k % 2])

    copy(0).start()
    for k in range(len(chunks)):
        if k + 1 < len(chunks):
            copy(k + 1).start()
        copy(k).wait()
        _, w_vmem, r0, cols = chunks[k]
        w_vmem[pl.ds(r0, CAST_ROWS), :] = stage[k % 2, :, 0:cols].astype(bf16)


def _layer_kernel(x_hbm, *refs):
    n_params = N_MIXER_PARAMS + N_FFN_PARAMS
    params = list(refs[:n_params])
    o_hbm, xbuf, xsem, obuf, osem, pre1_ref, xs_re, xs_im, ve_ext, vo_ext, stage, wsem = \
        refs[n_params:n_params + 12]
    ssm_refs = refs[n_params + 12:n_params + 17]
    w_vmem = refs[n_params + 17:]
    w_hbm = [params[p] for p in CAST_PARAMS]
    for p, w in zip(CAST_PARAMS, w_vmem):
        params[p] = w
    mixer_refs = params[:N_MIXER_PARAMS]
    ffn_refs = params[N_MIXER_PARAMS:]
    i = pl.program_id(0)
    group, part = i // DMA_TILES, i % DMA_TILES
    slot = group % 2
    ffn_tile = i - 1
    out_group, out_part = ffn_tile // DMA_TILES, ffn_tile % DMA_TILES
    out_slot = out_group % 2

    @pl.when(i == 0)
    def _():
        for cp in _group_copies(x_hbm, xbuf, xsem, 0, 0, True):
            cp.start()
        zeros = jnp.zeros((SUBLANES, N_STATE), f32)
        xs_re[0:SUBLANES, :] = zeros
        xs_im[0:SUBLANES, :] = zeros
        ve_ext[0:SUBLANES, :] = jnp.zeros((SUBLANES, CONV_WIDTH), f32)
        vo_ext[0:SUBLANES, :] = jnp.zeros((SUBLANES, CONV_WIDTH), f32)
        pre1_ref[...] = jnp.zeros_like(pre1_ref)
        _ssm_prep(*mixer_refs[2:5], *ssm_refs)
        _load_weights_bf16(w_hbm, w_vmem, stage, wsem)

    @pl.when((part == 0) & (group + 1 < N_GROUPS))
    def _():
        for cp in _group_copies(x_hbm, xbuf, xsem, group + 1, 1 - slot, True):
            cp.start()

    @pl.when((part == 0) & (group < N_GROUPS))
    def _():
        _group_wait(xbuf, xsem, slot)

    @pl.when((out_part == 0) & (out_group >= 2))
    def _():
        _group_wait(obuf, osem, out_slot)

    x_tile = xbuf[slot, pl.ds(part * T_TILE, T_TILE)]
    y, pre1 = _layer_tile(pre1_ref[...], _split_phases(x_tile), mixer_refs, ssm_refs, ffn_refs,
                          xs_re, xs_im, ve_ext, vo_ext)
    obuf[out_slot, pl.ds(out_part * T_TILE, T_TILE)] = _merge_phases(y)
    pre1_ref[...] = pre1

    @pl.when((ffn_tile >= 0) & (out_part == DMA_TILES - 1))
    def _():
        for cp in _group_copies(o_hbm, obuf, osem, out_group, out_slot, False):
            cp.start()

    @pl.when(i == N_TILES)
    def _():
        _group_wait(obuf, osem, 1 - out_slot)
        _group_wait(obuf, osem, out_slot)


def _const_spec(shape):
    nd = len(shape)
    return pl.BlockSpec(shape, lambda i, nd=nd: (0,) * nd, pipeline_mode=pl.Buffered(1))


def _ssm_prep(vec_ref, b_ref, c_ref, bmat_ref, a2_ref, wre_ref, wim_ref, kd_ref):
    lam_re, lam_im = vec_ref[0:1, :], vec_ref[1:2, :]
    dt = jnp.exp(vec_ref[2:3, :])
    mag = jnp.exp(lam_re * dt)
    a_re = mag * jnp.cos(lam_im * dt)
    a_im = mag * jnp.sin(lam_im * dt)
    den = lam_re * lam_re + lam_im * lam_im
    num_re = a_re - 1.0
    fr = (num_re * lam_re + a_im * lam_im) / den
    fi = (a_im * lam_re - num_re * lam_im) / den
    a2_re = a_re * a_re - a_im * a_im
    a2_im = 2.0 * a_re * a_im
    a2_ref[0:1, :] = a2_re
    a2_ref[1:2, :] = a2_im

    def cmul(xr, xi, yr, yi):
        return xr * yr - xi * yi, xr * yi + xi * yr

    b_re, b_im, c_re, c_im = b_ref[0], b_ref[1], c_ref[0], c_ref[1]
    bb_re, bb_im = cmul(fr, fi, b_re, b_im)
    abb_re, abb_im = cmul(a_re, a_im, bb_re, bb_im)
    ca_re, ca_im = cmul(a_re, a_im, c_re, c_im)
    ca2_re, ca2_im = cmul(a2_re, a2_im, c_re, c_im)

    row_group = lax.broadcasted_iota(jnp.int32, (BLOCK_CH, BLOCK_ST), 0) // SSM_GROUP
    col_group = lax.broadcasted_iota(jnp.int32, (BLOCK_CH, BLOCK_ST), 1) // SSM_STATE
    diag = row_group == col_group

    for j in range(N_BLOCKS):
        def bd(m):
            blk = m[:, j * BLOCK_ST:(j + 1) * BLOCK_ST]
            return jnp.where(diag, jnp.concatenate([blk] * GROUPS_PER_BLOCK, axis=0), 0.0)

        def nt(x, y):
            return lax.dot_general(x, y, (((1,), (1,)), ((), ())), precision=lax.Precision.HIGHEST,
                                   preferred_element_type=f32)

        bbr, bbi, abr, abi = bd(bb_re), bd(bb_im), bd(abb_re), bd(abb_im)
        cr, ci = bd(c_re), bd(c_im)
        bmat_ref[j, 0:BLOCK_CH, 0:BLOCK_ST] = abr.astype(bf16)
        bmat_ref[j, 0:BLOCK_CH, BLOCK_ST:2 * BLOCK_ST] = abi.astype(bf16)
        bmat_ref[j, BLOCK_CH:2 * BLOCK_CH, 0:BLOCK_ST] = bbr.astype(bf16)
        bmat_ref[j, BLOCK_CH:2 * BLOCK_CH, BLOCK_ST:2 * BLOCK_ST] = bbi.astype(bf16)
        wre_ref[j, :, 0:BLOCK_CH] = bd(ca_re).T.astype(bf16)
        wre_ref[j, :, BLOCK_CH:2 * BLOCK_CH] = bd(ca2_re).T.astype(bf16)
        wim_ref[j, :, 0:BLOCK_CH] = (-bd(ca_im)).T.astype(bf16)
        wim_ref[j, :, BLOCK_CH:2 * BLOCK_CH] = (-bd(ca2_im)).T.astype(bf16)
        k0 = nt(bbr, cr) - nt(bbi, ci)
        k1 = nt(abr, cr) - nt(abi, ci)
        kd_ref[j, 0:BLOCK_CH, 0:BLOCK_CH] = k0.astype(bf16)
        kd_ref[j, 0:BLOCK_CH, BLOCK_CH:2 * BLOCK_CH] = k1.astype(bf16)
        kd_ref[j, BLOCK_CH:2 * BLOCK_CH, 0:BLOCK_CH] = jnp.zeros((BLOCK_CH, BLOCK_CH), bf16)
        kd_ref[j, BLOCK_CH:2 * BLOCK_CH, BLOCK_CH:2 * BLOCK_CH] = k0.astype(bf16)


def _ssm_inputs(lam_re, lam_im, log_dt, b_re, b_im, c_re, c_im):
    ldt = jnp.broadcast_to(log_dt[:, None], (SSM_GROUPS, SSM_STATE))
    vec = jnp.stack([lam_re, lam_im, ldt]).astype(f32).reshape(3, N_STATE)
    b_t = jnp.transpose(jnp.stack([b_re, b_im]).astype(f32), (0, 3, 1, 2))
    c_t = jnp.transpose(jnp.stack([c_re, c_im]).astype(f32), (0, 2, 1, 3))
    return vec, b_t.reshape(2, SSM_GROUP, N_STATE), c_t.reshape(2, SSM_GROUP, N_STATE)


def _layer(x, w_in, b_in, lam_re, lam_im, log_dt, b_re, b_im, c_re, c_im, d_skip, glu_w, glu_b,
           w_ssm_out, conv_w, w_conv_out, w_o, ln1_g, ln1_b, w_gate, w_up, w_down, ln2_g, ln2_b):
    row = lambda v: v.reshape(1, -1).astype(f32)
    w32 = lambda w: w.astype(f32)
    mixer_in = (w32(w_in), row(b_in),
                *_ssm_inputs(lam_re, lam_im, log_dt, b_re, b_im, c_re, c_im), row(d_skip),
                w32(glu_w), row(glu_b), w32(w_ssm_out), conv_w.astype(f32),
                w32(w_conv_out), w32(w_o), row(ln1_g), row(ln1_b))
    ffn_in = (w32(w_gate), w32(w_up), w32(w_down), row(ln2_g), row(ln2_b))
    assert len(mixer_in) == N_MIXER_PARAMS and len(ffn_in) == N_FFN_PARAMS
    params = mixer_in + ffn_in
    param_specs = [pl.BlockSpec(memory_space=pl.ANY) if p in CAST_PARAMS else _const_spec(a.shape)
                   for p, a in enumerate(params)]
    stage_cols = max(params[p].shape[1] for p in CAST_PARAMS)
    return pl.pallas_call(
        _layer_kernel,
        grid=(N_TILES + 1,),
        in_specs=[pl.BlockSpec(memory_space=pl.ANY)] + param_specs,
        out_specs=pl.BlockSpec(memory_space=pl.ANY),
        out_shape=jax.ShapeDtypeStruct((BATCH, SEQ, D_MODEL), f32),
        scratch_shapes=[pltpu.VMEM((2, G_STEPS, BATCH, D_MODEL), f32),
                        pltpu.SemaphoreType.DMA((2,)),
                        pltpu.VMEM((2, G_STEPS, BATCH, D_MODEL), f32),
                        pltpu.SemaphoreType.DMA((2,)),
                        pltpu.VMEM((R_TILE, D_MODEL), f32),
                        pltpu.VMEM((HALF + SUBLANES, N_STATE), f32),
                        pltpu.VMEM((HALF + SUBLANES, N_STATE), f32),
                        pltpu.VMEM((HALF + SUBLANES, CONV_WIDTH), f32),
                        pltpu.VMEM((HALF + SUBLANES, CONV_WIDTH), f32),
                        pltpu.VMEM((2, CAST_ROWS, stage_cols), f32),
                        pltpu.SemaphoreType.DMA((2,)),
                        pltpu.VMEM((N_BLOCKS, 2 * BLOCK_CH, 2 * BLOCK_ST), bf16),
                        pltpu.VMEM((2, N_STATE), f32),
                        pltpu.VMEM((N_BLOCKS, BLOCK_ST, 2 * BLOCK_CH), bf16),
                        pltpu.VMEM((N_BLOCKS, BLOCK_ST, 2 * BLOCK_CH), bf16),
                        pltpu.VMEM((N_BLOCKS, 2 * BLOCK_CH, 2 * BLOCK_CH), bf16)]
                       + [pltpu.VMEM(params[p].shape, bf16) for p in CAST_PARAMS],
        compiler_params=pltpu.CompilerParams(dimension_semantics=("arbitrary",),
                                             vmem_limit_bytes=VMEM_LIMIT_BYTES),
        name="layer",
    )(x, *params)


def kernel(x, w_in, b_in, ssm_lambda_re, ssm_lambda_im, ssm_log_dt, ssm_b_re, ssm_b_im, ssm_c_re, ssm_c_im, ssm_d, glu_w, glu_b, w_ssm_out, conv_w, w_conv_out, w_o, ln1_g, ln1_b, w_gate, w_up, w_down, ln2_g, ln2_b):
    assert x.shape == (BATCH, SEQ, D_MODEL) and BATCH == SUBLANES
    for l in range(w_in.shape[0]):
        x = _layer(x, w_in[l], b_in[l], ssm_lambda_re[l], ssm_lambda_im[l], ssm_log_dt[l],
                   ssm_b_re[l], ssm_b_im[l], ssm_c_re[l], ssm_c_im[l], ssm_d[l], glu_w[l],
                   glu_b[l], w_ssm_out[l], conv_w[l], w_conv_out[l], w_o[l], ln1_g[l], ln1_b[l],
                   w_gate[l], w_up[l], w_down[l], ln2_g[l], ln2_b[l])
    return x
```

```python
import jax
import jax.numpy as jnp
from jax import lax
from jax.experimental import pallas as pl
from jax.experimental.pallas import tpu as pltpu

D_MODEL = 1024
BATCH = 8
SEQ = 4096
SSM_WIDTH = D_MODEL // 2
SSM_GROUP = 16
SSM_GROUPS = SSM_WIDTH // SSM_GROUP
SSM_STATE = 64
CONV_WIDTH = D_MODEL // 2
CONV_K = 3
FFN_HIDDEN = 2816
IN_COLS = SSM_WIDTH + 3 * CONV_WIDTH + 2 * D_MODEL
DEPTH = 1
ALPHA = (2.0 * DEPTH) ** 0.25
LN_EPS = 1e-5

SUBLANES = 8
VMEM_LIMIT_BYTES = 60 * 1024 * 1024

GROUPS_PER_BLOCK = 8
N_BLOCKS = SSM_GROUPS // GROUPS_PER_BLOCK
BLOCK_CH = GROUPS_PER_BLOCK * SSM_GROUP
BLOCK_ST = GROUPS_PER_BLOCK * SSM_STATE
N_STATE = SSM_GROUPS * SSM_STATE

PHASES = 2
T_TILE = 32
R_TILE = T_TILE * BATCH
H_TILE = T_TILE // PHASES
HALF = H_TILE * SUBLANES
N_TILES = SEQ // T_TILE
assert N_TILES >= 2 and SEQ % T_TILE == 0 and T_TILE % PHASES == 0

f32 = jnp.float32
bf16 = jnp.bfloat16


def _layer_norm(y, g, b):
    mu = jnp.mean(y, axis=-1, keepdims=True)
    var = jnp.mean(jnp.square(y - mu), axis=-1, keepdims=True)
    return (y - mu) * lax.rsqrt(var + LN_EPS) * g + b


def _split_phases(x3):
    x4 = x3.reshape(H_TILE, PHASES, SUBLANES, x3.shape[-1])
    return jnp.concatenate([x4[:, ph].reshape(HALF, x3.shape[-1]) for ph in range(PHASES)], axis=0)


def _merge_phases(y):
    parts = [y[ph * HALF:(ph + 1) * HALF].reshape(H_TILE, SUBLANES, y.shape[-1])
             for ph in range(PHASES)]
    return jnp.stack(parts, axis=1).reshape(T_TILE, SUBLANES, y.shape[-1])


def _layer_tile(pre1, x, mixer_refs, ssm_refs, ffn_refs, xs_re, xs_im, ve_ext, vo_ext):
    (win_ref, bin_ref, dskip_ref, gluw_ref, glub_ref, wssm_ref, convw_ref, wconv_ref, wo_ref,
     ln1g_ref, ln1b_ref) = mixer_refs[:2] + mixer_refs[5:]
    bmat_ref, a2_ref, wre_ref, wim_ref, kd_ref = ssm_refs
    wg_ref, wu_ref, wd_ref, ln2g_ref, ln2b_ref = ffn_refs
    dot = lambda a, w: jnp.dot(a, w, preferred_element_type=f32)
    x1 = _layer_norm(pre1, ln1g_ref[...], ln1b_ref[...])
    x1b = x1.astype(bf16)
    xb = x.astype(bf16)

    def proj(lo, hi):
        return dot(xb, win_ref[:, lo:hi]) + bin_ref[:, lo:hi]

    u = proj(0, SSM_WIDTH)
    h = proj(SSM_WIDTH, SSM_WIDTH + CONV_WIDTH)
    ub = u.astype(bf16)
    u2 = []
    for j in range(N_BLOCKS):
        cs = slice(j * BLOCK_CH, (j + 1) * BLOCK_CH)
        u2.append(jnp.concatenate([ub[:HALF, cs], ub[HALF:, cs]], axis=-1))
        bu = dot(u2[j], bmat_ref[j])
        xs_re[SUBLANES:SUBLANES + HALF, j * BLOCK_ST:(j + 1) * BLOCK_ST] = bu[:, :BLOCK_ST]
        xs_im[SUBLANES:SUBLANES + HALF, j * BLOCK_ST:(j + 1) * BLOCK_ST] = bu[:, BLOCK_ST:]

    o4 = SSM_WIDTH + 3 * CONV_WIDTH
    c_gate = proj(SSM_WIDTH + CONV_WIDTH, SSM_WIDTH + 2 * CONV_WIDTH)
    b_gate = proj(SSM_WIDTH + 2 * CONV_WIDTH, o4)
    gate_a = proj(o4, o4 + D_MODEL)
    gate_b = proj(o4 + D_MODEL, o4 + 2 * D_MODEL)

    for j in range(N_BLOCKS):
        sl = slice(j * BLOCK_ST, (j + 1) * BLOCK_ST)
        ar = jnp.broadcast_to(a2_ref[0:1, sl], (SUBLANES, BLOCK_ST))
        ai = jnp.broadcast_to(a2_ref[1:2, sl], (SUBLANES, BLOCK_ST))
        sr, si = xs_re[0:SUBLANES, sl], xs_im[0:SUBLANES, sl]
        for k in range(H_TILE):
            r = slice((k + 1) * SUBLANES, (k + 2) * SUBLANES)
            sr, si = (ar * sr - ai * si + xs_re[r, sl], ar * si + ai * sr + xs_im[r, sl])
            xs_re[r, sl] = sr
            xs_im[r, sl] = si

    gate = dot(x1b, wg_ref[...])

    y_e, y_o = [], []
    for j in range(N_BLOCKS):
        sl = slice(j * BLOCK_ST, (j + 1) * BLOCK_ST)
        y2 = (dot(xs_re[0:HALF, sl].astype(bf16), wre_ref[j])
              + dot(xs_im[0:HALF, sl].astype(bf16), wim_ref[j]) + dot(u2[j], kd_ref[j]))
        y_e.append(y2[:, :BLOCK_CH])
        y_o.append(y2[:, BLOCK_CH:])
    xs_re[0:SUBLANES, :] = xs_re[HALF:HALF + SUBLANES, :]
    xs_im[0:SUBLANES, :] = xs_im[HALF:HALF + SUBLANES, :]
    y_a = jnp.concatenate([jnp.concatenate(y_e, axis=-1), jnp.concatenate(y_o, axis=-1)], axis=0)
    y_a = y_a + dskip_ref[...] * u

    up = dot(x1b, wu_ref[...])

    v = c_gate * h
    ve_ext[SUBLANES:SUBLANES + HALF, :] = v[:HALF]
    vo_ext[SUBLANES:SUBLANES + HALF, :] = v[HALF:]
    w0, w1, w2 = convw_ref[0:1, :], convw_ref[1:2, :], convw_ref[2:3, :]
    z_e = w0 * ve_ext[0:HALF, :] + w1 * vo_ext[0:HALF, :] + w2 * v[:HALF]
    z_o = w0 * vo_ext[0:HALF, :] + w1 * v[:HALF] + w2 * v[HALF:]
    ve_ext[0:SUBLANES, :] = ve_ext[HALF:HALF + SUBLANES, :]
    vo_ext[0:SUBLANES, :] = vo_ext[HALF:HALF + SUBLANES, :]
    bz = (b_gate * jnp.concatenate([z_e, z_o], axis=0)).astype(bf16)

    g = jax.nn.gelu(y_a)
    glu = dot(g.astype(bf16), gluw_ref[...]) + glub_ref[...]
    y_b = dot(bz, wconv_ref[...])

    hid = (jax.nn.silu(gate) * up).astype(bf16)
    ffn_lo = dot(hid, wd_ref[:, 0:D_MODEL // 2])
    y_a = dot((g * jax.nn.sigmoid(glu)).astype(bf16), wssm_ref[...])
    ffn_hi = dot(hid, wd_ref[:, D_MODEL // 2:D_MODEL])
    ffn = jnp.concatenate([ffn_lo, ffn_hi], axis=-1)
    merged = jax.nn.sigmoid(gate_a) * y_a + jax.nn.sigmoid(gate_b) * y_b
    mix = dot(merged.astype(bf16), wo_ref[...])
    out = _layer_norm(ALPHA * x1 + ffn, ln2g_ref[...], ln2b_ref[...])
    return out, ALPHA * x + mix


N_MIXER_PARAMS = 14
N_FFN_PARAMS = 5
CAST_PARAMS = (0, 6, 8, 10, 11, 14, 15, 16)
CAST_ROWS = 128


def _load_weights_bf16(hbm_refs, vmem_refs, stage, sem):
    chunks = []
    for w_hbm, w_vmem in zip(hbm_refs, vmem_refs):
        rows, cols = w_hbm.shape
        assert rows % CAST_ROWS == 0 and cols <= stage.shape[-1]
        for r0 in range(0, rows, CAST_ROWS):
            chunks.append((w_hbm, w_vmem, r0, cols))

    def copy(k):
        w_hbm, _, r0, cols = chunks[k]
        return pltpu.make_async_copy(w_hbm.at[pl.ds(r0, CAST_ROWS), :],
                                     stage.at[k % 2, :, pl.ds(0, cols)], sem.at[k % 2])

    copy(0).start()
    for k in range(len(chunks)):
        if k + 1 < len(chunks):
            copy(k + 1).start()
        copy(k).wait()
        _, w_vmem, r0, cols = chunks[k]
        w_vmem[pl.ds(r0, CAST_ROWS), :] = stage[k % 2, :, 0:cols].astype(bf16)


def _layer_kernel(x_ref, *refs):
    n_params = N_MIXER_PARAMS + N_FFN_PARAMS
    params = list(refs[:n_params])
    o_ref, pre1_ref, xs_re, xs_im, ve_ext, vo_ext, stage, wsem = refs[n_params:n_params + 8]
    ssm_refs = refs[n_params + 8:n_params + 13]
    w_vmem = refs[n_params + 13:]
    w_hbm = [params[p] for p in CAST_PARAMS]
    for p, w in zip(CAST_PARAMS, w_vmem):
        params[p] = w
    mixer_refs = params[:N_MIXER_PARAMS]
    ffn_refs = params[N_MIXER_PARAMS:]
    @pl.when(pl.program_id(0) == 0)
    def _():
        zeros = jnp.zeros((SUBLANES, N_STATE), f32)
        xs_re[0:SUBLANES, :] = zeros
        xs_im[0:SUBLANES, :] = zeros
        ve_ext[0:SUBLANES, :] = jnp.zeros((SUBLANES, CONV_WIDTH), f32)
        vo_ext[0:SUBLANES, :] = jnp.zeros((SUBLANES, CONV_WIDTH), f32)
        pre1_ref[...] = jnp.zeros_like(pre1_ref)
        _ssm_prep(*mixer_refs[2:5], *ssm_refs)
        _load_weights_bf16(w_hbm, w_vmem, stage, wsem)

    x_slabs = jnp.swapaxes(x_ref[...], 0, 1)
    y, pre1 = _layer_tile(pre1_ref[...], _split_phases(x_slabs), mixer_refs, ssm_refs, ffn_refs,
                          xs_re, xs_im, ve_ext, vo_ext)
    o_ref[...] = jnp.swapaxes(_merge_phases(y), 0, 1)
    pre1_ref[...] = pre1


def _const_spec(shape):
    nd = len(shape)
    return pl.BlockSpec(shape, lambda i, nd=nd: (0,) * nd, pipeline_mode=pl.Buffered(1))


def _ssm_prep(vec_ref, b_ref, c_ref, bmat_ref, a2_ref, wre_ref, wim_ref, kd_ref):
    lam_re, lam_im = vec_ref[0:1, :], vec_ref[1:2, :]
    dt = jnp.exp(vec_ref[2:3, :])
    mag = jnp.exp(lam_re * dt)
    a_re = mag * jnp.cos(lam_im * dt)
    a_im = mag * jnp.sin(lam_im * dt)
    den = lam_re * lam_re + lam_im * lam_im
    num_re = a_re - 1.0
    fr = (num_re * lam_re + a_im * lam_im) / den
    fi = (a_im * lam_re - num_re * lam_im) / den
    a2_re = a_re * a_re - a_im * a_im
    a2_im = 2.0 * a_re * a_im
    a2_ref[0:1, :] = a2_re
    a2_ref[1:2, :] = a2_im

    def cmul(xr, xi, yr, yi):
        return xr * yr - xi * yi, xr * yi + xi * yr

    b_re, b_im, c_re, c_im = b_ref[0], b_ref[1], c_ref[0], c_ref[1]
    bb_re, bb_im = cmul(fr, fi, b_re, b_im)
    abb_re, abb_im = cmul(a_re, a_im, bb_re, bb_im)
    ca_re, ca_im = cmul(a_re, a_im, c_re, c_im)
    ca2_re, ca2_im = cmul(a2_re, a2_im, c_re, c_im)

    row_group = lax.broadcasted_iota(jnp.int32, (BLOCK_CH, BLOCK_ST), 0) // SSM_GROUP
    col_group = lax.broadcasted_iota(jnp.int32, (BLOCK_CH, BLOCK_ST), 1) // SSM_STATE
    diag = row_group == col_group

    for j in range(N_BLOCKS):
        def bd(m):
            blk = m[:, j * BLOCK_ST:(j + 1) * BLOCK_ST]
            return jnp.where(diag, jnp.concatenate([blk] * GROUPS_PER_BLOCK, axis=0), 0.0)

        def nt(x, y):
            return lax.dot_general(x, y, (((1,), (1,)), ((), ())), precision=lax.Precision.HIGHEST,
                                   preferred_element_type=f32)

        bbr, bbi, abr, abi = bd(bb_re), bd(bb_im), bd(abb_re), bd(abb_im)
        cr, ci = bd(c_re), bd(c_im)
        bmat_ref[j, 0:BLOCK_CH, 0:BLOCK_ST] = abr.astype(bf16)
        bmat_ref[j, 0:BLOCK_CH, BLOCK_ST:2 * BLOCK_ST] = abi.astype(bf16)
        bmat_ref[j, BLOCK_CH:2 * BLOCK_CH, 0:BLOCK_ST] = bbr.astype(bf16)
        bmat_ref[j, BLOCK_CH:2 * BLOCK_CH, BLOCK_ST:2 * BLOCK_ST] = bbi.astype(bf16)
        wre_ref[j, :, 0:BLOCK_CH] = bd(ca_re).T.astype(bf16)
        wre_ref[j, :, BLOCK_CH:2 * BLOCK_CH] = bd(ca2_re).T.astype(bf16)
        wim_ref[j, :, 0:BLOCK_CH] = (-bd(ca_im)).T.astype(bf16)
        wim_ref[j, :, BLOCK_CH:2 * BLOCK_CH] = (-bd(ca2_im)).T.astype(bf16)
        k0 = nt(bbr, cr) - nt(bbi, ci)
        k1 = nt(abr, cr) - nt(abi, ci)
        kd_ref[j, 0:BLOCK_CH, 0:BLOCK_CH] = k0.astype(bf16)
        kd_ref[j, 0:BLOCK_CH, BLOCK_CH:2 * BLOCK_CH] = k1.astype(bf16)
        kd_ref[j, BLOCK_CH:2 * BLOCK_CH, 0:BLOCK_CH] = jnp.zeros((BLOCK_CH, BLOCK_CH), bf16)
        kd_ref[j, BLOCK_CH:2 * BLOCK_CH, BLOCK_CH:2 * BLOCK_CH] = k0.astype(bf16)


def _ssm_inputs(lam_re, lam_im, log_dt, b_re, b_im, c_re, c_im):
    ldt = jnp.broadcast_to(log_dt[:, None], (SSM_GROUPS, SSM_STATE))
    vec = jnp.stack([lam_re, lam_im, ldt]).astype(f32).reshape(3, N_STATE)
    b_t = jnp.transpose(jnp.stack([b_re, b_im]).astype(f32), (0, 3, 1, 2))
    c_t = jnp.transpose(jnp.stack([c_re, c_im]).astype(f32), (0, 2, 1, 3))
    return vec, b_t.reshape(2, SSM_GROUP, N_STATE), c_t.reshape(2, SSM_GROUP, N_STATE)


def _layer(x, w_in, b_in, lam_re, lam_im, log_dt, b_re, b_im, c_re, c_im, d_skip, glu_w, glu_b,
           w_ssm_out, conv_w, w_conv_out, w_o, ln1_g, ln1_b, w_gate, w_up, w_down, ln2_g, ln2_b):
    row = lambda v: v.reshape(1, -1).astype(f32)
    w32 = lambda w: w.astype(f32)
    mixer_in = (w32(w_in), row(b_in),
                *_ssm_inputs(lam_re, lam_im, log_dt, b_re, b_im, c_re, c_im), row(d_skip),
                w32(glu_w), row(glu_b), w32(w_ssm_out), conv_w.astype(f32),
                w32(w_conv_out), w32(w_o), row(ln1_g), row(ln1_b))
    ffn_in = (w32(w_gate), w32(w_up), w32(w_down), row(ln2_g), row(ln2_b))
    assert len(mixer_in) == N_MIXER_PARAMS and len(ffn_in) == N_FFN_PARAMS
    params = mixer_in + ffn_in
    param_specs = [pl.BlockSpec(memory_space=pl.ANY) if p in CAST_PARAMS else _const_spec(a.shape)
                   for p, a in enumerate(params)]
    stage_cols = max(params[p].shape[1] for p in CAST_PARAMS)
    return pl.pallas_call(
        _layer_kernel,
        grid=(N_TILES + 1,),
        in_specs=[pl.BlockSpec((BATCH, T_TILE, D_MODEL),
                               lambda i: (0, jnp.minimum(i, N_TILES - 1), 0))] + param_specs,
        out_specs=pl.BlockSpec((BATCH, T_TILE, D_MODEL), lambda i: (0, jnp.maximum(i - 1, 0), 0)),
        out_shape=jax.ShapeDtypeStruct((BATCH, SEQ, D_MODEL), f32),
        scratch_shapes=[pltpu.VMEM((R_TILE, D_MODEL), f32),
                        pltpu.VMEM((HALF + SUBLANES, N_STATE), f32),
                        pltpu.VMEM((HALF + SUBLANES, N_STATE), f32),
                        pltpu.VMEM((HALF + SUBLANES, CONV_WIDTH), f32),
                        pltpu.VMEM((HALF + SUBLANES, CONV_WIDTH), f32),
                        pltpu.VMEM((2, CAST_ROWS, stage_cols), f32),
                        pltpu.SemaphoreType.DMA((2,)),
                        pltpu.VMEM((N_BLOCKS, 2 * BLOCK_CH, 2 * BLOCK_ST), bf16),
                        pltpu.VMEM((2, N_STATE), f32),
                        pltpu.VMEM((N_BLOCKS, BLOCK_ST, 2 * BLOCK_CH), bf16),
                        pltpu.VMEM((N_BLOCKS, BLOCK_ST, 2 * BLOCK_CH), bf16),
                        pltpu.VMEM((N_BLOCKS, 2 * BLOCK_CH, 2 * BLOCK_CH), bf16)]
                       + [pltpu.VMEM(params[p].shape, bf16) for p in CAST_PARAMS],
        compiler_params=pltpu.CompilerParams(dimension_semantics=("arbitrary",),
                                             vmem_limit_bytes=VMEM_LIMIT_BYTES),
        name="layer",
    )(x, *params)


def kernel(x, w_in, b_in, ssm_lambda_re, ssm_lambda_im, ssm_log_dt, ssm_b_re, ssm_b_im, ssm_c_re, ssm_c_im, ssm_d, glu_w, glu_b, w_ssm_out, conv_w, w_conv_out, w_o, ln1_g, ln1_b, w_gate, w_up, w_down, ln2_g, ln2_b):
    assert x.shape == (BATCH, SEQ, D_MODEL) and BATCH == SUBLANES
    for l in range(w_in.shape[0]):
        x = _layer(x, w_in[l], b_in[l], ssm_lambda_re[l], ssm_lambda_im[l], ssm_log_dt[l],
                   ssm_b_re[l], ssm_b_im[l], ssm_c_re[l], ssm_c_im[l], ssm_d[l], glu_w[l],
                   glu_b[l], w_ssm_out[l], conv_w[l], w_conv_out[l], w_o[l], ln1_g[l], ln1_b[l],
                   w_gate[l], w_up[l], w_down[l], ln2_g[l], ln2_b[l])
    return x
```

```python
import jax
import jax.numpy as jnp
from jax import lax
from jax.experimental import pallas as pl
from jax.experimental.pallas import tpu as pltpu

D_MODEL = 1024
BATCH = 8
SEQ = 4096
SSM_WIDTH = D_MODEL // 2
SSM_GROUP = 16
SSM_GROUPS = SSM_WIDTH // SSM_GROUP
SSM_STATE = 64
CONV_WIDTH = D_MODEL // 2
CONV_K = 3
FFN_HIDDEN = 2816
IN_COLS = SSM_WIDTH + 3 * CONV_WIDTH + 2 * D_MODEL
DEPTH = 1
ALPHA = (2.0 * DEPTH) ** 0.25
LN_EPS = 1e-5

SUBLANES = 8
VMEM_LIMIT_BYTES = 60 * 1024 * 1024

GROUPS_PER_BLOCK = 8
N_BLOCKS = SSM_GROUPS // GROUPS_PER_BLOCK
BLOCK_CH = GROUPS_PER_BLOCK * SSM_GROUP
BLOCK_ST = GROUPS_PER_BLOCK * SSM_STATE
N_STATE = SSM_GROUPS * SSM_STATE

PHASES = 2
T_TILE = 32
R_TILE = T_TILE * BATCH
H_TILE = T_TILE // PHASES
HALF = H_TILE * SUBLANES
N_TILES = SEQ // T_TILE
assert N_TILES >= 2 and SEQ % T_TILE == 0 and T_TILE % PHASES == 0

f32 = jnp.float32
bf16 = jnp.bfloat16


def _layer_norm(y, g, b):
    mu = jnp.mean(y, axis=-1, keepdims=True)
    var = jnp.mean(jnp.square(y - mu), axis=-1, keepdims=True)
    return (y - mu) * lax.rsqrt(var + LN_EPS) * g + b


def _tile_copies(hbm, buf, sem, tile, slot, to_vmem):
    copies = []
    for b in range(BATCH):
        h = hbm.at[b, pl.ds(tile * T_TILE, T_TILE), :]
        v = buf.at[slot, :, b, :]
        src, dst = (h, v) if to_vmem else (v, h)
        copies.append(pltpu.make_async_copy(src, dst, sem.at[slot]))
    return copies


def _tile_wait(buf, sem, slot):
    pltpu.make_async_copy(buf.at[slot], buf.at[slot], sem.at[slot]).wait()


def _split_phases(x3):
    x4 = x3.reshape(H_TILE, PHASES, SUBLANES, x3.shape[-1])
    return jnp.concatenate([x4[:, ph].reshape(HALF, x3.shape[-1]) for ph in range(PHASES)], axis=0)


def _merge_phases(y):
    parts = [y[ph * HALF:(ph + 1) * HALF].reshape(H_TILE, SUBLANES, y.shape[-1])
             for ph in range(PHASES)]
    return jnp.stack(parts, axis=1).reshape(T_TILE, SUBLANES, y.shape[-1])


def _layer_tile(pre1, x, mixer_refs, ssm_refs, ffn_refs, xs_re, xs_im, ve_ext, vo_ext):
    (win_ref, bin_ref, dskip_ref, gluw_ref, glub_ref, wssm_ref, convw_ref, wconv_ref, wo_ref,
     ln1g_ref, ln1b_ref) = mixer_refs[:2] + mixer_refs[5:]
    bmat_ref, a2_ref, wre_ref, wim_ref, kd_ref = ssm_refs
    wg_ref, wu_ref, wd_ref, ln2g_ref, ln2b_ref = ffn_refs
    dot = lambda a, w: jnp.dot(a, w, preferred_element_type=f32)
    x1 = _layer_norm(pre1, ln1g_ref[...], ln1b_ref[...])
    x1b = x1.astype(bf16)
    xb = x.astype(bf16)

    def proj(lo, hi):
        return dot(xb, win_ref[:, lo:hi]) + bin_ref[:, lo:hi]

    u = proj(0, SSM_WIDTH)
    h = proj(SSM_WIDTH, SSM_WIDTH + CONV_WIDTH)
    ub = u.astype(bf16)
    u2 = []
    for j in range(N_BLOCKS):
        cs = slice(j * BLOCK_CH, (j + 1) * BLOCK_CH)
        u2.append(jnp.concatenate([ub[:HALF, cs], ub[HALF:, cs]], axis=-1))
        bu = dot(u2[j], bmat_ref[j])
        xs_re[SUBLANES:SUBLANES + HALF, j * BLOCK_ST:(j + 1) * BLOCK_ST] = bu[:, :BLOCK_ST]
        xs_im[SUBLANES:SUBLANES + HALF, j * BLOCK_ST:(j + 1) * BLOCK_ST] = bu[:, BLOCK_ST:]

    o4 = SSM_WIDTH + 3 * CONV_WIDTH
    c_gate = proj(SSM_WIDTH + CONV_WIDTH, SSM_WIDTH + 2 * CONV_WIDTH)
    b_gate = proj(SSM_WIDTH + 2 * CONV_WIDTH, o4)
    gate_a = proj(o4, o4 + D_MODEL)
    gate_b = proj(o4 + D_MODEL, o4 + 2 * D_MODEL)

    for j in range(N_BLOCKS):
        sl = slice(j * BLOCK_ST, (j + 1) * BLOCK_ST)
        ar = jnp.broadcast_to(a2_ref[0:1, sl], (SUBLANES, BLOCK_ST))
        ai = jnp.broadcast_to(a2_ref[1:2, sl], (SUBLANES, BLOCK_ST))
        sr, si = xs_re[0:SUBLANES, sl], xs_im[0:SUBLANES, sl]
        for k in range(H_TILE):
            r = slice((k + 1) * SUBLANES, (k + 2) * SUBLANES)
            sr, si = (ar * sr - ai * si + xs_re[r, sl], ar * si + ai * sr + xs_im[r, sl])
            xs_re[r, sl] = sr
            xs_im[r, sl] = si

    gate = dot(x1b, wg_ref[...])

    y_e, y_o = [], []
    for j in range(N_BLOCKS):
        sl = slice(j * BLOCK_ST, (j + 1) * BLOCK_ST)
        y2 = (dot(xs_re[0:HALF, sl].astype(bf16), wre_ref[j])
              + dot(xs_im[0:HALF, sl].astype(bf16), wim_ref[j]) + dot(u2[j], kd_ref[j]))
        y_e.append(y2[:, :BLOCK_CH])
        y_o.append(y2[:, BLOCK_CH:])
    xs_re[0:SUBLANES, :] = xs_re[HALF:HALF + SUBLANES, :]
    xs_im[0:SUBLANES, :] = xs_im[HALF:HALF + SUBLANES, :]
    y_a = jnp.concatenate([jnp.concatenate(y_e, axis=-1), jnp.concatenate(y_o, axis=-1)], axis=0)
    y_a = y_a + dskip_ref[...] * u

    up = dot(x1b, wu_ref[...])

    v = c_gate * h
    ve_ext[SUBLANES:SUBLANES + HALF, :] = v[:HALF]
    vo_ext[SUBLANES:SUBLANES + HALF, :] = v[HALF:]
    w0, w1, w2 = convw_ref[0:1, :], convw_ref[1:2, :], convw_ref[2:3, :]
    z_e = w0 * ve_ext[0:HALF, :] + w1 * vo_ext[0:HALF, :] + w2 * v[:HALF]
    z_o = w0 * vo_ext[0:HALF, :] + w1 * v[:HALF] + w2 * v[HALF:]
    ve_ext[0:SUBLANES, :] = ve_ext[HALF:HALF + SUBLANES, :]
    vo_ext[0:SUBLANES, :] = vo_ext[HALF:HALF + SUBLANES, :]
    bz = (b_gate * jnp.concatenate([z_e, z_o], axis=0)).astype(bf16)

    g = jax.nn.gelu(y_a)
    glu = dot(g.astype(bf16), gluw_ref[...]) + glub_ref[...]
    y_b = dot(bz, wconv_ref[...])

    hid = (jax.nn.silu(gate) * up).astype(bf16)
    ffn_lo = dot(hid, wd_ref[:, 0:D_MODEL // 2])
    y_a = dot((g * jax.nn.sigmoid(glu)).astype(bf16), wssm_ref[...])
    ffn_hi = dot(hid, wd_ref[:, D_MODEL // 2:D_MODEL])
    ffn = jnp.concatenate([ffn_lo, ffn_hi], axis=-1)
    merged = jax.nn.sigmoid(gate_a) * y_a + jax.nn.sigmoid(gate_b) * y_b
    mix = dot(merged.astype(bf16), wo_ref[...])
    out = _layer_norm(ALPHA * x1 + ffn, ln2g_ref[...], ln2b_ref[...])
    return out, ALPHA * x + mix


N_MIXER_PARAMS = 14
N_FFN_PARAMS = 5
CAST_PARAMS = (0, 6, 8, 10, 11, 14, 15, 16)
CAST_ROWS = 128


def _load_weights_bf16(hbm_refs, vmem_refs, stage, sem):
    width = stage.shape[-1]
    fills = []
    for w_hbm, w_vmem in zip(hbm_refs, vmem_refs):
        rows, cols = w_hbm.shape
        assert rows % CAST_ROWS == 0 and cols <= width
        per_fill = width // cols
        row_chunks = list(range(0, rows, CAST_ROWS))
        for f0 in range(0, len(row_chunks), per_fill):
            fills.append([(w_hbm, w_vmem, r0, cols, q * cols)
                          for q, r0 in enumerate(row_chunks[f0:f0 + per_fill])])

    def copies(f):
        return [pltpu.make_async_copy(w_hbm.at[pl.ds(r0, CAST_ROWS), :],
                                      stage.at[f % 2, :, pl.ds(lane0, cols)], sem.at[f % 2])
                for w_hbm, _, r0, cols, lane0 in fills[f]]

    for cp in copies(0):
        cp.start()
    for f in range(len(fills)):
        if f + 1 < len(fills):
            for cp in copies(f + 1):
                cp.start()
        for cp in copies(f):
            cp.wait()
        for _, w_vmem, r0, cols, lane0 in fills[f]:
            w_vmem[pl.ds(r0, CAST_ROWS), :] = stage[f % 2, :, lane0:lane0 + cols].astype(bf16)


def _layer_kernel(x_hbm, *refs):
    n_params = N_MIXER_PARAMS + N_FFN_PARAMS
    params = list(refs[:n_params])
    o_hbm, xbuf, xsem, obuf, osem, pre1_ref, xs_re, xs_im, ve_ext, vo_ext, stage, wsem = \
        refs[n_params:n_params + 12]
    ssm_refs = refs[n_params + 12:n_params + 17]
    w_vmem = refs[n_params + 17:]
    w_hbm = [params[p] for p in CAST_PARAMS]
    for p, w in zip(CAST_PARAMS, w_vmem):
        params[p] = w
    mixer_refs = params[:N_MIXER_PARAMS]
    ffn_refs = params[N_MIXER_PARAMS:]
    i = pl.program_id(0)
    slot = i % 2
    ffn_tile = i - 1
    ffn_slot = (i + 1) % 2

    @pl.when(i == 0)
    def _():
        for cp in _tile_copies(x_hbm, xbuf, xsem, 0, 0, True):
            cp.start()
        zeros = jnp.zeros((SUBLANES, N_STATE), f32)
        xs_re[0:SUBLANES, :] = zeros
        xs_im[0:SUBLANES, :] = zeros
        ve_ext[0:SUBLANES, :] = jnp.zeros((SUBLANES, CONV_WIDTH), f32)
        vo_ext[0:SUBLANES, :] = jnp.zeros((SUBLANES, CONV_WIDTH), f32)
        pre1_ref[...] = jnp.zeros_like(pre1_ref)
        _ssm_prep(*mixer_refs[2:5], *ssm_refs)
        _load_weights_bf16(w_hbm, w_vmem, stage, wsem)

    @pl.when(i + 1 < N_TILES)
    def _():
        for cp in _tile_copies(x_hbm, xbuf, xsem, i + 1, 1 - slot, True):
            cp.start()

    @pl.when(i < N_TILES)
    def _():
        _tile_wait(xbuf, xsem, slot)

    @pl.when(ffn_tile >= 2)
    def _():
        _tile_wait(obuf, osem, ffn_slot)

    y, pre1 = _layer_tile(pre1_ref[...], _split_phases(xbuf[slot]), mixer_refs, ssm_refs, ffn_refs,
                          xs_re, xs_im, ve_ext, vo_ext)
    obuf[ffn_slot] = _merge_phases(y)
    pre1_ref[...] = pre1

    @pl.when(i > 0)
    def _():
        for cp in _tile_copies(o_hbm, obuf, osem, ffn_tile, ffn_slot, False):
            cp.start()

    @pl.when(i == N_TILES)
    def _():
        _tile_wait(obuf, osem, 1 - ffn_slot)
        _tile_wait(obuf, osem, ffn_slot)


def _const_spec(shape):
    nd = len(shape)
    return pl.BlockSpec(shape, lambda i, nd=nd: (0,) * nd, pipeline_mode=pl.Buffered(1))


def _ssm_prep(vec_ref, b_ref, c_ref, bmat_ref, a2_ref, wre_ref, wim_ref, kd_ref):
    lam_re, lam_im = vec_ref[0:1, :], vec_ref[1:2, :]
    dt = jnp.exp(vec_ref[2:3, :])
    mag = jnp.exp(lam_re * dt)
    a_re = mag * jnp.cos(lam_im * dt)
    a_im = mag * jnp.sin(lam_im * dt)
    den = lam_re * lam_re + lam_im * lam_im
    num_re = a_re - 1.0
    fr = (num_re * lam_re + a_im * lam_im) / den
    fi = (a_im * lam_re - num_re * lam_im) / den
    a2_re = a_re * a_re - a_im * a_im
    a2_im = 2.0 * a_re * a_im
    a2_ref[0:1, :] = a2_re
    a2_ref[1:2, :] = a2_im

    def cmul(xr, xi, yr, yi):
        return xr * yr - xi * yi, xr * yi + xi * yr

    b_re, b_im, c_re, c_im = b_ref[0], b_ref[1], c_ref[0], c_ref[1]
    bb_re, bb_im = cmul(fr, fi, b_re, b_im)
    abb_re, abb_im = cmul(a_re, a_im, bb_re, bb_im)
    ca_re, ca_im = cmul(a_re, a_im, c_re, c_im)
    ca2_re, ca2_im = cmul(a2_re, a2_im, c_re, c_im)

    row_group = lax.broadcasted_iota(jnp.int32, (BLOCK_CH, BLOCK_ST), 0) // SSM_GROUP
    col_group = lax.broadcasted_iota(jnp.int32, (BLOCK_CH, BLOCK_ST), 1) // SSM_STATE
    diag = row_group == col_group

    for j in range(N_BLOCKS):
        def bd(m):
            blk = m[:, j * BLOCK_ST:(j + 1) * BLOCK_ST]
            return jnp.where(diag, jnp.concatenate([blk] * GROUPS_PER_BLOCK, axis=0), 0.0)

        def nt(x, y):
            return lax.dot_general(x, y, (((1,), (1,)), ((), ())), precision=lax.Precision.HIGHEST,
                                   preferred_element_type=f32)

        bbr, bbi, abr, abi = bd(bb_re), bd(bb_im), bd(abb_re), bd(abb_im)
        cr, ci = bd(c_re), bd(c_im)
        bmat_ref[j, 0:BLOCK_CH, 0:BLOCK_ST] = abr.astype(bf16)
        bmat_ref[j, 0:BLOCK_CH, BLOCK_ST:2 * BLOCK_ST] = abi.astype(bf16)
        bmat_ref[j, BLOCK_CH:2 * BLOCK_CH, 0:BLOCK_ST] = bbr.astype(bf16)
        bmat_ref[j, BLOCK_CH:2 * BLOCK_CH, BLOCK_ST:2 * BLOCK_ST] = bbi.astype(bf16)
        wre_ref[j, :, 0:BLOCK_CH] = bd(ca_re).T.astype(bf16)
        wre_ref[j, :, BLOCK_CH:2 * BLOCK_CH] = bd(ca2_re).T.astype(bf16)
        wim_ref[j, :, 0:BLOCK_CH] = (-bd(ca_im)).T.astype(bf16)
        wim_ref[j, :, BLOCK_CH:2 * BLOCK_CH] = (-bd(ca2_im)).T.astype(bf16)
        k0 = nt(bbr, cr) - nt(bbi, ci)
        k1 = nt(abr, cr) - nt(abi, ci)
        kd_ref[j, 0:BLOCK_CH, 0:BLOCK_CH] = k0.astype(bf16)
        kd_ref[j, 0:BLOCK_CH, BLOCK_CH:2 * BLOCK_CH] = k1.astype(bf16)
        kd_ref[j, BLOCK_CH:2 * BLOCK_CH, 0:BLOCK_CH] = jnp.zeros((BLOCK_CH, BLOCK_CH), bf16)
        kd_ref[j, BLOCK_CH:2 * BLOCK_CH, BLOCK_CH:2 * BLOCK_CH] = k0.astype(bf16)


def _ssm_inputs(lam_re, lam_im, log_dt, b_re, b_im, c_re, c_im):
    ldt = jnp.broadcast_to(log_dt[:, None], (SSM_GROUPS, SSM_STATE))
    vec = jnp.stack([lam_re, lam_im, ldt]).astype(f32).reshape(3, N_STATE)
    b_t = jnp.transpose(jnp.stack([b_re, b_im]).astype(f32), (0, 3, 1, 2))
    c_t = jnp.transpose(jnp.stack([c_re, c_im]).astype(f32), (0, 2, 1, 3))
    return vec, b_t.reshape(2, SSM_GROUP, N_STATE), c_t.reshape(2, SSM_GROUP, N_STATE)


def _layer(x, w_in, b_in, lam_re, lam_im, log_dt, b_re, b_im, c_re, c_im, d_skip, glu_w, glu_b,
           w_ssm_out, conv_w, w_conv_out, w_o, ln1_g, ln1_b, w_gate, w_up, w_down, ln2_g, ln2_b):
    row = lambda v: v.reshape(1, -1).astype(f32)
    w32 = lambda w: w.astype(f32)
    mixer_in = (w32(w_in), row(b_in),
                *_ssm_inputs(lam_re, lam_im, log_dt, b_re, b_im, c_re, c_im), row(d_skip),
                w32(glu_w), row(glu_b), w32(w_ssm_out), conv_w.astype(f32),
                w32(w_conv_out), w32(w_o), row(ln1_g), row(ln1_b))
    ffn_in = (w32(w_gate), w32(w_up), w32(w_down), row(ln2_g), row(ln2_b))
    assert len(mixer_in) == N_MIXER_PARAMS and len(ffn_in) == N_FFN_PARAMS
    params = mixer_in + ffn_in
    param_specs = [pl.BlockSpec(memory_space=pl.ANY) if p in CAST_PARAMS else _const_spec(a.shape)
                   for p, a in enumerate(params)]
    stage_cols = max(params[p].shape[1] for p in CAST_PARAMS)
    return pl.pallas_call(
        _layer_kernel,
        grid=(N_TILES + 1,),
        in_specs=[pl.BlockSpec(memory_space=pl.ANY)] + param_specs,
        out_specs=pl.BlockSpec(memory_space=pl.ANY),
        out_shape=jax.ShapeDtypeStruct((BATCH, SEQ, D_MODEL), f32),
        scratch_shapes=[pltpu.VMEM((2, T_TILE, BATCH, D_MODEL), f32),
                        pltpu.SemaphoreType.DMA((2,)),
                        pltpu.VMEM((2, T_TILE, BATCH, D_MODEL), f32),
                        pltpu.SemaphoreType.DMA((2,)),
                        pltpu.VMEM((R_TILE, D_MODEL), f32),
                        pltpu.VMEM((HALF + SUBLANES, N_STATE), f32),
                        pltpu.VMEM((HALF + SUBLANES, N_STATE), f32),
                        pltpu.VMEM((HALF + SUBLANES, CONV_WIDTH), f32),
                        pltpu.VMEM((HALF + SUBLANES, CONV_WIDTH), f32),
                        pltpu.VMEM((2, CAST_ROWS, stage_cols), f32),
                        pltpu.SemaphoreType.DMA((2,)),
                        pltpu.VMEM((N_BLOCKS, 2 * BLOCK_CH, 2 * BLOCK_ST), bf16),
                        pltpu.VMEM((2, N_STATE), f32),
                        pltpu.VMEM((N_BLOCKS, BLOCK_ST, 2 * BLOCK_CH), bf16),
                        pltpu.VMEM((N_BLOCKS, BLOCK_ST, 2 * BLOCK_CH), bf16),
                        pltpu.VMEM((N_BLOCKS, 2 * BLOCK_CH, 2 * BLOCK_CH), bf16)]
                       + [pltpu.VMEM(params[p].shape, bf16) for p in CAST_PARAMS],
        compiler_params=pltpu.CompilerParams(dimension_semantics=("arbitrary",),
                                             vmem_limit_bytes=VMEM_LIMIT_BYTES),
        name="layer",
    )(x, *params)


def kernel(x, w_in, b_in, ssm_lambda_re, ssm_lambda_im, ssm_log_dt, ssm_b_re, ssm_b_im, ssm_c_re, ssm_c_im, ssm_d, glu_w, glu_b, w_ssm_out, conv_w, w_conv_out, w_o, ln1_g, ln1_b, w_gate, w_up, w_down, ln2_g, ln2_b):
    assert x.shape == (BATCH, SEQ, D_MODEL) and BATCH == SUBLANES
    for l in range(w_in.shape[0]):
        x = _layer(x, w_in[l], b_in[l], ssm_lambda_re[l], ssm_lambda_im[l], ssm_log_dt[l],
                   ssm_b_re[l], ssm_b_im[l], ssm_c_re[l], ssm_c_im[l], ssm_d[l], glu_w[l],
                   glu_b[l], w_ssm_out[l], conv_w[l], w_conv_out[l], w_o[l], ln1_g[l], ln1_b[l],
                   w_gate[l], w_up[l], w_down[l], ln2_g[l], ln2_b[l])
    return x
```

```python
import jax
import jax.numpy as jnp
from jax import lax
from jax.experimental import pallas as pl
from jax.experimental.pallas import tpu as pltpu

D_MODEL = 1024
BATCH = 8
SEQ = 4096
SSM_WIDTH = D_MODEL // 2
SSM_GROUP = 16
SSM_GROUPS = SSM_WIDTH // SSM_GROUP
SSM_STATE = 64
CONV_WIDTH = D_MODEL // 2
CONV_K = 3
FFN_HIDDEN = 2816
IN_COLS = SSM_WIDTH + 3 * CONV_WIDTH + 2 * D_MODEL
DEPTH = 1
ALPHA = (2.0 * DEPTH) ** 0.25
LN_EPS = 1e-5

SUBLANES = 8
VMEM_LIMIT_BYTES = 60 * 1024 * 1024

GROUPS_PER_BLOCK = 8
N_BLOCKS = SSM_GROUPS // GROUPS_PER_BLOCK
BLOCK_CH = GROUPS_PER_BLOCK * SSM_GROUP
BLOCK_ST = GROUPS_PER_BLOCK * SSM_STATE
N_STATE = SSM_GROUPS * SSM_STATE

PHASES = 2
T_TILE = 32
R_TILE = T_TILE * BATCH
H_TILE = T_TILE // PHASES
HALF = H_TILE * SUBLANES
N_TILES = SEQ // T_TILE
assert N_TILES >= 2 and SEQ % T_TILE == 0 and T_TILE % PHASES == 0

f32 = jnp.float32
bf16 = jnp.bfloat16


def _layer_norm(y, g, b):
    mu = jnp.mean(y, axis=-1, keepdims=True)
    var = jnp.mean(jnp.square(y - mu), axis=-1, keepdims=True)
    return (y - mu) * lax.rsqrt(var + LN_EPS) * g + b


def _tile_copies(hbm, buf, sem, tile, slot, to_vmem):
    copies = []
    for b in range(BATCH):
        h = hbm.at[b, pl.ds(tile * T_TILE, T_TILE), :]
        v = buf.at[slot, :, b, :]
        src, dst = (h, v) if to_vmem else (v, h)
        copies.append(pltpu.make_async_copy(src, dst, sem.at[slot]))
    return copies


def _tile_wait(buf, sem, slot):
    pltpu.make_async_copy(buf.at[slot], buf.at[slot], sem.at[slot]).wait()


def _split_phases(x3):
    x4 = x3.reshape(H_TILE, PHASES, SUBLANES, x3.shape[-1])
    return jnp.concatenate([x4[:, ph].reshape(HALF, x3.shape[-1]) for ph in range(PHASES)], axis=0)


def _merge_phases(y):
    parts = [y[ph * HALF:(ph + 1) * HALF].reshape(H_TILE, SUBLANES, y.shape[-1])
             for ph in range(PHASES)]
    return jnp.stack(parts, axis=1).reshape(T_TILE, SUBLANES, y.shape[-1])


def _layer_tile(pre1, x, mixer_refs, ssm_refs, ffn_refs, xs_re, xs_im, ve_ext, vo_ext):
    (win_ref, bin_ref, dskip_ref, gluw_ref, glub_ref, wssm_ref, convw_ref, wconv_ref, wo_ref,
     ln1g_ref, ln1b_ref) = mixer_refs[:2] + mixer_refs[5:]
    bmat_ref, a2_ref, wre_ref, wim_ref, kd_ref = ssm_refs
    wg_ref, wu_ref, wd_ref, ln2g_ref, ln2b_ref = ffn_refs
    ffn_on, mixer_on = pre1 is not None, x is not None
    dot = lambda a, w: jnp.dot(a, w, preferred_element_type=f32)
    if ffn_on:
        x1 = _layer_norm(pre1, ln1g_ref[...], ln1b_ref[...])
        x1b = x1.astype(bf16)
    if mixer_on:
        xb = x.astype(bf16)

        def proj(lo, hi):
            return dot(xb, win_ref[:, lo:hi]) + bin_ref[:, lo:hi]

        u = proj(0, SSM_WIDTH)
        h = proj(SSM_WIDTH, SSM_WIDTH + CONV_WIDTH)
        ub = u.astype(bf16)
        u2 = []
        for j in range(N_BLOCKS):
            cs = slice(j * BLOCK_CH, (j + 1) * BLOCK_CH)
            u2.append(jnp.concatenate([ub[:HALF, cs], ub[HALF:, cs]], axis=-1))
            bu = dot(u2[j], bmat_ref[j])
            xs_re[SUBLANES:SUBLANES + HALF, j * BLOCK_ST:(j + 1) * BLOCK_ST] = bu[:, :BLOCK_ST]
            xs_im[SUBLANES:SUBLANES + HALF, j * BLOCK_ST:(j + 1) * BLOCK_ST] = bu[:, BLOCK_ST:]

        o4 = SSM_WIDTH + 3 * CONV_WIDTH
        c_gate = proj(SSM_WIDTH + CONV_WIDTH, SSM_WIDTH + 2 * CONV_WIDTH)
        b_gate = proj(SSM_WIDTH + 2 * CONV_WIDTH, o4)
        gate_a = proj(o4, o4 + D_MODEL)
        gate_b = proj(o4 + D_MODEL, o4 + 2 * D_MODEL)

        for j in range(N_BLOCKS):
            sl = slice(j * BLOCK_ST, (j + 1) * BLOCK_ST)
            ar = jnp.broadcast_to(a2_ref[0:1, sl], (SUBLANES, BLOCK_ST))
            ai = jnp.broadcast_to(a2_ref[1:2, sl], (SUBLANES, BLOCK_ST))
            sr, si = xs_re[0:SUBLANES, sl], xs_im[0:SUBLANES, sl]
            for k in range(H_TILE):
                r = slice((k + 1) * SUBLANES, (k + 2) * SUBLANES)
                sr, si = (ar * sr - ai * si + xs_re[r, sl], ar * si + ai * sr + xs_im[r, sl])
                xs_re[r, sl] = sr
                xs_im[r, sl] = si

    if ffn_on:
        gate = dot(x1b, wg_ref[...])

    if mixer_on:
        y_e, y_o = [], []
        for j in range(N_BLOCKS):
            sl = slice(j * BLOCK_ST, (j + 1) * BLOCK_ST)
            y2 = (dot(xs_re[0:HALF, sl].astype(bf16), wre_ref[j])
                  + dot(xs_im[0:HALF, sl].astype(bf16), wim_ref[j]) + dot(u2[j], kd_ref[j]))
            y_e.append(y2[:, :BLOCK_CH])
            y_o.append(y2[:, BLOCK_CH:])
        xs_re[0:SUBLANES, :] = xs_re[HALF:HALF + SUBLANES, :]
        xs_im[0:SUBLANES, :] = xs_im[HALF:HALF + SUBLANES, :]
        y_a = jnp.concatenate([jnp.concatenate(y_e, axis=-1), jnp.concatenate(y_o, axis=-1)], axis=0)
        y_a = y_a + dskip_ref[...] * u

    if ffn_on:
        up = dot(x1b, wu_ref[...])

    if mixer_on:
        v = c_gate * h
        ve_ext[SUBLANES:SUBLANES + HALF, :] = v[:HALF]
        vo_ext[SUBLANES:SUBLANES + HALF, :] = v[HALF:]
        w0, w1, w2 = convw_ref[0:1, :], convw_ref[1:2, :], convw_ref[2:3, :]
        z_e = w0 * ve_ext[0:HALF, :] + w1 * vo_ext[0:HALF, :] + w2 * v[:HALF]
        z_o = w0 * vo_ext[0:HALF, :] + w1 * v[:HALF] + w2 * v[HALF:]
        ve_ext[0:SUBLANES, :] = ve_ext[HALF:HALF + SUBLANES, :]
        vo_ext[0:SUBLANES, :] = vo_ext[HALF:HALF + SUBLANES, :]
        bz = (b_gate * jnp.concatenate([z_e, z_o], axis=0)).astype(bf16)

        g = jax.nn.gelu(y_a)
        glu = dot(g.astype(bf16), gluw_ref[...]) + glub_ref[...]
        y_b = dot(bz, wconv_ref[...])

    if ffn_on:
        hid = (jax.nn.silu(gate) * up).astype(bf16)
        ffn_lo = dot(hid, wd_ref[:, 0:D_MODEL // 2])
    if mixer_on:
        y_a = dot((g * jax.nn.sigmoid(glu)).astype(bf16), wssm_ref[...])
    if ffn_on:
        ffn_hi = dot(hid, wd_ref[:, D_MODEL // 2:D_MODEL])
        ffn = jnp.concatenate([ffn_lo, ffn_hi], axis=-1)
    out = pre1_next = None
    if mixer_on:
        merged = jax.nn.sigmoid(gate_a) * y_a + jax.nn.sigmoid(gate_b) * y_b
        pre1_next = ALPHA * x + dot(merged.astype(bf16), wo_ref[...])
    if ffn_on:
        out = _layer_norm(ALPHA * x1 + ffn, ln2g_ref[...], ln2b_ref[...])
    return out, pre1_next


N_MIXER_PARAMS = 14
N_FFN_PARAMS = 5
CAST_PARAMS = (0, 6, 8, 10, 11, 14, 15, 16)
CAST_ROWS = 128


def _load_weights_bf16(hbm_refs, vmem_refs, stage, sem):
    width = stage.shape[-1]
    fills = []
    for w_hbm, w_vmem in zip(hbm_refs, vmem_refs):
        rows, cols = w_hbm.shape
        assert rows % CAST_ROWS == 0 and cols <= width
        per_fill = width // cols
        row_chunks = list(range(0, rows, CAST_ROWS))
        for f0 in range(0, len(row_chunks), per_fill):
            fills.append([(w_hbm, w_vmem, r0, cols, q * cols)
                          for q, r0 in enumerate(row_chunks[f0:f0 + per_fill])])

    def copies(f):
        return [pltpu.make_async_copy(w_hbm.at[pl.ds(r0, CAST_ROWS), :],
                                      stage.at[f % 2, :, pl.ds(lane0, cols)], sem.at[f % 2])
                for w_hbm, _, r0, cols, lane0 in fills[f]]

    for cp in copies(0):
        cp.start()
    for f in range(len(fills)):
        if f + 1 < len(fills):
            for cp in copies(f + 1):
                cp.start()
        for cp in copies(f):
            cp.wait()
        for _, w_vmem, r0, cols, lane0 in fills[f]:
            w_vmem[pl.ds(r0, CAST_ROWS), :] = stage[f % 2, :, lane0:lane0 + cols].astype(bf16)


def _layer_kernel(x_hbm, *refs):
    n_params = N_MIXER_PARAMS + N_FFN_PARAMS
    params = list(refs[:n_params])
    o_hbm, xbuf, xsem, obuf, osem, pre1_ref, xs_re, xs_im, ve_ext, vo_ext, stage, wsem = \
        refs[n_params:n_params + 12]
    ssm_refs = refs[n_params + 12:n_params + 17]
    w_vmem = refs[n_params + 17:]
    w_hbm = [params[p] for p in CAST_PARAMS]
    for p, w in zip(CAST_PARAMS, w_vmem):
        params[p] = w
    mixer_refs = params[:N_MIXER_PARAMS]
    ffn_refs = params[N_MIXER_PARAMS:]
    i = pl.program_id(0)
    slot = i % 2
    ffn_tile = i - 1
    ffn_slot = (i + 1) % 2

    @pl.when(i == 0)
    def _():
        for cp in _tile_copies(x_hbm, xbuf, xsem, 0, 0, True):
            cp.start()
        zeros = jnp.zeros((SUBLANES, N_STATE), f32)
        xs_re[0:SUBLANES, :] = zeros
        xs_im[0:SUBLANES, :] = zeros
        ve_ext[0:SUBLANES, :] = jnp.zeros((SUBLANES, CONV_WIDTH), f32)
        vo_ext[0:SUBLANES, :] = jnp.zeros((SUBLANES, CONV_WIDTH), f32)
        _ssm_prep(*mixer_refs[2:5], *ssm_refs)
        _load_weights_bf16(w_hbm, w_vmem, stage, wsem)

    @pl.when(i + 1 < N_TILES)
    def _():
        for cp in _tile_copies(x_hbm, xbuf, xsem, i + 1, 1 - slot, True):
            cp.start()

    @pl.when(i < N_TILES)
    def _():
        _tile_wait(xbuf, xsem, slot)

    @pl.when(ffn_tile >= 2)
    def _():
        _tile_wait(obuf, osem, ffn_slot)

    stage_refs = (mixer_refs, ssm_refs, ffn_refs, xs_re, xs_im, ve_ext, vo_ext)

    @pl.when(i == 0)
    def _():
        _, pre1 = _layer_tile(None, _split_phases(xbuf[slot]), *stage_refs)
        pre1_ref[...] = pre1

    @pl.when((i > 0) & (i < N_TILES))
    def _():
        y, pre1 = _layer_tile(pre1_ref[...], _split_phases(xbuf[slot]), *stage_refs)
        obuf[ffn_slot] = _merge_phases(y)
        pre1_ref[...] = pre1

    @pl.when(i == N_TILES)
    def _():
        y, _ = _layer_tile(pre1_ref[...], None, *stage_refs)
        obuf[ffn_slot] = _merge_phases(y)

    @pl.when(i > 0)
    def _():
        for cp in _tile_copies(o_hbm, obuf, osem, ffn_tile, ffn_slot, False):
            cp.start()

    @pl.when(i == N_TILES)
    def _():
        _tile_wait(obuf, osem, 1 - ffn_slot)
        _tile_wait(obuf, osem, ffn_slot)


def _const_spec(shape):
    nd = len(shape)
    return pl.BlockSpec(shape, lambda i, nd=nd: (0,) * nd, pipeline_mode=pl.Buffered(1))


def _ssm_prep(vec_ref, b_ref, c_ref, bmat_ref, a2_ref, wre_ref, wim_ref, kd_ref):
    lam_re, lam_im = vec_ref[0:1, :], vec_ref[1:2, :]
    dt = jnp.exp(vec_ref[2:3, :])
    mag = jnp.exp(lam_re * dt)
    a_re = mag * jnp.cos(lam_im * dt)
    a_im = mag * jnp.sin(lam_im * dt)
    den = lam_re * lam_re + lam_im * lam_im
    num_re = a_re - 1.0
    fr = (num_re * lam_re + a_im * lam_im) / den
    fi = (a_im * lam_re - num_re * lam_im) / den
    a2_re = a_re * a_re - a_im * a_im
    a2_im = 2.0 * a_re * a_im
    a2_ref[0:1, :] = a2_re
    a2_ref[1:2, :] = a2_im

    def cmul(xr, xi, yr, yi):
        return xr * yr - xi * yi, xr * yi + xi * yr

    b_re, b_im, c_re, c_im = b_ref[0], b_ref[1], c_ref[0], c_ref[1]
    bb_re, bb_im = cmul(fr, fi, b_re, b_im)
    abb_re, abb_im = cmul(a_re, a_im, bb_re, bb_im)
    ca_re, ca_im = cmul(a_re, a_im, c_re, c_im)
    ca2_re, ca2_im = cmul(a2_re, a2_im, c_re, c_im)

    row_group = lax.broadcasted_iota(jnp.int32, (BLOCK_CH, BLOCK_ST), 0) // SSM_GROUP
    col_group = lax.broadcasted_iota(jnp.int32, (BLOCK_CH, BLOCK_ST), 1) // SSM_STATE
    diag = row_group == col_group

    for j in range(N_BLOCKS):
        def bd(m):
            blk = m[:, j * BLOCK_ST:(j + 1) * BLOCK_ST]
            return jnp.where(diag, jnp.concatenate([blk] * GROUPS_PER_BLOCK, axis=0), 0.0)

        def nt(x, y):
            return lax.dot_general(x, y, (((1,), (1,)), ((), ())), precision=lax.Precision.HIGHEST,
                                   preferred_element_type=f32)

        bbr, bbi, abr, abi = bd(bb_re), bd(bb_im), bd(abb_re), bd(abb_im)
        cr, ci = bd(c_re), bd(c_im)
        bmat_ref[j, 0:BLOCK_CH, 0:BLOCK_ST] = abr.astype(bf16)
        bmat_ref[j, 0:BLOCK_CH, BLOCK_ST:2 * BLOCK_ST] = abi.astype(bf16)
        bmat_ref[j, BLOCK_CH:2 * BLOCK_CH, 0:BLOCK_ST] = bbr.astype(bf16)
        bmat_ref[j, BLOCK_CH:2 * BLOCK_CH, BLOCK_ST:2 * BLOCK_ST] = bbi.astype(bf16)
        wre_ref[j, :, 0:BLOCK_CH] = bd(ca_re).T.astype(bf16)
        wre_ref[j, :, BLOCK_CH:2 * BLOCK_CH] = bd(ca2_re).T.astype(bf16)
        wim_ref[j, :, 0:BLOCK_CH] = (-bd(ca_im)).T.astype(bf16)
        wim_ref[j, :, BLOCK_CH:2 * BLOCK_CH] = (-bd(ca2_im)).T.astype(bf16)
        k0 = nt(bbr, cr) - nt(bbi, ci)
        k1 = nt(abr, cr) - nt(abi, ci)
        kd_ref[j, 0:BLOCK_CH, 0:BLOCK_CH] = k0.astype(bf16)
        kd_ref[j, 0:BLOCK_CH, BLOCK_CH:2 * BLOCK_CH] = k1.astype(bf16)
        kd_ref[j, BLOCK_CH:2 * BLOCK_CH, 0:BLOCK_CH] = jnp.zeros((BLOCK_CH, BLOCK_CH), bf16)
        kd_ref[j, BLOCK_CH:2 * BLOCK_CH, BLOCK_CH:2 * BLOCK_CH] = k0.astype(bf16)


def _ssm_inputs(lam_re, lam_im, log_dt, b_re, b_im, c_re, c_im):
    ldt = jnp.broadcast_to(log_dt[:, None], (SSM_GROUPS, SSM_STATE))
    vec = jnp.stack([lam_re, lam_im, ldt]).astype(f32).reshape(3, N_STATE)
    b_t = jnp.transpose(jnp.stack([b_re, b_im]).astype(f32), (0, 3, 1, 2))
    c_t = jnp.transpose(jnp.stack([c_re, c_im]).astype(f32), (0, 2, 1, 3))
    return vec, b_t.reshape(2, SSM_GROUP, N_STATE), c_t.reshape(2, SSM_GROUP, N_STATE)


def _layer(x, w_in, b_in, lam_re, lam_im, log_dt, b_re, b_im, c_re, c_im, d_skip, glu_w, glu_b,
           w_ssm_out, conv_w, w_conv_out, w_o, ln1_g, ln1_b, w_gate, w_up, w_down, ln2_g, ln2_b):
    row = lambda v: v.reshape(1, -1).astype(f32)
    w32 = lambda w: w.astype(f32)
    mixer_in = (w32(w_in), row(b_in),
                *_ssm_inputs(lam_re, lam_im, log_dt, b_re, b_im, c_re, c_im), row(d_skip),
                w32(glu_w), row(glu_b), w32(w_ssm_out), conv_w.astype(f32),
                w32(w_conv_out), w32(w_o), row(ln1_g), row(ln1_b))
    ffn_in = (w32(w_gate), w32(w_up), w32(w_down), row(ln2_g), row(ln2_b))
    assert len(mixer_in) == N_MIXER_PARAMS and len(ffn_in) == N_FFN_PARAMS
    params = mixer_in + ffn_in
    param_specs = [pl.BlockSpec(memory_space=pl.ANY) if p in CAST_PARAMS else _const_spec(a.shape)
                   for p, a in enumerate(params)]
    stage_cols = max(params[p].shape[1] for p in CAST_PARAMS)
    return pl.pallas_call(
        _layer_kernel,
        grid=(N_TILES + 1,),
        in_specs=[pl.BlockSpec(memory_space=pl.ANY)] + param_specs,
        out_specs=pl.BlockSpec(memory_space=pl.ANY),
        out_shape=jax.ShapeDtypeStruct((BATCH, SEQ, D_MODEL), f32),
        scratch_shapes=[pltpu.VMEM((2, T_TILE, BATCH, D_MODEL), f32),
                        pltpu.SemaphoreType.DMA((2,)),
                        pltpu.VMEM((2, T_TILE, BATCH, D_MODEL), f32),
                        pltpu.SemaphoreType.DMA((2,)),
                        pltpu.VMEM((R_TILE, D_MODEL), f32),
                        pltpu.VMEM((HALF + SUBLANES, N_STATE), f32),
                        pltpu.VMEM((HALF + SUBLANES, N_STATE), f32),
                        pltpu.VMEM((HALF + SUBLANES, CONV_WIDTH), f32),
                        pltpu.VMEM((HALF + SUBLANES, CONV_WIDTH), f32),
                        pltpu.VMEM((2, CAST_ROWS, stage_cols), f32),
                        pltpu.SemaphoreType.DMA((2,)),
                        pltpu.VMEM((N_BLOCKS, 2 * BLOCK_CH, 2 * BLOCK_ST), bf16),
                        pltpu.VMEM((2, N_STATE), f32),
                        pltpu.VMEM((N_BLOCKS, BLOCK_ST, 2 * BLOCK_CH), bf16),
                        pltpu.VMEM((N_BLOCKS, BLOCK_ST, 2 * BLOCK_CH), bf16),
                        pltpu.VMEM((N_BLOCKS, 2 * BLOCK_CH, 2 * BLOCK_CH), bf16)]
                       + [pltpu.VMEM(params[p].shape, bf16) for p in CAST_PARAMS],
        compiler_params=pltpu.CompilerParams(dimension_semantics=("arbitrary",),
                                             vmem_limit_bytes=VMEM_LIMIT_BYTES),
        name="layer",
    )(x, *params)


def kernel(x, w_in, b_in, ssm_lambda_re, ssm_lambda_im, ssm_log_dt, ssm_b_re, ssm_b_im, ssm_c_re, ssm_c_im, ssm_d, glu_w, glu_b, w_ssm_out, conv_w, w_conv_out, w_o, ln1_g, ln1_b, w_gate, w_up, w_down, ln2_g, ln2_b):
    assert x.shape == (BATCH, SEQ, D_MODEL) and BATCH == SUBLANES
    for l in range(w_in.shape[0]):
        x = _layer(x, w_in[l], b_in[l], ssm_lambda_re[l], ssm_lambda_im[l], ssm_log_dt[l],
                   ssm_b_re[l], ssm_b_im[l], ssm_c_re[l], ssm_c_im[l], ssm_d[l], glu_w[l],
                   glu_b[l], w_ssm_out[l], conv_w[l], w_conv_out[l], w_o[l], ln1_g[l], ln1_b[l],
                   w_gate[l], w_up[l], w_down[l], ln2_g[l], ln2_b[l])
    return x
```

```python
import jax
import jax.numpy as jnp
from jax import lax
from jax.experimental import pallas as pl
from jax.experimental.pallas import tpu as pltpu

D_MODEL = 1024
BATCH = 8
SEQ = 4096
SSM_WIDTH = D_MODEL // 2
SSM_GROUP = 16
SSM_GROUPS = SSM_WIDTH // SSM_GROUP
SSM_STATE = 64
CONV_WIDTH = D_MODEL // 2
CONV_K = 3
FFN_HIDDEN = 2816
IN_COLS = SSM_WIDTH + 3 * CONV_WIDTH + 2 * D_MODEL
DEPTH = 1
ALPHA = (2.0 * DEPTH) ** 0.25
LN_EPS = 1e-5

SUBLANES = 8
VMEM_LIMIT_BYTES = 60 * 1024 * 1024

GROUPS_PER_BLOCK = 8
N_BLOCKS = SSM_GROUPS // GROUPS_PER_BLOCK
BLOCK_CH = GROUPS_PER_BLOCK * SSM_GROUP
BLOCK_ST = GROUPS_PER_BLOCK * SSM_STATE
N_STATE = SSM_GROUPS * SSM_STATE

PHASES = 2
T_TILE = 32
R_TILE = T_TILE * BATCH
H_TILE = T_TILE // PHASES
HALF = H_TILE * SUBLANES
STEP_TILES = 2
G_STEPS = STEP_TILES * T_TILE
N_GROUPS = SEQ // G_STEPS
assert N_GROUPS >= 2 and SEQ % G_STEPS == 0 and T_TILE % PHASES == 0

f32 = jnp.float32
bf16 = jnp.bfloat16


def _layer_norm(y, g, b):
    mu = jnp.mean(y, axis=-1, keepdims=True)
    var = jnp.mean(jnp.square(y - mu), axis=-1, keepdims=True)
    return (y - mu) * lax.rsqrt(var + LN_EPS) * g + b


def _tile_copies(hbm, buf, sem, group, slot, to_vmem):
    copies = []
    for b in range(BATCH):
        h = hbm.at[b, pl.ds(group * G_STEPS, G_STEPS), :]
        v = buf.at[slot, :, b, :]
        src, dst = (h, v) if to_vmem else (v, h)
        copies.append(pltpu.make_async_copy(src, dst, sem.at[slot]))
    return copies


def _tile_wait(buf, sem, slot):
    pltpu.make_async_copy(buf.at[slot], buf.at[slot], sem.at[slot]).wait()


def _split_phases(x3):
    x4 = x3.reshape(H_TILE, PHASES, SUBLANES, x3.shape[-1])
    return jnp.concatenate([x4[:, ph].reshape(HALF, x3.shape[-1]) for ph in range(PHASES)], axis=0)


def _merge_phases(y):
    parts = [y[ph * HALF:(ph + 1) * HALF].reshape(H_TILE, SUBLANES, y.shape[-1])
             for ph in range(PHASES)]
    return jnp.stack(parts, axis=1).reshape(T_TILE, SUBLANES, y.shape[-1])


def _layer_tile(pre1, x, mixer_refs, ssm_refs, ffn_refs, xs_re, xs_im, ve_ext, vo_ext):
    (win_ref, bin_ref, dskip_ref, gluw_ref, glub_ref, wssm_ref, convw_ref, wconv_ref, wo_ref,
     ln1g_ref, ln1b_ref) = mixer_refs[:2] + mixer_refs[5:]
    bmat_ref, a2_ref, wre_ref, wim_ref, kd_ref = ssm_refs
    wg_ref, wu_ref, wd_ref, ln2g_ref, ln2b_ref = ffn_refs
    ffn_on, mixer_on = pre1 is not None, x is not None
    dot = lambda a, w: jnp.dot(a, w, preferred_element_type=f32)
    if ffn_on:
        x1 = _layer_norm(pre1, ln1g_ref[...], ln1b_ref[...])
        x1b = x1.astype(bf16)
    if mixer_on:
        xb = x.astype(bf16)

        def proj(lo, hi):
            return dot(xb, win_ref[:, lo:hi]) + bin_ref[:, lo:hi]

        u = proj(0, SSM_WIDTH)
        h = proj(SSM_WIDTH, SSM_WIDTH + CONV_WIDTH)
        ub = u.astype(bf16)
        u2 = []
        for j in range(N_BLOCKS):
            cs = slice(j * BLOCK_CH, (j + 1) * BLOCK_CH)
            u2.append(jnp.concatenate([ub[:HALF, cs], ub[HALF:, cs]], axis=-1))
            bu = dot(u2[j], bmat_ref[j])
            xs_re[SUBLANES:SUBLANES + HALF, j * BLOCK_ST:(j + 1) * BLOCK_ST] = bu[:, :BLOCK_ST]
            xs_im[SUBLANES:SUBLANES + HALF, j * BLOCK_ST:(j + 1) * BLOCK_ST] = bu[:, BLOCK_ST:]

        o4 = SSM_WIDTH + 3 * CONV_WIDTH
        c_gate = proj(SSM_WIDTH + CONV_WIDTH, SSM_WIDTH + 2 * CONV_WIDTH)
        b_gate = proj(SSM_WIDTH + 2 * CONV_WIDTH, o4)
        gate_a = proj(o4, o4 + D_MODEL)
        gate_b = proj(o4 + D_MODEL, o4 + 2 * D_MODEL)

        for j in range(N_BLOCKS):
            sl = slice(j * BLOCK_ST, (j + 1) * BLOCK_ST)
            ar = jnp.broadcast_to(a2_ref[0:1, sl], (SUBLANES, BLOCK_ST))
            ai = jnp.broadcast_to(a2_ref[1:2, sl], (SUBLANES, BLOCK_ST))
            sr, si = xs_re[0:SUBLANES, sl], xs_im[0:SUBLANES, sl]
            for k in range(H_TILE):
                r = slice((k + 1) * SUBLANES, (k + 2) * SUBLANES)
                sr, si = (ar * sr - ai * si + xs_re[r, sl], ar * si + ai * sr + xs_im[r, sl])
                xs_re[r, sl] = sr
                xs_im[r, sl] = si

    if ffn_on:
        gate = dot(x1b, wg_ref[...])

    if mixer_on:
        y_e, y_o = [], []
        for j in range(N_BLOCKS):
            sl = slice(j * BLOCK_ST, (j + 1) * BLOCK_ST)
            y2 = (dot(xs_re[0:HALF, sl].astype(bf16), wre_ref[j])
                  + dot(xs_im[0:HALF, sl].astype(bf16), wim_ref[j]) + dot(u2[j], kd_ref[j]))
            y_e.append(y2[:, :BLOCK_CH])
            y_o.append(y2[:, BLOCK_CH:])
        xs_re[0:SUBLANES, :] = xs_re[HALF:HALF + SUBLANES, :]
        xs_im[0:SUBLANES, :] = xs_im[HALF:HALF + SUBLANES, :]
        y_a = jnp.concatenate([jnp.concatenate(y_e, axis=-1), jnp.concatenate(y_o, axis=-1)], axis=0)
        y_a = y_a + dskip_ref[...] * u

    if ffn_on:
        up = dot(x1b, wu_ref[...])

    if mixer_on:
        v = c_gate * h
        ve_ext[SUBLANES:SUBLANES + HALF, :] = v[:HALF]
        vo_ext[SUBLANES:SUBLANES + HALF, :] = v[HALF:]
        w0, w1, w2 = convw_ref[0:1, :], convw_ref[1:2, :], convw_ref[2:3, :]
        z_e = w0 * ve_ext[0:HALF, :] + w1 * vo_ext[0:HALF, :] + w2 * v[:HALF]
        z_o = w0 * vo_ext[0:HALF, :] + w1 * v[:HALF] + w2 * v[HALF:]
        ve_ext[0:SUBLANES, :] = ve_ext[HALF:HALF + SUBLANES, :]
        vo_ext[0:SUBLANES, :] = vo_ext[HALF:HALF + SUBLANES, :]
        bz = (b_gate * jnp.concatenate([z_e, z_o], axis=0)).astype(bf16)

        g = jax.nn.gelu(y_a)
        glu = dot(g.astype(bf16), gluw_ref[...]) + glub_ref[...]
        y_b = dot(bz, wconv_ref[...])

    if ffn_on:
        hid = (jax.nn.silu(gate) * up).astype(bf16)
        ffn_lo = dot(hid, wd_ref[:, 0:D_MODEL // 2])
    if mixer_on:
        y_a = dot((g * jax.nn.sigmoid(glu)).astype(bf16), wssm_ref[...])
    if ffn_on:
        ffn_hi = dot(hid, wd_ref[:, D_MODEL // 2:D_MODEL])
        ffn = jnp.concatenate([ffn_lo, ffn_hi], axis=-1)
    out = pre1_next = None
    if mixer_on:
        merged = jax.nn.sigmoid(gate_a) * y_a + jax.nn.sigmoid(gate_b) * y_b
        pre1_next = ALPHA * x + dot(merged.astype(bf16), wo_ref[...])
    if ffn_on:
        out = _layer_norm(ALPHA * x1 + ffn, ln2g_ref[...], ln2b_ref[...])
    return out, pre1_next


N_MIXER_PARAMS = 14
N_FFN_PARAMS = 5
CAST_PARAMS = (0, 6, 8, 10, 11, 14, 15, 16)
CAST_ROWS = 128


def _load_weights_bf16(hbm_refs, vmem_refs, stage, sem):
    width = stage.shape[-1]
    fills = []
    for w_hbm, w_vmem in zip(hbm_refs, vmem_refs):
        rows, cols = w_hbm.shape
        assert rows % CAST_ROWS == 0 and cols <= width
        per_fill = width // cols
        row_chunks = list(range(0, rows, CAST_ROWS))
        for f0 in range(0, len(row_chunks), per_fill):
            fills.append([(w_hbm, w_vmem, r0, cols, q * cols)
                          for q, r0 in enumerate(row_chunks[f0:f0 + per_fill])])

    def copies(f):
        return [pltpu.make_async_copy(w_hbm.at[pl.ds(r0, CAST_ROWS), :],
                                      stage.at[f % 2, :, pl.ds(lane0, cols)], sem.at[f % 2])
                for w_hbm, _, r0, cols, lane0 in fills[f]]

    for cp in copies(0):
        cp.start()
    for f in range(len(fills)):
        if f + 1 < len(fills):
            for cp in copies(f + 1):
                cp.start()
        for cp in copies(f):
            cp.wait()
        for _, w_vmem, r0, cols, lane0 in fills[f]:
            w_vmem[pl.ds(r0, CAST_ROWS), :] = stage[f % 2, :, lane0:lane0 + cols].astype(bf16)


def _layer_kernel(x_hbm, *refs):
    n_params = N_MIXER_PARAMS + N_FFN_PARAMS
    params = list(refs[:n_params])
    o_hbm, xbuf, xsem, obuf, osem, pre1_ref, xs_re, xs_im, ve_ext, vo_ext, stage, wsem = \
        refs[n_params:n_params + 12]
    ssm_refs = refs[n_params + 12:n_params + 17]
    w_vmem = refs[n_params + 17:]
    w_hbm = [params[p] for p in CAST_PARAMS]
    for p, w in zip(CAST_PARAMS, w_vmem):
        params[p] = w
    mixer_refs = params[:N_MIXER_PARAMS]
    ffn_refs = params[N_MIXER_PARAMS:]
    i = pl.program_id(0)
    slot = i % 2
    ffn_group = i - 1
    ffn_slot = (i + 1) % 2

    @pl.when(i == 0)
    def _():
        for cp in _tile_copies(x_hbm, xbuf, xsem, 0, 0, True):
            cp.start()
        zeros = jnp.zeros((SUBLANES, N_STATE), f32)
        xs_re[0:SUBLANES, :] = zeros
        xs_im[0:SUBLANES, :] = zeros
        ve_ext[0:SUBLANES, :] = jnp.zeros((SUBLANES, CONV_WIDTH), f32)
        vo_ext[0:SUBLANES, :] = jnp.zeros((SUBLANES, CONV_WIDTH), f32)
        _ssm_prep(*mixer_refs[2:5], *ssm_refs)
        _load_weights_bf16(w_hbm, w_vmem, stage, wsem)

    @pl.when(i + 1 < N_GROUPS)
    def _():
        for cp in _tile_copies(x_hbm, xbuf, xsem, i + 1, 1 - slot, True):
            cp.start()

    @pl.when(i < N_GROUPS)
    def _():
        _tile_wait(xbuf, xsem, slot)

    @pl.when(ffn_group >= 2)
    def _():
        _tile_wait(obuf, osem, ffn_slot)

    stage_refs = (mixer_refs, ssm_refs, ffn_refs, xs_re, xs_im, ve_ext, vo_ext)
    rows = lambda k: pl.ds(k * T_TILE, T_TILE)

    @pl.when(i == 0)
    def _():
        for k in range(STEP_TILES):
            _, pre1 = _layer_tile(None, _split_phases(xbuf[slot, rows(k)]), *stage_refs)
            pre1_ref[k] = pre1

    @pl.when((i > 0) & (i < N_GROUPS))
    def _():
        for k in range(STEP_TILES):
            y, pre1 = _layer_tile(pre1_ref[k], _split_phases(xbuf[slot, rows(k)]), *stage_refs)
            obuf[ffn_slot, rows(k)] = _merge_phases(y)
            pre1_ref[k] = pre1

    @pl.when(i == N_GROUPS)
    def _():
        for k in range(STEP_TILES):
            y, _ = _layer_tile(pre1_ref[k], None, *stage_refs)
            obuf[ffn_slot, rows(k)] = _merge_phases(y)

    @pl.when(i > 0)
    def _():
        for cp in _tile_copies(o_hbm, obuf, osem, ffn_group, ffn_slot, False):
            cp.start()

    @pl.when(i == N_GROUPS)
    def _():
        _tile_wait(obuf, osem, 1 - ffn_slot)
        _tile_wait(obuf, osem, ffn_slot)


def _const_spec(shape):
    nd = len(shape)
    return pl.BlockSpec(shape, lambda i, nd=nd: (0,) * nd, pipeline_mode=pl.Buffered(1))


def _ssm_prep(vec_ref, b_ref, c_ref, bmat_ref, a2_ref, wre_ref, wim_ref, kd_ref):
    lam_re, lam_im = vec_ref[0:1, :], vec_ref[1:2, :]
    dt = jnp.exp(vec_ref[2:3, :])
    mag = jnp.exp(lam_re * dt)
    a_re = mag * jnp.cos(lam_im * dt)
    a_im = mag * jnp.sin(lam_im * dt)
    den = lam_re * lam_re + lam_im * lam_im
    num_re = a_re - 1.0
    fr = (num_re * lam_re + a_im * lam_im) / den
    fi = (a_im * lam_re - num_re * lam_im) / den
    a2_re = a_re * a_re - a_im * a_im
    a2_im = 2.0 * a_re * a_im
    a2_ref[0:1, :] = a2_re
    a2_ref[1:2, :] = a2_im

    def cmul(xr, xi, yr, yi):
        return xr * yr - xi * yi, xr * yi + xi * yr

    b_re, b_im, c_re, c_im = b_ref[0], b_ref[1], c_ref[0], c_ref[1]
    bb_re, bb_im = cmul(fr, fi, b_re, b_im)
    abb_re, abb_im = cmul(a_re, a_im, bb_re, bb_im)
    ca_re, ca_im = cmul(a_re, a_im, c_re, c_im)
    ca2_re, ca2_im = cmul(a2_re, a2_im, c_re, c_im)

    row_group = lax.broadcasted_iota(jnp.int32, (BLOCK_CH, BLOCK_ST), 0) // SSM_GROUP
    col_group = lax.broadcasted_iota(jnp.int32, (BLOCK_CH, BLOCK_ST), 1) // SSM_STATE
    diag = row_group == col_group

    for j in range(N_BLOCKS):
        def bd(m):
            blk = m[:, j * BLOCK_ST:(j + 1) * BLOCK_ST]
            return jnp.where(diag, jnp.concatenate([blk] * GROUPS_PER_BLOCK, axis=0), 0.0)

        def nt(x, y):
            return lax.dot_general(x, y, (((1,), (1,)), ((), ())), precision=lax.Precision.HIGHEST,
                                   preferred_element_type=f32)

        bbr, bbi, abr, abi = bd(bb_re), bd(bb_im), bd(abb_re), bd(abb_im)
        cr, ci = bd(c_re), bd(c_im)
        bmat_ref[j, 0:BLOCK_CH, 0:BLOCK_ST] = abr.astype(bf16)
        bmat_ref[j, 0:BLOCK_CH, BLOCK_ST:2 * BLOCK_ST] = abi.astype(bf16)
        bmat_ref[j, BLOCK_CH:2 * BLOCK_CH, 0:BLOCK_ST] = bbr.astype(bf16)
        bmat_ref[j, BLOCK_CH:2 * BLOCK_CH, BLOCK_ST:2 * BLOCK_ST] = bbi.astype(bf16)
        wre_ref[j, :, 0:BLOCK_CH] = bd(ca_re).T.astype(bf16)
        wre_ref[j, :, BLOCK_CH:2 * BLOCK_CH] = bd(ca2_re).T.astype(bf16)
        wim_ref[j, :, 0:BLOCK_CH] = (-bd(ca_im)).T.astype(bf16)
        wim_ref[j, :, BLOCK_CH:2 * BLOCK_CH] = (-bd(ca2_im)).T.astype(bf16)
        k0 = nt(bbr, cr) - nt(bbi, ci)
        k1 = nt(abr, cr) - nt(abi, ci)
        kd_ref[j, 0:BLOCK_CH, 0:BLOCK_CH] = k0.astype(bf16)
        kd_ref[j, 0:BLOCK_CH, BLOCK_CH:2 * BLOCK_CH] = k1.astype(bf16)
        kd_ref[j, BLOCK_CH:2 * BLOCK_CH, 0:BLOCK_CH] = jnp.zeros((BLOCK_CH, BLOCK_CH), bf16)
        kd_ref[j, BLOCK_CH:2 * BLOCK_CH, BLOCK_CH:2 * BLOCK_CH] = k0.astype(bf16)


def _ssm_inputs(lam_re, lam_im, log_dt, b_re, b_im, c_re, c_im):
    ldt = jnp.broadcast_to(log_dt[:, None], (SSM_GROUPS, SSM_STATE))
    vec = jnp.stack([lam_re, lam_im, ldt]).astype(f32).reshape(3, N_STATE)
    b_t = jnp.transpose(jnp.stack([b_re, b_im]).astype(f32), (0, 3, 1, 2))
    c_t = jnp.transpose(jnp.stack([c_re, c_im]).astype(f32), (0, 2, 1, 3))
    return vec, b_t.reshape(2, SSM_GROUP, N_STATE), c_t.reshape(2, SSM_GROUP, N_STATE)


def _layer(x, w_in, b_in, lam_re, lam_im, log_dt, b_re, b_im, c_re, c_im, d_skip, glu_w, glu_b,
           w_ssm_out, conv_w, w_conv_out, w_o, ln1_g, ln1_b, w_gate, w_up, w_down, ln2_g, ln2_b):
    row = lambda v: v.reshape(1, -1).astype(f32)
    w32 = lambda w: w.astype(f32)
    mixer_in = (w32(w_in), row(b_in),
                *_ssm_inputs(lam_re, lam_im, log_dt, b_re, b_im, c_re, c_im), row(d_skip),
                w32(glu_w), row(glu_b), w32(w_ssm_out), conv_w.astype(f32),
                w32(w_conv_out), w32(w_o), row(ln1_g), row(ln1_b))
    ffn_in = (w32(w_gate), w32(w_up), w32(w_down), row(ln2_g), row(ln2_b))
    assert len(mixer_in) == N_MIXER_PARAMS and len(ffn_in) == N_FFN_PARAMS
    params = mixer_in + ffn_in
    param_specs = [pl.BlockSpec(memory_space=pl.ANY) if p in CAST_PARAMS else _const_spec(a.shape)
                   for p, a in enumerate(params)]
    stage_cols = max(params[p].shape[1] for p in CAST_PARAMS)
    return pl.pallas_call(
        _layer_kernel,
        grid=(N_GROUPS + 1,),
        in_specs=[pl.BlockSpec(memory_space=pl.ANY)] + param_specs,
        out_specs=pl.BlockSpec(memory_space=pl.ANY),
        out_shape=jax.ShapeDtypeStruct((BATCH, SEQ, D_MODEL), f32),
        scratch_shapes=[pltpu.VMEM((2, G_STEPS, BATCH, D_MODEL), f32),
                        pltpu.SemaphoreType.DMA((2,)),
                        pltpu.VMEM((2, G_STEPS, BATCH, D_MODEL), f32),
                        pltpu.SemaphoreType.DMA((2,)),
                        pltpu.VMEM((STEP_TILES, R_TILE, D_MODEL), f32),
                        pltpu.VMEM((HALF + SUBLANES, N_STATE), f32),
                        pltpu.VMEM((HALF + SUBLANES, N_STATE), f32),
                        pltpu.VMEM((HALF + SUBLANES, CONV_WIDTH), f32),
                        pltpu.VMEM((HALF + SUBLANES, CONV_WIDTH), f32),
                        pltpu.VMEM((2, CAST_ROWS, stage_cols), f32),
                        pltpu.SemaphoreType.DMA((2,)),
                        pltpu.VMEM((N_BLOCKS, 2 * BLOCK_CH, 2 * BLOCK_ST), bf16),
                        pltpu.VMEM((2, N_STATE), f32),
                        pltpu.VMEM((N_BLOCKS, BLOCK_ST, 2 * BLOCK_CH), bf16),
                        pltpu.VMEM((N_BLOCKS, BLOCK_ST, 2 * BLOCK_CH), bf16),
                        pltpu.VMEM((N_BLOCKS, 2 * BLOCK_CH, 2 * BLOCK_CH), bf16)]
                       + [pltpu.VMEM(params[p].shape, bf16) for p in CAST_PARAMS],
        compiler_params=pltpu.CompilerParams(dimension_semantics=("arbitrary",),
                                             vmem_limit_bytes=VMEM_LIMIT_BYTES),
        name="layer",
    )(x, *params)


def kernel(x, w_in, b_in, ssm_lambda_re, ssm_lambda_im, ssm_log_dt, ssm_b_re, ssm_b_im, ssm_c_re, ssm_c_im, ssm_d, glu_w, glu_b, w_ssm_out, conv_w, w_conv_out, w_o, ln1_g, ln1_b, w_gate, w_up, w_down, ln2_g, ln2_b):
    assert x.shape == (BATCH, SEQ, D_MODEL) and BATCH == SUBLANES
    for l in range(w_in.shape[0]):
        x = _layer(x, w_in[l], b_in[l], ssm_lambda_re[l], ssm_lambda_im[l], ssm_log_dt[l],
                   ssm_b_re[l], ssm_b_im[l], ssm_c_re[l], ssm_c_im[l], ssm_d[l], glu_w[l],
                   glu_b[l], w_ssm_out[l], conv_w[l], w_conv_out[l], w_o[l], ln1_g[l], ln1_b[l],
                   w_gate[l], w_up[l], w_down[l], ln2_g[l], ln2_b[l])
    return x
```

```python
import jax
import jax.numpy as jnp
from jax import lax
from jax.experimental import pallas as pl
from jax.experimental.pallas import tpu as pltpu

D_MODEL = 1024
BATCH = 8
SEQ = 4096
SSM_WIDTH = D_MODEL // 2
SSM_GROUP = 16
SSM_GROUPS = SSM_WIDTH // SSM_GROUP
SSM_STATE = 64
CONV_WIDTH = D_MODEL // 2
CONV_K = 3
FFN_HIDDEN = 2816
IN_COLS = SSM_WIDTH + 3 * CONV_WIDTH + 2 * D_MODEL
DEPTH = 1
ALPHA = (2.0 * DEPTH) ** 0.25
LN_EPS = 1e-5

SUBLANES = 8
VMEM_LIMIT_BYTES = 60 * 1024 * 1024

GROUPS_PER_BLOCK = 8
N_BLOCKS = SSM_GROUPS // GROUPS_PER_BLOCK
BLOCK_CH = GROUPS_PER_BLOCK * SSM_GROUP
BLOCK_ST = GROUPS_PER_BLOCK * SSM_STATE
N_STATE = SSM_GROUPS * SSM_STATE

PHASES = 2
T_TILE = 32
R_TILE = T_TILE * BATCH
H_TILE = T_TILE // PHASES
HALF = H_TILE * SUBLANES
N_TILES = SEQ // T_TILE
assert N_TILES >= 2 and SEQ % T_TILE == 0 and T_TILE % PHASES == 0

f32 = jnp.float32
bf16 = jnp.bfloat16


def _layer_norm(y, g, b):
    mu = jnp.mean(y, axis=-1, keepdims=True)
    var = jnp.mean(jnp.square(y - mu), axis=-1, keepdims=True)
    return (y - mu) * lax.rsqrt(var + LN_EPS) * g + b


def _tile_copies(hbm, buf, sem, tile, slot, to_vmem):
    copies = []
    for b in range(BATCH):
        h = hbm.at[b, pl.ds(tile * T_TILE, T_TILE), :]
        v = buf.at[slot, :, b, :]
        src, dst = (h, v) if to_vmem else (v, h)
        copies.append(pltpu.make_async_copy(src, dst, sem.at[slot]))
    return copies


def _tile_wait(buf, sem, slot):
    pltpu.make_async_copy(buf.at[slot], buf.at[slot], sem.at[slot]).wait()


def _split_phases(x3):
    x4 = x3.reshape(H_TILE, PHASES, SUBLANES, x3.shape[-1])
    return jnp.concatenate([x4[:, ph].reshape(HALF, x3.shape[-1]) for ph in range(PHASES)], axis=0)


def _merge_phases(y):
    parts = [y[ph * HALF:(ph + 1) * HALF].reshape(H_TILE, SUBLANES, y.shape[-1])
             for ph in range(PHASES)]
    return jnp.stack(parts, axis=1).reshape(T_TILE, SUBLANES, y.shape[-1])


def _layer_tile(pre1, x, mixer_refs, ssm_refs, ffn_refs, xs_re, xs_im, ve_ext, vo_ext):
    win_ref, small_ref = mixer_refs[:2]
    gluw_ref, wssm_ref, wconv_ref, wo_ref = mixer_refs[5:]
    bmat_ref, a2_ref, wre_ref, wim_ref, kd_ref = ssm_refs
    wg_ref, wu_ref, wd_ref = ffn_refs
    half_row = lambda r, k: small_ref[r:r + 1, k * SMALL_COLS // 2:(k + 1) * SMALL_COLS // 2]
    full_row = lambda r: small_ref[r:r + 1, :]
    ffn_on, mixer_on = pre1 is not None, x is not None
    dot = lambda a, w: jnp.dot(a, w, preferred_element_type=f32)
    if ffn_on:
        x1 = _layer_norm(pre1, full_row(ROW_LN1_G), full_row(ROW_LN1_B))
        x1b = x1.astype(bf16)
    if mixer_on:
        xb = x.astype(bf16)

        def proj(lo, hi):
            c0 = lo % SMALL_COLS
            return dot(xb, win_ref[:, lo:hi]) + small_ref[lo // SMALL_COLS:lo // SMALL_COLS + 1,
                                                          c0:c0 + hi - lo]

        u = proj(0, SSM_WIDTH)
        h = proj(SSM_WIDTH, SSM_WIDTH + CONV_WIDTH)
        ub = u.astype(bf16)
        u2 = []
        for j in range(N_BLOCKS):
            cs = slice(j * BLOCK_CH, (j + 1) * BLOCK_CH)
            u2.append(jnp.concatenate([ub[:HALF, cs], ub[HALF:, cs]], axis=-1))
            bu = dot(u2[j], bmat_ref[j])
            xs_re[SUBLANES:SUBLANES + HALF, j * BLOCK_ST:(j + 1) * BLOCK_ST] = bu[:, :BLOCK_ST]
            xs_im[SUBLANES:SUBLANES + HALF, j * BLOCK_ST:(j + 1) * BLOCK_ST] = bu[:, BLOCK_ST:]

        o4 = SSM_WIDTH + 3 * CONV_WIDTH
        c_gate = proj(SSM_WIDTH + CONV_WIDTH, SSM_WIDTH + 2 * CONV_WIDTH)
        b_gate = proj(SSM_WIDTH + 2 * CONV_WIDTH, o4)
        gate_a = proj(o4, o4 + D_MODEL)
        gate_b = proj(o4 + D_MODEL, o4 + 2 * D_MODEL)

        for j in range(N_BLOCKS):
            sl = slice(j * BLOCK_ST, (j + 1) * BLOCK_ST)
            ar = jnp.broadcast_to(a2_ref[0:1, sl], (SUBLANES, BLOCK_ST))
            ai = jnp.broadcast_to(a2_ref[1:2, sl], (SUBLANES, BLOCK_ST))
            sr, si = xs_re[0:SUBLANES, sl], xs_im[0:SUBLANES, sl]
            for k in range(H_TILE):
                r = slice((k + 1) * SUBLANES, (k + 2) * SUBLANES)
                sr, si = (ar * sr - ai * si + xs_re[r, sl], ar * si + ai * sr + xs_im[r, sl])
                xs_re[r, sl] = sr
                xs_im[r, sl] = si

    if ffn_on:
        gate = dot(x1b, wg_ref[...])

    if mixer_on:
        y_e, y_o = [], []
        for j in range(N_BLOCKS):
            sl = slice(j * BLOCK_ST, (j + 1) * BLOCK_ST)
            y2 = (dot(xs_re[0:HALF, sl].astype(bf16), wre_ref[j])
                  + dot(xs_im[0:HALF, sl].astype(bf16), wim_ref[j]) + dot(u2[j], kd_ref[j]))
            y_e.append(y2[:, :BLOCK_CH])
            y_o.append(y2[:, BLOCK_CH:])
        xs_re[0:SUBLANES, :] = xs_re[HALF:HALF + SUBLANES, :]
        xs_im[0:SUBLANES, :] = xs_im[HALF:HALF + SUBLANES, :]
        y_a = jnp.concatenate([jnp.concatenate(y_e, axis=-1), jnp.concatenate(y_o, axis=-1)], axis=0)
        y_a = y_a + half_row(ROW_DSKIP_GLUB, 0) * u

    if ffn_on:
        up = dot(x1b, wu_ref[...])

    if mixer_on:
        v = c_gate * h
        ve_ext[SUBLANES:SUBLANES + HALF, :] = v[:HALF]
        vo_ext[SUBLANES:SUBLANES + HALF, :] = v[HALF:]
        w0, w1, w2 = half_row(ROW_CONV, 0), half_row(ROW_CONV, 1), half_row(ROW_CONV + 1, 0)
        z_e = w0 * ve_ext[0:HALF, :] + w1 * vo_ext[0:HALF, :] + w2 * v[:HALF]
        z_o = w0 * vo_ext[0:HALF, :] + w1 * v[:HALF] + w2 * v[HALF:]
        ve_ext[0:SUBLANES, :] = ve_ext[HALF:HALF + SUBLANES, :]
        vo_ext[0:SUBLANES, :] = vo_ext[HALF:HALF + SUBLANES, :]
        bz = (b_gate * jnp.concatenate([z_e, z_o], axis=0)).astype(bf16)

        g = jax.nn.gelu(y_a)
        glu = dot(g.astype(bf16), gluw_ref[...]) + half_row(ROW_DSKIP_GLUB, 1)
        y_b = dot(bz, wconv_ref[...])

    if ffn_on:
        hid = (jax.nn.silu(gate) * up).astype(bf16)
        ffn_lo = dot(hid, wd_ref[:, 0:D_MODEL // 2])
    if mixer_on:
        y_a = dot((g * jax.nn.sigmoid(glu)).astype(bf16), wssm_ref[...])
    if ffn_on:
        ffn_hi = dot(hid, wd_ref[:, D_MODEL // 2:D_MODEL])
        ffn = jnp.concatenate([ffn_lo, ffn_hi], axis=-1)
    out = pre1_next = None
    if mixer_on:
        merged = jax.nn.sigmoid(gate_a) * y_a + jax.nn.sigmoid(gate_b) * y_b
        pre1_next = ALPHA * x + dot(merged.astype(bf16), wo_ref[...])
    if ffn_on:
        out = _layer_norm(ALPHA * x1 + ffn, full_row(ROW_LN2_G), full_row(ROW_LN2_B))
    return out, pre1_next


N_MIXER_PARAMS = 9
N_FFN_PARAMS = 3
CAST_PARAMS = (0, 5, 6, 7, 8, 9, 10, 11)
SMALL_COLS = D_MODEL
ROW_DSKIP_GLUB = IN_COLS // SMALL_COLS
ROW_CONV = ROW_DSKIP_GLUB + 1
ROW_LN1_G, ROW_LN1_B, ROW_LN2_G, ROW_LN2_B = (ROW_CONV + 2 + r for r in range(4))
SMALL_ROWS = 16
CAST_ROWS = 128


def _load_weights_bf16(hbm_refs, vmem_refs, stage, sem):
    width = stage.shape[-1]
    fills = []
    for w_hbm, w_vmem in zip(hbm_refs, vmem_refs):
        rows, cols = w_hbm.shape
        assert rows % CAST_ROWS == 0 and cols <= width
        per_fill = width // cols
        row_chunks = list(range(0, rows, CAST_ROWS))
        for f0 in range(0, len(row_chunks), per_fill):
            fills.append([(w_hbm, w_vmem, r0, cols, q * cols)
                          for q, r0 in enumerate(row_chunks[f0:f0 + per_fill])])

    def copies(f):
        return [pltpu.make_async_copy(w_hbm.at[pl.ds(r0, CAST_ROWS), :],
                                      stage.at[f % 2, :, pl.ds(lane0, cols)], sem.at[f % 2])
                for w_hbm, _, r0, cols, lane0 in fills[f]]

    for cp in copies(0):
        cp.start()
    for f in range(len(fills)):
        if f + 1 < len(fills):
            for cp in copies(f + 1):
                cp.start()
        for cp in copies(f):
            cp.wait()
        for _, w_vmem, r0, cols, lane0 in fills[f]:
            w_vmem[pl.ds(r0, CAST_ROWS), :] = stage[f % 2, :, lane0:lane0 + cols].astype(bf16)


def _layer_kernel(x_hbm, *refs):
    n_params = N_MIXER_PARAMS + N_FFN_PARAMS
    params = list(refs[:n_params])
    o_hbm, xbuf, xsem, obuf, osem, pre1_ref, xs_re, xs_im, ve_ext, vo_ext, stage, wsem = \
        refs[n_params:n_params + 12]
    ssm_refs = refs[n_params + 12:n_params + 17]
    w_vmem = refs[n_params + 17:]
    w_hbm = [params[p] for p in CAST_PARAMS]
    for p, w in zip(CAST_PARAMS, w_vmem):
        params[p] = w
    mixer_refs = params[:N_MIXER_PARAMS]
    ffn_refs = params[N_MIXER_PARAMS:]
    i = pl.program_id(0)
    slot = i % 2
    ffn_tile = i - 1
    ffn_slot = (i + 1) % 2

    @pl.when(i == 0)
    def _():
        for cp in _tile_copies(x_hbm, xbuf, xsem, 0, 0, True):
            cp.start()
        zeros = jnp.zeros((SUBLANES, N_STATE), f32)
        xs_re[0:SUBLANES, :] = zeros
        xs_im[0:SUBLANES, :] = zeros
        ve_ext[0:SUBLANES, :] = jnp.zeros((SUBLANES, CONV_WIDTH), f32)
        vo_ext[0:SUBLANES, :] = jnp.zeros((SUBLANES, CONV_WIDTH), f32)
        _ssm_prep(*mixer_refs[2:5], *ssm_refs)
        _load_weights_bf16(w_hbm, w_vmem, stage, wsem)

    @pl.when(i + 1 < N_TILES)
    def _():
        for cp in _tile_copies(x_hbm, xbuf, xsem, i + 1, 1 - slot, True):
            cp.start()

    @pl.when(i < N_TILES)
    def _():
        _tile_wait(xbuf, xsem, slot)

    @pl.when(ffn_tile >= 2)
    def _():
        _tile_wait(obuf, osem, ffn_slot)

    stage_refs = (mixer_refs, ssm_refs, ffn_refs, xs_re, xs_im, ve_ext, vo_ext)

    @pl.when(i == 0)
    def _():
        _, pre1 = _layer_tile(None, _split_phases(xbuf[slot]), *stage_refs)
        pre1_ref[...] = pre1

    @pl.when((i > 0) & (i < N_TILES))
    def _():
        y, pre1 = _layer_tile(pre1_ref[...], _split_phases(xbuf[slot]), *stage_refs)
        obuf[ffn_slot] = _merge_phases(y)
        pre1_ref[...] = pre1

    @pl.when(i == N_TILES)
    def _():
        y, _ = _layer_tile(pre1_ref[...], None, *stage_refs)
        obuf[ffn_slot] = _merge_phases(y)

    @pl.when(i > 0)
    def _():
        for cp in _tile_copies(o_hbm, obuf, osem, ffn_tile, ffn_slot, False):
            cp.start()

    @pl.when(i == N_TILES)
    def _():
        _tile_wait(obuf, osem, 1 - ffn_slot)
        _tile_wait(obuf, osem, ffn_slot)


def _const_spec(shape):
    nd = len(shape)
    return pl.BlockSpec(shape, lambda i, nd=nd: (0,) * nd, pipeline_mode=pl.Buffered(1))


def _ssm_prep(vec_ref, b_ref, c_ref, bmat_ref, a2_ref, wre_ref, wim_ref, kd_ref):
    lam_re, lam_im = vec_ref[0:1, :], vec_ref[1:2, :]
    dt = jnp.exp(vec_ref[2:3, :])
    mag = jnp.exp(lam_re * dt)
    a_re = mag * jnp.cos(lam_im * dt)
    a_im = mag * jnp.sin(lam_im * dt)
    den = lam_re * lam_re + lam_im * lam_im
    num_re = a_re - 1.0
    fr = (num_re * lam_re + a_im * lam_im) / den
    fi = (a_im * lam_re - num_re * lam_im) / den
    a2_re = a_re * a_re - a_im * a_im
    a2_im = 2.0 * a_re * a_im
    a2_ref[0:1, :] = a2_re
    a2_ref[1:2, :] = a2_im

    def cmul(xr, xi, yr, yi):
        return xr * yr - xi * yi, xr * yi + xi * yr

    b_re, b_im, c_re, c_im = b_ref[0], b_ref[1], c_ref[0], c_ref[1]
    bb_re, bb_im = cmul(fr, fi, b_re, b_im)
    abb_re, abb_im = cmul(a_re, a_im, bb_re, bb_im)
    ca_re, ca_im = cmul(a_re, a_im, c_re, c_im)
    ca2_re, ca2_im = cmul(a2_re, a2_im, c_re, c_im)

    row_group = lax.broadcasted_iota(jnp.int32, (BLOCK_CH, BLOCK_ST), 0) // SSM_GROUP
    col_group = lax.broadcasted_iota(jnp.int32, (BLOCK_CH, BLOCK_ST), 1) // SSM_STATE
    diag = row_group == col_group

    for j in range(N_BLOCKS):
        def bd(m):
            blk = m[:, j * BLOCK_ST:(j + 1) * BLOCK_ST]
            return jnp.where(diag, jnp.concatenate([blk] * GROUPS_PER_BLOCK, axis=0), 0.0)

        def nt(x, y):
            return lax.dot_general(x, y, (((1,), (1,)), ((), ())), precision=lax.Precision.HIGHEST,
                                   preferred_element_type=f32)

        bbr, bbi, abr, abi = bd(bb_re), bd(bb_im), bd(abb_re), bd(abb_im)
        cr, ci = bd(c_re), bd(c_im)
        bmat_ref[j, 0:BLOCK_CH, 0:BLOCK_ST] = abr.astype(bf16)
        bmat_ref[j, 0:BLOCK_CH, BLOCK_ST:2 * BLOCK_ST] = abi.astype(bf16)
        bmat_ref[j, BLOCK_CH:2 * BLOCK_CH, 0:BLOCK_ST] = bbr.astype(bf16)
        bmat_ref[j, BLOCK_CH:2 * BLOCK_CH, BLOCK_ST:2 * BLOCK_ST] = bbi.astype(bf16)
        wre_ref[j, :, 0:BLOCK_CH] = bd(ca_re).T.astype(bf16)
        wre_ref[j, :, BLOCK_CH:2 * BLOCK_CH] = bd(ca2_re).T.astype(bf16)
        wim_ref[j, :, 0:BLOCK_CH] = (-bd(ca_im)).T.astype(bf16)
        wim_ref[j, :, BLOCK_CH:2 * BLOCK_CH] = (-bd(ca2_im)).T.astype(bf16)
        k0 = nt(bbr, cr) - nt(bbi, ci)
        k1 = nt(abr, cr) - nt(abi, ci)
        kd_ref[j, 0:BLOCK_CH, 0:BLOCK_CH] = k0.astype(bf16)
        kd_ref[j, 0:BLOCK_CH, BLOCK_CH:2 * BLOCK_CH] = k1.astype(bf16)
        kd_ref[j, BLOCK_CH:2 * BLOCK_CH, 0:BLOCK_CH] = jnp.zeros((BLOCK_CH, BLOCK_CH), bf16)
        kd_ref[j, BLOCK_CH:2 * BLOCK_CH, BLOCK_CH:2 * BLOCK_CH] = k0.astype(bf16)


def _ssm_inputs(lam_re, lam_im, log_dt, b_re, b_im, c_re, c_im):
    ldt = jnp.broadcast_to(log_dt[:, None], (SSM_GROUPS, SSM_STATE))
    vec = jnp.stack([lam_re, lam_im, ldt]).astype(f32).reshape(3, N_STATE)
    b_t = jnp.transpose(jnp.stack([b_re, b_im]).astype(f32), (0, 3, 1, 2))
    c_t = jnp.transpose(jnp.stack([c_re, c_im]).astype(f32), (0, 2, 1, 3))
    return vec, b_t.reshape(2, SSM_GROUP, N_STATE), c_t.reshape(2, SSM_GROUP, N_STATE)


def _layer(x, w_in, b_in, lam_re, lam_im, log_dt, b_re, b_im, c_re, c_im, d_skip, glu_w, glu_b,
           w_ssm_out, conv_w, w_conv_out, w_o, ln1_g, ln1_b, w_gate, w_up, w_down, ln2_g, ln2_b):
    w32 = lambda w: w.astype(f32)
    flat = lambda *vs: jnp.concatenate([v.astype(f32).reshape(-1) for v in vs])
    small = flat(b_in, d_skip, glu_b, conv_w, jnp.zeros((CONV_WIDTH,), f32), ln1_g, ln1_b, ln2_g, ln2_b)
    small = jnp.pad(small, (0, SMALL_ROWS * SMALL_COLS - small.size)).reshape(SMALL_ROWS, SMALL_COLS)
    mixer_in = (w32(w_in), small, *_ssm_inputs(lam_re, lam_im, log_dt, b_re, b_im, c_re, c_im),
                w32(glu_w), w32(w_ssm_out), w32(w_conv_out), w32(w_o))
    ffn_in = (w32(w_gate), w32(w_up), w32(w_down))
    assert len(mixer_in) == N_MIXER_PARAMS and len(ffn_in) == N_FFN_PARAMS
    params = mixer_in + ffn_in
    param_specs = [pl.BlockSpec(memory_space=pl.ANY) if p in CAST_PARAMS else _const_spec(a.shape)
                   for p, a in enumerate(params)]
    stage_cols = max(params[p].shape[1] for p in CAST_PARAMS)
    return pl.pallas_call(
        _layer_kernel,
        grid=(N_TILES + 1,),
        in_specs=[pl.BlockSpec(memory_space=pl.ANY)] + param_specs,
        out_specs=pl.BlockSpec(memory_space=pl.ANY),
        out_shape=jax.ShapeDtypeStruct((BATCH, SEQ, D_MODEL), f32),
        scratch_shapes=[pltpu.VMEM((2, T_TILE, BATCH, D_MODEL), f32),
                        pltpu.SemaphoreType.DMA((2,)),
                        pltpu.VMEM((2, T_TILE, BATCH, D_MODEL), f32),
                        pltpu.SemaphoreType.DMA((2,)),
                        pltpu.VMEM((R_TILE, D_MODEL), f32),
                        pltpu.VMEM((HALF + SUBLANES, N_STATE), f32),
                        pltpu.VMEM((HALF + SUBLANES, N_STATE), f32),
                        pltpu.VMEM((HALF + SUBLANES, CONV_WIDTH), f32),
                        pltpu.VMEM((HALF + SUBLANES, CONV_WIDTH), f32),
                        pltpu.VMEM((2, CAST_ROWS, stage_cols), f32),
                        pltpu.SemaphoreType.DMA((2,)),
                        pltpu.VMEM((N_BLOCKS, 2 * BLOCK_CH, 2 * BLOCK_ST), bf16),
                        pltpu.VMEM((2, N_STATE), f32),
                        pltpu.VMEM((N_BLOCKS, BLOCK_ST, 2 * BLOCK_CH), bf16),
                        pltpu.VMEM((N_BLOCKS, BLOCK_ST, 2 * BLOCK_CH), bf16),
                        pltpu.VMEM((N_BLOCKS, 2 * BLOCK_CH, 2 * BLOCK_CH), bf16)]
                       + [pltpu.VMEM(params[p].shape, bf16) for p in CAST_PARAMS],
        compiler_params=pltpu.CompilerParams(dimension_semantics=("arbitrary",),
                                             vmem_limit_bytes=VMEM_LIMIT_BYTES),
        name="layer",
    )(x, *params)


def kernel(x, w_in, b_in, ssm_lambda_re, ssm_lambda_im, ssm_log_dt, ssm_b_re, ssm_b_im, ssm_c_re, ssm_c_im, ssm_d, glu_w, glu_b, w_ssm_out, conv_w, w_conv_out, w_o, ln1_g, ln1_b, w_gate, w_up, w_down, ln2_g, ln2_b):
    assert x.shape == (BATCH, SEQ, D_MODEL) and BATCH == SUBLANES
    for l in range(w_in.shape[0]):
        x = _layer(x, w_in[l], b_in[l], ssm_lambda_re[l], ssm_lambda_im[l], ssm_log_dt[l],
                   ssm_b_re[l], ssm_b_im[l], ssm_c_re[l], ssm_c_im[l], ssm_d[l], glu_w[l],
                   glu_b[l], w_ssm_out[l], conv_w[l], w_conv_out[l], w_o[l], ln1_g[l], ln1_b[l],
                   w_gate[l], w_up[l], w_down[l], ln2_g[l], ln2_b[l])
    return x
```

```python
import jax
import jax.numpy as jnp
from jax import lax
from jax.experimental import pallas as pl
from jax.experimental.pallas import tpu as pltpu

D_MODEL = 1024
BATCH = 8
SEQ = 4096
SSM_WIDTH = D_MODEL // 2
SSM_GROUP = 16
SSM_GROUPS = SSM_WIDTH // SSM_GROUP
SSM_STATE = 64
CONV_WIDTH = D_MODEL // 2
CONV_K = 3
FFN_HIDDEN = 2816
IN_COLS = SSM_WIDTH + 3 * CONV_WIDTH + 2 * D_MODEL
DEPTH = 1
ALPHA = (2.0 * DEPTH) ** 0.25
LN_EPS = 1e-5

SUBLANES = 8
VMEM_LIMIT_BYTES = 60 * 1024 * 1024

GROUPS_PER_BLOCK = 8
N_BLOCKS = SSM_GROUPS // GROUPS_PER_BLOCK
BLOCK_CH = GROUPS_PER_BLOCK * SSM_GROUP
BLOCK_ST = GROUPS_PER_BLOCK * SSM_STATE
N_STATE = SSM_GROUPS * SSM_STATE

PHASES = 2
T_TILE = 32
R_TILE = T_TILE * BATCH
H_TILE = T_TILE // PHASES
HALF = H_TILE * SUBLANES
N_TILES = SEQ // T_TILE
assert N_TILES >= 2 and SEQ % T_TILE == 0 and T_TILE % PHASES == 0

f32 = jnp.float32
bf16 = jnp.bfloat16


def _layer_norm(y, g, b):
    mu = jnp.mean(y, axis=-1, keepdims=True)
    var = jnp.mean(jnp.square(y - mu), axis=-1, keepdims=True)
    return (y - mu) * lax.rsqrt(var + LN_EPS) * g + b


def _tile_copies(hbm, buf, sem, tile, slot, to_vmem):
    copies = []
    for b in range(BATCH):
        h = hbm.at[b, pl.ds(tile * T_TILE, T_TILE), :]
        v = buf.at[slot, :, b, :]
        src, dst = (h, v) if to_vmem else (v, h)
        copies.append(pltpu.make_async_copy(src, dst, sem.at[slot]))
    return copies


def _tile_wait(buf, sem, slot):
    pltpu.make_async_copy(buf.at[slot], buf.at[slot], sem.at[slot]).wait()


def _split_phases(x3):
    x4 = x3.reshape(H_TILE, PHASES, SUBLANES, x3.shape[-1])
    return jnp.concatenate([x4[:, ph].reshape(HALF, x3.shape[-1]) for ph in range(PHASES)], axis=0)


def _merge_phases(y):
    parts = [y[ph * HALF:(ph + 1) * HALF].reshape(H_TILE, SUBLANES, y.shape[-1])
             for ph in range(PHASES)]
    return jnp.stack(parts, axis=1).reshape(T_TILE, SUBLANES, y.shape[-1])


def _layer_tile(pre1, x, mixer_refs, ssm_refs, ffn_refs, xs_re, xs_im, ve_ext, vo_ext):
    win_ref, bin_ref, small_ref = mixer_refs[:3]
    gluw_ref, wssm_ref, wconv_ref, wo_ref, ln1g_ref, ln1b_ref = mixer_refs[6:]
    bmat_ref, a2_ref, wre_ref, wim_ref, kd_ref = ssm_refs
    wg_ref, wu_ref, wd_ref, ln2g_ref, ln2b_ref = ffn_refs
    small = lambda k: small_ref[:, k * CONV_WIDTH:(k + 1) * CONV_WIDTH]
    ffn_on, mixer_on = pre1 is not None, x is not None
    dot = lambda a, w: jnp.dot(a, w, preferred_element_type=f32)
    if ffn_on:
        x1 = _layer_norm(pre1, ln1g_ref[...], ln1b_ref[...])
        x1b = x1.astype(bf16)
    if mixer_on:
        xb = x.astype(bf16)

        def proj(lo, hi):
            return dot(xb, win_ref[:, lo:hi]) + bin_ref[:, lo:hi]

        u = proj(0, SSM_WIDTH)
        h = proj(SSM_WIDTH, SSM_WIDTH + CONV_WIDTH)
        ub = u.astype(bf16)
        u2 = []
        for j in range(N_BLOCKS):
            cs = slice(j * BLOCK_CH, (j + 1) * BLOCK_CH)
            u2.append(jnp.concatenate([ub[:HALF, cs], ub[HALF:, cs]], axis=-1))
            bu = dot(u2[j], bmat_ref[j])
            xs_re[SUBLANES:SUBLANES + HALF, j * BLOCK_ST:(j + 1) * BLOCK_ST] = bu[:, :BLOCK_ST]
            xs_im[SUBLANES:SUBLANES + HALF, j * BLOCK_ST:(j + 1) * BLOCK_ST] = bu[:, BLOCK_ST:]

        o4 = SSM_WIDTH + 3 * CONV_WIDTH
        c_gate = proj(SSM_WIDTH + CONV_WIDTH, SSM_WIDTH + 2 * CONV_WIDTH)
        b_gate = proj(SSM_WIDTH + 2 * CONV_WIDTH, o4)
        gate_a = proj(o4, o4 + D_MODEL)
        gate_b = proj(o4 + D_MODEL, o4 + 2 * D_MODEL)

        for j in range(N_BLOCKS):
            sl = slice(j * BLOCK_ST, (j + 1) * BLOCK_ST)
            ar = jnp.broadcast_to(a2_ref[0:1, sl], (SUBLANES, BLOCK_ST))
            ai = jnp.broadcast_to(a2_ref[1:2, sl], (SUBLANES, BLOCK_ST))
            sr, si = xs_re[0:SUBLANES, sl], xs_im[0:SUBLANES, sl]
            for k in range(H_TILE):
                r = slice((k + 1) * SUBLANES, (k + 2) * SUBLANES)
                sr, si = (ar * sr - ai * si + xs_re[r, sl], ar * si + ai * sr + xs_im[r, sl])
                xs_re[r, sl] = sr
                xs_im[r, sl] = si

    if ffn_on:
        gate = dot(x1b, wg_ref[...])

    if mixer_on:
        y_e, y_o = [], []
        for j in range(N_BLOCKS):
            sl = slice(j * BLOCK_ST, (j + 1) * BLOCK_ST)
            y2 = (dot(xs_re[0:HALF, sl].astype(bf16), wre_ref[j])
                  + dot(xs_im[0:HALF, sl].astype(bf16), wim_ref[j]) + dot(u2[j], kd_ref[j]))
            y_e.append(y2[:, :BLOCK_CH])
            y_o.append(y2[:, BLOCK_CH:])
        xs_re[0:SUBLANES, :] = xs_re[HALF:HALF + SUBLANES, :]
        xs_im[0:SUBLANES, :] = xs_im[HALF:HALF + SUBLANES, :]
        y_a = jnp.concatenate([jnp.concatenate(y_e, axis=-1), jnp.concatenate(y_o, axis=-1)], axis=0)
        y_a = y_a + small(SMALL_DSKIP) * u

    if ffn_on:
        up = dot(x1b, wu_ref[...])

    if mixer_on:
        v = c_gate * h
        ve_ext[SUBLANES:SUBLANES + HALF, :] = v[:HALF]
        vo_ext[SUBLANES:SUBLANES + HALF, :] = v[HALF:]
        w0, w1, w2 = (small(SMALL_CONV + k) for k in range(CONV_K))
        z_e = w0 * ve_ext[0:HALF, :] + w1 * vo_ext[0:HALF, :] + w2 * v[:HALF]
        z_o = w0 * vo_ext[0:HALF, :] + w1 * v[:HALF] + w2 * v[HALF:]
        ve_ext[0:SUBLANES, :] = ve_ext[HALF:HALF + SUBLANES, :]
        vo_ext[0:SUBLANES, :] = vo_ext[HALF:HALF + SUBLANES, :]
        bz = (b_gate * jnp.concatenate([z_e, z_o], axis=0)).astype(bf16)

        g = jax.nn.gelu(y_a)
        glu = dot(g.astype(bf16), gluw_ref[...]) + small(SMALL_GLUB)
        y_b = dot(bz, wconv_ref[...])

    if ffn_on:
        hid = (jax.nn.silu(gate) * up).astype(bf16)
        ffn_lo = dot(hid, wd_ref[:, 0:D_MODEL // 2])
    if mixer_on:
        y_a = dot((g * jax.nn.sigmoid(glu)).astype(bf16), wssm_ref[...])
    if ffn_on:
        ffn_hi = dot(hid, wd_ref[:, D_MODEL // 2:D_MODEL])
        ffn = jnp.concatenate([ffn_lo, ffn_hi], axis=-1)
    out = pre1_next = None
    if mixer_on:
        merged = jax.nn.sigmoid(gate_a) * y_a + jax.nn.sigmoid(gate_b) * y_b
        pre1_next = ALPHA * x + dot(merged.astype(bf16), wo_ref[...])
    if ffn_on:
        out = _layer_norm(ALPHA * x1 + ffn, ln2g_ref[...], ln2b_ref[...])
    return out, pre1_next


N_MIXER_PARAMS = 12
N_FFN_PARAMS = 5
CAST_PARAMS = (0, 6, 7, 8, 9, 12, 13, 14)
SMALL_DSKIP, SMALL_GLUB, SMALL_CONV = 0, 1, 2
SMALL_WIDTH = 8 * CONV_WIDTH
CAST_ROWS = 128


def _load_weights_bf16(hbm_refs, vmem_refs, stage, sem):
    width = stage.shape[-1]
    fills = []
    for w_hbm, w_vmem in zip(hbm_refs, vmem_refs):
        rows, cols = w_hbm.shape
        assert rows % CAST_ROWS == 0 and cols <= width
        per_fill = width // cols
        row_chunks = list(range(0, rows, CAST_ROWS))
        for f0 in range(0, len(row_chunks), per_fill):
            fills.append([(w_hbm, w_vmem, r0, cols, q * cols)
                          for q, r0 in enumerate(row_chunks[f0:f0 + per_fill])])

    def copies(f):
        return [pltpu.make_async_copy(w_hbm.at[pl.ds(r0, CAST_ROWS), :],
                                      stage.at[f % 2, :, pl.ds(lane0, cols)], sem.at[f % 2])
                for w_hbm, _, r0, cols, lane0 in fills[f]]

    for cp in copies(0):
        cp.start()
    for f in range(len(fills)):
        if f + 1 < len(fills):
            for cp in copies(f + 1):
                cp.start()
        for cp in copies(f):
            cp.wait()
        for _, w_vmem, r0, cols, lane0 in fills[f]:
            w_vmem[pl.ds(r0, CAST_ROWS), :] = stage[f % 2, :, lane0:lane0 + cols].astype(bf16)


def _layer_kernel(x_hbm, *refs):
    n_params = N_MIXER_PARAMS + N_FFN_PARAMS
    params = list(refs[:n_params])
    o_hbm, xbuf, xsem, obuf, osem, pre1_ref, xs_re, xs_im, ve_ext, vo_ext, stage, wsem = \
        refs[n_params:n_params + 12]
    ssm_refs = refs[n_params + 12:n_params + 17]
    w_vmem = refs[n_params + 17:]
    w_hbm = [params[p] for p in CAST_PARAMS]
    for p, w in zip(CAST_PARAMS, w_vmem):
        params[p] = w
    mixer_refs = params[:N_MIXER_PARAMS]
    ffn_refs = params[N_MIXER_PARAMS:]
    i = pl.program_id(0)
    slot = i % 2
    ffn_tile = i - 1
    ffn_slot = (i + 1) % 2

    @pl.when(i == 0)
    def _():
        for cp in _tile_copies(x_hbm, xbuf, xsem, 0, 0, True):
            cp.start()
        zeros = jnp.zeros((SUBLANES, N_STATE), f32)
        xs_re[0:SUBLANES, :] = zeros
        xs_im[0:SUBLANES, :] = zeros
        ve_ext[0:SUBLANES, :] = jnp.zeros((SUBLANES, CONV_WIDTH), f32)
        vo_ext[0:SUBLANES, :] = jnp.zeros((SUBLANES, CONV_WIDTH), f32)
        _ssm_prep(*mixer_refs[3:6], *ssm_refs)
        _load_weights_bf16(w_hbm, w_vmem, stage, wsem)

    @pl.when(i + 1 < N_TILES)
    def _():
        for cp in _tile_copies(x_hbm, xbuf, xsem, i + 1, 1 - slot, True):
            cp.start()

    @pl.when(i < N_TILES)
    def _():
        _tile_wait(xbuf, xsem, slot)

    @pl.when(ffn_tile >= 2)
    def _():
        _tile_wait(obuf, osem, ffn_slot)

    stage_refs = (mixer_refs, ssm_refs, ffn_refs, xs_re, xs_im, ve_ext, vo_ext)

    @pl.when(i == 0)
    def _():
        _, pre1 = _layer_tile(None, _split_phases(xbuf[slot]), *stage_refs)
        pre1_ref[...] = pre1

    @pl.when((i > 0) & (i < N_TILES))
    def _():
        y, pre1 = _layer_tile(pre1_ref[...], _split_phases(xbuf[slot]), *stage_refs)
        obuf[ffn_slot] = _merge_phases(y)
        pre1_ref[...] = pre1

    @pl.when(i == N_TILES)
    def _():
        y, _ = _layer_tile(pre1_ref[...], None, *stage_refs)
        obuf[ffn_slot] = _merge_phases(y)

    @pl.when(i > 0)
    def _():
        for cp in _tile_copies(o_hbm, obuf, osem, ffn_tile, ffn_slot, False):
            cp.start()

    @pl.when(i == N_TILES)
    def _():
        _tile_wait(obuf, osem, 1 - ffn_slot)
        _tile_wait(obuf, osem, ffn_slot)


def _const_spec(shape):
    nd = len(shape)
    return pl.BlockSpec(shape, lambda i, nd=nd: (0,) * nd, pipeline_mode=pl.Buffered(1))


def _ssm_prep(vec_ref, b_ref, c_ref, bmat_ref, a2_ref, wre_ref, wim_ref, kd_ref):
    lam_re, lam_im = vec_ref[0:1, :], vec_ref[1:2, :]
    dt = jnp.exp(vec_ref[2:3, :])
    mag = jnp.exp(lam_re * dt)
    a_re = mag * jnp.cos(lam_im * dt)
    a_im = mag * jnp.sin(lam_im * dt)
    den = lam_re * lam_re + lam_im * lam_im
    num_re = a_re - 1.0
    fr = (num_re * lam_re + a_im * lam_im) / den
    fi = (a_im * lam_re - num_re * lam_im) / den
    a2_re = a_re * a_re - a_im * a_im
    a2_im = 2.0 * a_re * a_im
    a2_ref[0:1, :] = a2_re
    a2_ref[1:2, :] = a2_im

    def cmul(xr, xi, yr, yi):
        return xr * yr - xi * yi, xr * yi + xi * yr

    b_re, b_im, c_re, c_im = b_ref[0], b_ref[1], c_ref[0], c_ref[1]
    bb_re, bb_im = cmul(fr, fi, b_re, b_im)
    abb_re, abb_im = cmul(a_re, a_im, bb_re, bb_im)
    ca_re, ca_im = cmul(a_re, a_im, c_re, c_im)
    ca2_re, ca2_im = cmul(a2_re, a2_im, c_re, c_im)

    row_group = lax.broadcasted_iota(jnp.int32, (BLOCK_CH, BLOCK_ST), 0) // SSM_GROUP
    col_group = lax.broadcasted_iota(jnp.int32, (BLOCK_CH, BLOCK_ST), 1) // SSM_STATE
    diag = row_group == col_group

    for j in range(N_BLOCKS):
        def bd(m):
            blk = m[:, j * BLOCK_ST:(j + 1) * BLOCK_ST]
            return jnp.where(diag, jnp.concatenate([blk] * GROUPS_PER_BLOCK, axis=0), 0.0)

        def nt(x, y):
            return lax.dot_general(x, y, (((1,), (1,)), ((), ())), precision=lax.Precision.HIGHEST,
                                   preferred_element_type=f32)

        bbr, bbi, abr, abi = bd(bb_re), bd(bb_im), bd(abb_re), bd(abb_im)
        cr, ci = bd(c_re), bd(c_im)
        bmat_ref[j, 0:BLOCK_CH, 0:BLOCK_ST] = abr.astype(bf16)
        bmat_ref[j, 0:BLOCK_CH, BLOCK_ST:2 * BLOCK_ST] = abi.astype(bf16)
        bmat_ref[j, BLOCK_CH:2 * BLOCK_CH, 0:BLOCK_ST] = bbr.astype(bf16)
        bmat_ref[j, BLOCK_CH:2 * BLOCK_CH, BLOCK_ST:2 * BLOCK_ST] = bbi.astype(bf16)
        wre_ref[j, :, 0:BLOCK_CH] = bd(ca_re).T.astype(bf16)
        wre_ref[j, :, BLOCK_CH:2 * BLOCK_CH] = bd(ca2_re).T.astype(bf16)
        wim_ref[j, :, 0:BLOCK_CH] = (-bd(ca_im)).T.astype(bf16)
        wim_ref[j, :, BLOCK_CH:2 * BLOCK_CH] = (-bd(ca2_im)).T.astype(bf16)
        k0 = nt(bbr, cr) - nt(bbi, ci)
        k1 = nt(abr, cr) - nt(abi, ci)
        kd_ref[j, 0:BLOCK_CH, 0:BLOCK_CH] = k0.astype(bf16)
        kd_ref[j, 0:BLOCK_CH, BLOCK_CH:2 * BLOCK_CH] = k1.astype(bf16)
        kd_ref[j, BLOCK_CH:2 * BLOCK_CH, 0:BLOCK_CH] = jnp.zeros((BLOCK_CH, BLOCK_CH), bf16)
        kd_ref[j, BLOCK_CH:2 * BLOCK_CH, BLOCK_CH:2 * BLOCK_CH] = k0.astype(bf16)


def _ssm_inputs(lam_re, lam_im, log_dt, b_re, b_im, c_re, c_im):
    ldt = jnp.broadcast_to(log_dt[:, None], (SSM_GROUPS, SSM_STATE))
    vec = jnp.stack([lam_re, lam_im, ldt]).astype(f32).reshape(3, N_STATE)
    b_t = jnp.transpose(jnp.stack([b_re, b_im]).astype(f32), (0, 3, 1, 2))
    c_t = jnp.transpose(jnp.stack([c_re, c_im]).astype(f32), (0, 2, 1, 3))
    return vec, b_t.reshape(2, SSM_GROUP, N_STATE), c_t.reshape(2, SSM_GROUP, N_STATE)


def _layer(x, w_in, b_in, lam_re, lam_im, log_dt, b_re, b_im, c_re, c_im, d_skip, glu_w, glu_b,
           w_ssm_out, conv_w, w_conv_out, w_o, ln1_g, ln1_b, w_gate, w_up, w_down, ln2_g, ln2_b):
    row = lambda v: v.reshape(1, -1).astype(f32)
    w32 = lambda w: w.astype(f32)
    small = jnp.concatenate([d_skip, glu_b, conv_w.reshape(-1)]).astype(f32)
    small = jnp.pad(small, (0, SMALL_WIDTH - small.size)).reshape(1, SMALL_WIDTH)
    mixer_in = (w32(w_in), row(b_in), small,
                *_ssm_inputs(lam_re, lam_im, log_dt, b_re, b_im, c_re, c_im),
                w32(glu_w), w32(w_ssm_out), w32(w_conv_out), w32(w_o), row(ln1_g), row(ln1_b))
    ffn_in = (w32(w_gate), w32(w_up), w32(w_down), row(ln2_g), row(ln2_b))
    assert len(mixer_in) == N_MIXER_PARAMS and len(ffn_in) == N_FFN_PARAMS
    params = mixer_in + ffn_in
    param_specs = [pl.BlockSpec(memory_space=pl.ANY) if p in CAST_PARAMS else _const_spec(a.shape)
                   for p, a in enumerate(params)]
    stage_cols = max(params[p].shape[1] for p in CAST_PARAMS)
    return pl.pallas_call(
        _layer_kernel,
        grid=(N_TILES + 1,),
        in_specs=[pl.BlockSpec(memory_space=pl.ANY)] + param_specs,
        out_specs=pl.BlockSpec(memory_space=pl.ANY),
        out_shape=jax.ShapeDtypeStruct((BATCH, SEQ, D_MODEL), f32),
        scratch_shapes=[pltpu.VMEM((2, T_TILE, BATCH, D_MODEL), f32),
                        pltpu.SemaphoreType.DMA((2,)),
                        pltpu.VMEM((2, T_TILE, BATCH, D_MODEL), f32),
                        pltpu.SemaphoreType.DMA((2,)),
                        pltpu.VMEM((R_TILE, D_MODEL), f32),
                        pltpu.VMEM((HALF + SUBLANES, N_STATE), f32),
                        pltpu.VMEM((HALF + SUBLANES, N_STATE), f32),
                        pltpu.VMEM((HALF + SUBLANES, CONV_WIDTH), f32),
                        pltpu.VMEM((HALF + SUBLANES, CONV_WIDTH), f32),
                        pltpu.VMEM((2, CAST_ROWS, stage_cols), f32),
                        pltpu.SemaphoreType.DMA((2,)),
                        pltpu.VMEM((N_BLOCKS, 2 * BLOCK_CH, 2 * BLOCK_ST), bf16),
                        pltpu.VMEM((2, N_STATE), f32),
                        pltpu.VMEM((N_BLOCKS, BLOCK_ST, 2 * BLOCK_CH), bf16),
                        pltpu.VMEM((N_BLOCKS, BLOCK_ST, 2 * BLOCK_CH), bf16),
                        pltpu.VMEM((N_BLOCKS, 2 * BLOCK_CH, 2 * BLOCK_CH), bf16)]
                       + [pltpu.VMEM(params[p].shape, bf16) for p in CAST_PARAMS],
        compiler_params=pltpu.CompilerParams(dimension_semantics=("arbitrary",),
                                             vmem_limit_bytes=VMEM_LIMIT_BYTES),
        name="layer",
    )(x, *params)


def kernel(x, w_in, b_in, ssm_lambda_re, ssm_lambda_im, ssm_log_dt, ssm_b_re, ssm_b_im, ssm_c_re, ssm_c_im, ssm_d, glu_w, glu_b, w_ssm_out, conv_w, w_conv_out, w_o, ln1_g, ln1_b, w_gate, w_up, w_down, ln2_g, ln2_b):
    assert x.shape == (BATCH, SEQ, D_MODEL) and BATCH == SUBLANES
    for l in range(w_in.shape[0]):
        x = _layer(x, w_in[l], b_in[l], ssm_lambda_re[l], ssm_lambda_im[l], ssm_log_dt[l],
                   ssm_b_re[l], ssm_b_im[l], ssm_c_re[l], ssm_c_im[l], ssm_d[l], glu_w[l],
                   glu_b[l], w_ssm_out[l], conv_w[l], w_conv_out[l], w_o[l], ln1_g[l], ln1_b[l],
                   w_gate[l], w_up[l], w_down[l], ln2_g[l], ln2_b[l])
    return x
```

```python
import jax
import jax.numpy as jnp
from jax import lax
from jax.experimental import pallas as pl
from jax.experimental.pallas import tpu as pltpu

D_MODEL = 1024
BATCH = 8
SEQ = 4096
SSM_WIDTH = D_MODEL // 2
SSM_GROUP = 16
SSM_GROUPS = SSM_WIDTH // SSM_GROUP
SSM_STATE = 64
CONV_WIDTH = D_MODEL // 2
CONV_K = 3
FFN_HIDDEN = 2816
IN_COLS = SSM_WIDTH + 3 * CONV_WIDTH + 2 * D_MODEL
DEPTH = 1
ALPHA = (2.0 * DEPTH) ** 0.25
LN_EPS = 1e-5

SUBLANES = 8
VMEM_LIMIT_BYTES = 60 * 1024 * 1024

GROUPS_PER_BLOCK = 8
N_BLOCKS = SSM_GROUPS // GROUPS_PER_BLOCK
BLOCK_CH = GROUPS_PER_BLOCK * SSM_GROUP
BLOCK_ST = GROUPS_PER_BLOCK * SSM_STATE
N_STATE = SSM_GROUPS * SSM_STATE

PHASES = 2
T_TILE = 32
R_TILE = T_TILE * BATCH
H_TILE = T_TILE // PHASES
HALF = H_TILE * SUBLANES
N_TILES = SEQ // T_TILE
assert N_TILES >= 4 and SEQ % T_TILE == 0 and T_TILE % PHASES == 0

f32 = jnp.float32
bf16 = jnp.bfloat16


def _layer_norm(y, g, b):
    mu = jnp.mean(y, axis=-1, keepdims=True)
    var = jnp.mean(jnp.square(y - mu), axis=-1, keepdims=True)
    return (y - mu) * lax.rsqrt(var + LN_EPS) * g + b


def _tile_copies(hbm, buf, sem, tile, slot, to_vmem):
    copies = []
    for b in range(BATCH):
        h = hbm.at[b, pl.ds(tile * T_TILE, T_TILE), :]
        v = buf.at[slot, :, b, :]
        src, dst = (h, v) if to_vmem else (v, h)
        copies.append(pltpu.make_async_copy(src, dst, sem.at[slot]))
    return copies


def _tile_wait(buf, sem, slot):
    pltpu.make_async_copy(buf.at[slot], buf.at[slot], sem.at[slot]).wait()


def _split_phases(x3):
    x4 = x3.reshape(H_TILE, PHASES, SUBLANES, x3.shape[-1])
    return jnp.concatenate([x4[:, ph].reshape(HALF, x3.shape[-1]) for ph in range(PHASES)], axis=0)


def _merge_phases(y):
    parts = [y[ph * HALF:(ph + 1) * HALF].reshape(H_TILE, SUBLANES, y.shape[-1])
             for ph in range(PHASES)]
    return jnp.stack(parts, axis=1).reshape(T_TILE, SUBLANES, y.shape[-1])


def _layer_tile(pre1, x, mixer_refs, ssm_refs, ffn_refs, xs_re, xs_im, ve_ext, vo_ext):
    win_ref, small_ref = mixer_refs[:2]
    gluw_ref, wssm_ref, wconv_ref, wo_ref = mixer_refs[5:]
    bmat_ref, a2_ref, wre_ref, wim_ref, kd_ref = ssm_refs
    wg_ref, wu_ref, wd_ref = ffn_refs
    half_row = lambda r, k: small_ref[r:r + 1, k * SMALL_COLS // 2:(k + 1) * SMALL_COLS // 2]
    full_row = lambda r: small_ref[r:r + 1, :]
    ffn_on, mixer_on = pre1 is not None, x is not None
    dot = lambda a, w: jnp.dot(a, w, preferred_element_type=f32)
    if ffn_on:
        x1 = _layer_norm(pre1, full_row(ROW_LN1_G), full_row(ROW_LN1_B))
        x1b = x1.astype(bf16)
    if mixer_on:
        xb = x.astype(bf16)

        def proj(lo, hi):
            c0 = lo % SMALL_COLS
            return dot(xb, win_ref[:, lo:hi]) + small_ref[lo // SMALL_COLS:lo // SMALL_COLS + 1,
                                                          c0:c0 + hi - lo]

        u = proj(0, SSM_WIDTH)
        h = proj(SSM_WIDTH, SSM_WIDTH + CONV_WIDTH)
        ub = u.astype(bf16)
        u2 = []
        for j in range(N_BLOCKS):
            cs = slice(j * BLOCK_CH, (j + 1) * BLOCK_CH)
            u2.append(jnp.concatenate([ub[:HALF, cs], ub[HALF:, cs]], axis=-1))
            bu = dot(u2[j], bmat_ref[j])
            xs_re[SUBLANES:SUBLANES + HALF, j * BLOCK_ST:(j + 1) * BLOCK_ST] = bu[:, :BLOCK_ST]
            xs_im[SUBLANES:SUBLANES + HALF, j * BLOCK_ST:(j + 1) * BLOCK_ST] = bu[:, BLOCK_ST:]

        o4 = SSM_WIDTH + 3 * CONV_WIDTH
        c_gate = proj(SSM_WIDTH + CONV_WIDTH, SSM_WIDTH + 2 * CONV_WIDTH)
        b_gate = proj(SSM_WIDTH + 2 * CONV_WIDTH, o4)
        gate_a = proj(o4, o4 + D_MODEL)
        gate_b = proj(o4 + D_MODEL, o4 + 2 * D_MODEL)

        for j in range(N_BLOCKS):
            sl = slice(j * BLOCK_ST, (j + 1) * BLOCK_ST)
            ar = jnp.broadcast_to(a2_ref[0:1, sl], (SUBLANES, BLOCK_ST))
            ai = jnp.broadcast_to(a2_ref[1:2, sl], (SUBLANES, BLOCK_ST))
            sr, si = xs_re[0:SUBLANES, sl], xs_im[0:SUBLANES, sl]
            for k in range(H_TILE):
                r = slice((k + 1) * SUBLANES, (k + 2) * SUBLANES)
                sr, si = (ar * sr - ai * si + xs_re[r, sl], ar * si + ai * sr + xs_im[r, sl])
                xs_re[r, sl] = sr
                xs_im[r, sl] = si

    if ffn_on:
        gate = dot(x1b, wg_ref[...])

    if mixer_on:
        y_e, y_o = [], []
        for j in range(N_BLOCKS):
            sl = slice(j * BLOCK_ST, (j + 1) * BLOCK_ST)
            y2 = (dot(xs_re[0:HALF, sl].astype(bf16), wre_ref[j])
                  + dot(xs_im[0:HALF, sl].astype(bf16), wim_ref[j]) + dot(u2[j], kd_ref[j]))
            y_e.append(y2[:, :BLOCK_CH])
            y_o.append(y2[:, BLOCK_CH:])
        xs_re[0:SUBLANES, :] = xs_re[HALF:HALF + SUBLANES, :]
        xs_im[0:SUBLANES, :] = xs_im[HALF:HALF + SUBLANES, :]
        y_a = jnp.concatenate([jnp.concatenate(y_e, axis=-1), jnp.concatenate(y_o, axis=-1)], axis=0)
        y_a = y_a + half_row(ROW_DSKIP_GLUB, 0) * u

    if ffn_on:
        up = dot(x1b, wu_ref[...])

    if mixer_on:
        v = c_gate * h
        ve_ext[SUBLANES:SUBLANES + HALF, :] = v[:HALF]
        vo_ext[SUBLANES:SUBLANES + HALF, :] = v[HALF:]
        w0, w1, w2 = half_row(ROW_CONV, 0), half_row(ROW_CONV, 1), half_row(ROW_CONV + 1, 0)
        z_e = w0 * ve_ext[0:HALF, :] + w1 * vo_ext[0:HALF, :] + w2 * v[:HALF]
        z_o = w0 * vo_ext[0:HALF, :] + w1 * v[:HALF] + w2 * v[HALF:]
        ve_ext[0:SUBLANES, :] = ve_ext[HALF:HALF + SUBLANES, :]
        vo_ext[0:SUBLANES, :] = vo_ext[HALF:HALF + SUBLANES, :]
        bz = (b_gate * jnp.concatenate([z_e, z_o], axis=0)).astype(bf16)

        g = jax.nn.gelu(y_a)
        glu = dot(g.astype(bf16), gluw_ref[...]) + half_row(ROW_DSKIP_GLUB, 1)
        y_b = dot(bz, wconv_ref[...])

    if ffn_on:
        hid = (jax.nn.silu(gate) * up).astype(bf16)
        ffn_lo = dot(hid, wd_ref[:, 0:D_MODEL // 2])
    if mixer_on:
        y_a = dot((g * jax.nn.sigmoid(glu)).astype(bf16), wssm_ref[...])
    if ffn_on:
        ffn_hi = dot(hid, wd_ref[:, D_MODEL // 2:D_MODEL])
        ffn = jnp.concatenate([ffn_lo, ffn_hi], axis=-1)
    out = pre1_next = None
    if mixer_on:
        merged = jax.nn.sigmoid(gate_a) * y_a + jax.nn.sigmoid(gate_b) * y_b
        pre1_next = ALPHA * x + dot(merged.astype(bf16), wo_ref[...])
    if ffn_on:
        out = _layer_norm(ALPHA * x1 + ffn, full_row(ROW_LN2_G), full_row(ROW_LN2_B))
    return out, pre1_next


N_MIXER_PARAMS = 9
N_FFN_PARAMS = 3
CAST_PARAMS = (0, 5, 6, 7, 8, 9, 10, 11)
SMALL_COLS = D_MODEL
ROW_DSKIP_GLUB = IN_COLS // SMALL_COLS
ROW_CONV = ROW_DSKIP_GLUB + 1
ROW_LN1_G, ROW_LN1_B, ROW_LN2_G, ROW_LN2_B = (ROW_CONV + 2 + r for r in range(4))
SMALL_ROWS = 16
CAST_ROWS = 128


def _load_weights_bf16(hbm_refs, vmem_refs, stage, sem):
    width = stage.shape[-1]
    fills = []
    for w_hbm, w_vmem in zip(hbm_refs, vmem_refs):
        rows, cols = w_hbm.shape
        assert rows % CAST_ROWS == 0 and cols <= width
        per_fill = width // cols
        row_chunks = list(range(0, rows, CAST_ROWS))
        for f0 in range(0, len(row_chunks), per_fill):
            fills.append([(w_hbm, w_vmem, r0, cols, q * cols)
                          for q, r0 in enumerate(row_chunks[f0:f0 + per_fill])])

    def copies(f):
        return [pltpu.make_async_copy(w_hbm.at[pl.ds(r0, CAST_ROWS), :],
                                      stage.at[f % 2, :, pl.ds(lane0, cols)], sem.at[f % 2])
                for w_hbm, _, r0, cols, lane0 in fills[f]]

    for cp in copies(0):
        cp.start()
    for f in range(len(fills)):
        if f + 1 < len(fills):
            for cp in copies(f + 1):
                cp.start()
        for cp in copies(f):
            cp.wait()
        for _, w_vmem, r0, cols, lane0 in fills[f]:
            w_vmem[pl.ds(r0, CAST_ROWS), :] = stage[f % 2, :, lane0:lane0 + cols].astype(bf16)


def _layer_kernel(x_hbm, *refs):
    n_params = N_MIXER_PARAMS + N_FFN_PARAMS
    params = list(refs[:n_params])
    o_hbm, xbuf, xsem, obuf, osem, pre1_ref, xs_re, xs_im, ve_ext, vo_ext, stage, wsem = \
        refs[n_params:n_params + 12]
    ssm_refs = refs[n_params + 12:n_params + 17]
    w_vmem = refs[n_params + 17:]
    w_hbm = [params[p] for p in CAST_PARAMS]
    for p, w in zip(CAST_PARAMS, w_vmem):
        params[p] = w
    mixer_refs = params[:N_MIXER_PARAMS]
    ffn_refs = params[N_MIXER_PARAMS:]
    stage_refs = (mixer_refs, ssm_refs, ffn_refs, xs_re, xs_im, ve_ext, vo_ext)

    for cp in _tile_copies(x_hbm, xbuf, xsem, 0, 0, True):
        cp.start()
    zeros = jnp.zeros((SUBLANES, N_STATE), f32)
    xs_re[0:SUBLANES, :] = zeros
    xs_im[0:SUBLANES, :] = zeros
    ve_ext[0:SUBLANES, :] = jnp.zeros((SUBLANES, CONV_WIDTH), f32)
    vo_ext[0:SUBLANES, :] = jnp.zeros((SUBLANES, CONV_WIDTH), f32)
    _ssm_prep(*mixer_refs[2:5], *ssm_refs)
    _load_weights_bf16(w_hbm, w_vmem, stage, wsem)

    for cp in _tile_copies(x_hbm, xbuf, xsem, 1, 1, True):
        cp.start()
    _tile_wait(xbuf, xsem, 0)
    _, pre1 = _layer_tile(None, _split_phases(xbuf[0]), *stage_refs)
    pre1_ref[...] = pre1

    def step(i, carry):
        slot = i % 2
        ffn_slot = 1 - slot

        @pl.when(i + 1 < N_TILES)
        def _():
            for cp in _tile_copies(x_hbm, xbuf, xsem, i + 1, 1 - slot, True):
                cp.start()

        _tile_wait(xbuf, xsem, slot)

        @pl.when(i >= 3)
        def _():
            _tile_wait(obuf, osem, ffn_slot)

        y, pre1 = _layer_tile(pre1_ref[...], _split_phases(xbuf[slot]), *stage_refs)
        obuf[ffn_slot] = _merge_phases(y)
        pre1_ref[...] = pre1
        for cp in _tile_copies(o_hbm, obuf, osem, i - 1, ffn_slot, False):
            cp.start()
        return carry

    lax.fori_loop(1, N_TILES, step, 0)

    last_slot = (N_TILES - 1) % 2
    _tile_wait(obuf, osem, last_slot)
    y, _ = _layer_tile(pre1_ref[...], None, *stage_refs)
    obuf[last_slot] = _merge_phases(y)
    for cp in _tile_copies(o_hbm, obuf, osem, N_TILES - 1, last_slot, False):
        cp.start()
    _tile_wait(obuf, osem, 1 - last_slot)
    _tile_wait(obuf, osem, last_slot)


def _ssm_prep(vec_ref, b_ref, c_ref, bmat_ref, a2_ref, wre_ref, wim_ref, kd_ref):
    lam_re, lam_im = vec_ref[0:1, :], vec_ref[1:2, :]
    dt = jnp.exp(vec_ref[2:3, :])
    mag = jnp.exp(lam_re * dt)
    a_re = mag * jnp.cos(lam_im * dt)
    a_im = mag * jnp.sin(lam_im * dt)
    den = lam_re * lam_re + lam_im * lam_im
    num_re = a_re - 1.0
    fr = (num_re * lam_re + a_im * lam_im) / den
    fi = (a_im * lam_re - num_re * lam_im) / den
    a2_re = a_re * a_re - a_im * a_im
    a2_im = 2.0 * a_re * a_im
    a2_ref[0:1, :] = a2_re
    a2_ref[1:2, :] = a2_im

    def cmul(xr, xi, yr, yi):
        return xr * yr - xi * yi, xr * yi + xi * yr

    b_re, b_im, c_re, c_im = b_ref[0], b_ref[1], c_ref[0], c_ref[1]
    bb_re, bb_im = cmul(fr, fi, b_re, b_im)
    abb_re, abb_im = cmul(a_re, a_im, bb_re, bb_im)
    ca_re, ca_im = cmul(a_re, a_im, c_re, c_im)
    ca2_re, ca2_im = cmul(a2_re, a2_im, c_re, c_im)

    row_group = lax.broadcasted_iota(jnp.int32, (BLOCK_CH, BLOCK_ST), 0) // SSM_GROUP
    col_group = lax.broadcasted_iota(jnp.int32, (BLOCK_CH, BLOCK_ST), 1) // SSM_STATE
    diag = row_group == col_group

    for j in range(N_BLOCKS):
        def bd(m):
            blk = m[:, j * BLOCK_ST:(j + 1) * BLOCK_ST]
            return jnp.where(diag, jnp.concatenate([blk] * GROUPS_PER_BLOCK, axis=0), 0.0)

        def nt(x, y):
            return lax.dot_general(x, y, (((1,), (1,)), ((), ())), precision=lax.Precision.HIGHEST,
                                   preferred_element_type=f32)

        bbr, bbi, abr, abi = bd(bb_re), bd(bb_im), bd(abb_re), bd(abb_im)
        cr, ci = bd(c_re), bd(c_im)
        bmat_ref[j, 0:BLOCK_CH, 0:BLOCK_ST] = abr.astype(bf16)
        bmat_ref[j, 0:BLOCK_CH, BLOCK_ST:2 * BLOCK_ST] = abi.astype(bf16)
        bmat_ref[j, BLOCK_CH:2 * BLOCK_CH, 0:BLOCK_ST] = bbr.astype(bf16)
        bmat_ref[j, BLOCK_CH:2 * BLOCK_CH, BLOCK_ST:2 * BLOCK_ST] = bbi.astype(bf16)
        wre_ref[j, :, 0:BLOCK_CH] = bd(ca_re).T.astype(bf16)
        wre_ref[j, :, BLOCK_CH:2 * BLOCK_CH] = bd(ca2_re).T.astype(bf16)
        wim_ref[j, :, 0:BLOCK_CH] = (-bd(ca_im)).T.astype(bf16)
        wim_ref[j, :, BLOCK_CH:2 * BLOCK_CH] = (-bd(ca2_im)).T.astype(bf16)
        k0 = nt(bbr, cr) - nt(bbi, ci)
        k1 = nt(abr, cr) - nt(abi, ci)
        kd_ref[j, 0:BLOCK_CH, 0:BLOCK_CH] = k0.astype(bf16)
        kd_ref[j, 0:BLOCK_CH, BLOCK_CH:2 * BLOCK_CH] = k1.astype(bf16)
        kd_ref[j, BLOCK_CH:2 * BLOCK_CH, 0:BLOCK_CH] = jnp.zeros((BLOCK_CH, BLOCK_CH), bf16)
        kd_ref[j, BLOCK_CH:2 * BLOCK_CH, BLOCK_CH:2 * BLOCK_CH] = k0.astype(bf16)


def _ssm_inputs(lam_re, lam_im, log_dt, b_re, b_im, c_re, c_im):
    ldt = jnp.broadcast_to(log_dt[:, None], (SSM_GROUPS, SSM_STATE))
    vec = jnp.stack([lam_re, lam_im, ldt]).astype(f32).reshape(3, N_STATE)
    b_t = jnp.transpose(jnp.stack([b_re, b_im]).astype(f32), (0, 3, 1, 2))
    c_t = jnp.transpose(jnp.stack([c_re, c_im]).astype(f32), (0, 2, 1, 3))
    return vec, b_t.reshape(2, SSM_GROUP, N_STATE), c_t.reshape(2, SSM_GROUP, N_STATE)


def _layer(x, w_in, b_in, lam_re, lam_im, log_dt, b_re, b_im, c_re, c_im, d_skip, glu_w, glu_b,
           w_ssm_out, conv_w, w_conv_out, w_o, ln1_g, ln1_b, w_gate, w_up, w_down, ln2_g, ln2_b):
    w32 = lambda w: w.astype(f32)
    flat = lambda *vs: jnp.concatenate([v.astype(f32).reshape(-1) for v in vs])
    small = flat(b_in, d_skip, glu_b, conv_w, jnp.zeros((CONV_WIDTH,), f32), ln1_g, ln1_b, ln2_g, ln2_b)
    small = jnp.pad(small, (0, SMALL_ROWS * SMALL_COLS - small.size)).reshape(SMALL_ROWS, SMALL_COLS)
    mixer_in = (w32(w_in), small, *_ssm_inputs(lam_re, lam_im, log_dt, b_re, b_im, c_re, c_im),
                w32(glu_w), w32(w_ssm_out), w32(w_conv_out), w32(w_o))
    ffn_in = (w32(w_gate), w32(w_up), w32(w_down))
    assert len(mixer_in) == N_MIXER_PARAMS and len(ffn_in) == N_FFN_PARAMS
    params = mixer_in + ffn_in
    param_specs = [pl.BlockSpec(memory_space=pl.ANY if p in CAST_PARAMS else pltpu.VMEM)
                   for p, a in enumerate(params)]
    stage_cols = max(params[p].shape[1] for p in CAST_PARAMS)
    return pl.pallas_call(
        _layer_kernel,
        in_specs=[pl.BlockSpec(memory_space=pl.ANY)] + param_specs,
        out_specs=pl.BlockSpec(memory_space=pl.ANY),
        out_shape=jax.ShapeDtypeStruct((BATCH, SEQ, D_MODEL), f32),
        scratch_shapes=[pltpu.VMEM((2, T_TILE, BATCH, D_MODEL), f32),
                        pltpu.SemaphoreType.DMA((2,)),
                        pltpu.VMEM((2, T_TILE, BATCH, D_MODEL), f32),
                        pltpu.SemaphoreType.DMA((2,)),
                        pltpu.VMEM((R_TILE, D_MODEL), f32),
                        pltpu.VMEM((HALF + SUBLANES, N_STATE), f32),
                        pltpu.VMEM((HALF + SUBLANES, N_STATE), f32),
                        pltpu.VMEM((HALF + SUBLANES, CONV_WIDTH), f32),
                        pltpu.VMEM((HALF + SUBLANES, CONV_WIDTH), f32),
                        pltpu.VMEM((2, CAST_ROWS, stage_cols), f32),
                        pltpu.SemaphoreType.DMA((2,)),
                        pltpu.VMEM((N_BLOCKS, 2 * BLOCK_CH, 2 * BLOCK_ST), bf16),
                        pltpu.VMEM((2, N_STATE), f32),
                        pltpu.VMEM((N_BLOCKS, BLOCK_ST, 2 * BLOCK_CH), bf16),
                        pltpu.VMEM((N_BLOCKS, BLOCK_ST, 2 * BLOCK_CH), bf16),
                        pltpu.VMEM((N_BLOCKS, 2 * BLOCK_CH, 2 * BLOCK_CH), bf16)]
                       + [pltpu.VMEM(params[p].shape, bf16) for p in CAST_PARAMS],
        compiler_params=pltpu.CompilerParams(vmem_limit_bytes=VMEM_LIMIT_BYTES),
        name="layer",
    )(x, *params)


def kernel(x, w_in, b_in, ssm_lambda_re, ssm_lambda_im, ssm_log_dt, ssm_b_re, ssm_b_im, ssm_c_re, ssm_c_im, ssm_d, glu_w, glu_b, w_ssm_out, conv_w, w_conv_out, w_o, ln1_g, ln1_b, w_gate, w_up, w_down, ln2_g, ln2_b):
    assert x.shape == (BATCH, SEQ, D_MODEL) and BATCH == SUBLANES
    for l in range(w_in.shape[0]):
        x = _layer(x, w_in[l], b_in[l], ssm_lambda_re[l], ssm_lambda_im[l], ssm_log_dt[l],
                   ssm_b_re[l], ssm_b_im[l], ssm_c_re[l], ssm_c_im[l], ssm_d[l], glu_w[l],
                   glu_b[l], w_ssm_out[l], conv_w[l], w_conv_out[l], w_o[l], ln1_g[l], ln1_b[l],
                   w_gate[l], w_up[l], w_down[l], ln2_g[l], ln2_b[l])
    return x
```

```python
import jax
import jax.numpy as jnp
from jax import lax
from jax.experimental import pallas as pl
from jax.experimental.pallas import tpu as pltpu

D_MODEL = 1024
BATCH = 8
SEQ = 4096
SSM_WIDTH = D_MODEL // 2
SSM_GROUP = 16
SSM_GROUPS = SSM_WIDTH // SSM_GROUP
SSM_STATE = 64
CONV_WIDTH = D_MODEL // 2
CONV_K = 3
FFN_HIDDEN = 2816
IN_COLS = SSM_WIDTH + 3 * CONV_WIDTH + 2 * D_MODEL
DEPTH = 1
ALPHA = (2.0 * DEPTH) ** 0.25
LN_EPS = 1e-5

SUBLANES = 8
VMEM_LIMIT_BYTES = 60 * 1024 * 1024

GROUPS_PER_BLOCK = 8
N_BLOCKS = SSM_GROUPS // GROUPS_PER_BLOCK
BLOCK_CH = GROUPS_PER_BLOCK * SSM_GROUP
BLOCK_ST = GROUPS_PER_BLOCK * SSM_STATE
N_STATE = SSM_GROUPS * SSM_STATE

PHASES = 2
T_TILE = 32
R_TILE = T_TILE * BATCH
H_TILE = T_TILE // PHASES
HALF = H_TILE * SUBLANES
N_TILES = SEQ // T_TILE
assert N_TILES >= 4 and SEQ % T_TILE == 0 and T_TILE % PHASES == 0

f32 = jnp.float32
bf16 = jnp.bfloat16


def _layer_norm(y, g, b):
    mu = jnp.mean(y, axis=-1, keepdims=True)
    var = jnp.mean(jnp.square(y - mu), axis=-1, keepdims=True)
    return (y - mu) * lax.rsqrt(var + LN_EPS) * g + b


def _tile_copies(hbm, buf, sem, tile, slot, to_vmem):
    copies = []
    for b in range(BATCH):
        h = hbm.at[b, pl.ds(tile * T_TILE, T_TILE), :]
        v = buf.at[slot, :, b, :]
        src, dst = (h, v) if to_vmem else (v, h)
        copies.append(pltpu.make_async_copy(src, dst, sem.at[slot]))
    return copies


def _tile_wait(buf, sem, slot):
    pltpu.make_async_copy(buf.at[slot], buf.at[slot], sem.at[slot]).wait()


def _split_phases(x3):
    x4 = x3.reshape(H_TILE, PHASES, SUBLANES, x3.shape[-1])
    return jnp.concatenate([x4[:, ph].reshape(HALF, x3.shape[-1]) for ph in range(PHASES)], axis=0)


def _merge_phases(y):
    parts = [y[ph * HALF:(ph + 1) * HALF].reshape(H_TILE, SUBLANES, y.shape[-1])
             for ph in range(PHASES)]
    return jnp.stack(parts, axis=1).reshape(T_TILE, SUBLANES, y.shape[-1])


def _layer_tile(pre1, x, mixer_refs, ssm_refs, ffn_refs, xs_re, xs_im, ve_ext, vo_ext):
    win_ref, small_ref = mixer_refs[:2]
    gluw_ref, wssm_ref, wconv_ref, wo_ref = mixer_refs[5:]
    bmat_ref, a2_ref, wre_ref, wim_ref, kd_ref = ssm_refs
    wg_ref, wu_ref, wd_ref = ffn_refs
    half_row = lambda r, k: small_ref[r:r + 1, k * SMALL_COLS // 2:(k + 1) * SMALL_COLS // 2]
    full_row = lambda r: small_ref[r:r + 1, :]
    ffn_on, mixer_on = pre1 is not None, x is not None
    dot = lambda a, w: jnp.dot(a, w, preferred_element_type=f32)
    if ffn_on:
        x1 = _layer_norm(pre1, full_row(ROW_LN1_G), full_row(ROW_LN1_B))
        x1b = x1.astype(bf16)
    if mixer_on:
        xb = x.astype(bf16)

        def proj(lo, hi):
            c0 = lo % SMALL_COLS
            return dot(xb, win_ref[:, lo:hi]) + small_ref[lo // SMALL_COLS:lo // SMALL_COLS + 1,
                                                          c0:c0 + hi - lo]

        u = proj(0, SSM_WIDTH)
        h = proj(SSM_WIDTH, SSM_WIDTH + CONV_WIDTH)
        ub = u.astype(bf16)
        u2 = []
        for j in range(N_BLOCKS):
            cs = slice(j * BLOCK_CH, (j + 1) * BLOCK_CH)
            u2.append(jnp.concatenate([ub[:HALF, cs], ub[HALF:, cs]], axis=-1))
            bu = dot(u2[j], bmat_ref[j])
            xs_re[SUBLANES:SUBLANES + HALF, j * BLOCK_ST:(j + 1) * BLOCK_ST] = bu[:, :BLOCK_ST]
            xs_im[SUBLANES:SUBLANES + HALF, j * BLOCK_ST:(j + 1) * BLOCK_ST] = bu[:, BLOCK_ST:]

        o4 = SSM_WIDTH + 3 * CONV_WIDTH
        c_gate = proj(SSM_WIDTH + CONV_WIDTH, SSM_WIDTH + 2 * CONV_WIDTH)
        b_gate = proj(SSM_WIDTH + 2 * CONV_WIDTH, o4)
        gate_a = proj(o4, o4 + D_MODEL)
        gate_b = proj(o4 + D_MODEL, o4 + 2 * D_MODEL)

        for j in range(N_BLOCKS):
            sl = slice(j * BLOCK_ST, (j + 1) * BLOCK_ST)
            ar = jnp.broadcast_to(a2_ref[0:1, sl], (SUBLANES, BLOCK_ST))
            ai = jnp.broadcast_to(a2_ref[1:2, sl], (SUBLANES, BLOCK_ST))
            sr, si = xs_re[0:SUBLANES, sl], xs_im[0:SUBLANES, sl]
            for k in range(H_TILE):
                r = slice((k + 1) * SUBLANES, (k + 2) * SUBLANES)
                sr, si = (ar * sr - ai * si + xs_re[r, sl], ar * si + ai * sr + xs_im[r, sl])
                xs_re[r, sl] = sr
                xs_im[r, sl] = si

    if ffn_on:
        gate = dot(x1b, wg_ref[...])

    if mixer_on:
        y_e, y_o = [], []
        for j in range(N_BLOCKS):
            sl = slice(j * BLOCK_ST, (j + 1) * BLOCK_ST)
            y2 = (dot(xs_re[0:HALF, sl].astype(bf16), wre_ref[j])
                  + dot(xs_im[0:HALF, sl].astype(bf16), wim_ref[j]) + dot(u2[j], kd_ref[j]))
            y_e.append(y2[:, :BLOCK_CH])
            y_o.append(y2[:, BLOCK_CH:])
        xs_re[0:SUBLANES, :] = xs_re[HALF:HALF + SUBLANES, :]
        xs_im[0:SUBLANES, :] = xs_im[HALF:HALF + SUBLANES, :]
        y_a = jnp.concatenate([jnp.concatenate(y_e, axis=-1), jnp.concatenate(y_o, axis=-1)], axis=0)
        y_a = y_a + half_row(ROW_DSKIP_GLUB, 0) * u

    if ffn_on:
        up = dot(x1b, wu_ref[...])

    if mixer_on:
        v = c_gate * h
        ve_ext[SUBLANES:SUBLANES + HALF, :] = v[:HALF]
        vo_ext[SUBLANES:SUBLANES + HALF, :] = v[HALF:]
        w0, w1, w2 = half_row(ROW_CONV, 0), half_row(ROW_CONV, 1), half_row(ROW_CONV + 1, 0)
        z_e = w0 * ve_ext[0:HALF, :] + w1 * vo_ext[0:HALF, :] + w2 * v[:HALF]
        z_o = w0 * vo_ext[0:HALF, :] + w1 * v[:HALF] + w2 * v[HALF:]
        ve_ext[0:SUBLANES, :] = ve_ext[HALF:HALF + SUBLANES, :]
        vo_ext[0:SUBLANES, :] = vo_ext[HALF:HALF + SUBLANES, :]
        bz = (b_gate * jnp.concatenate([z_e, z_o], axis=0)).astype(bf16)

        g = jax.nn.gelu(y_a)
        glu = dot(g.astype(bf16), gluw_ref[...]) + half_row(ROW_DSKIP_GLUB, 1)
        y_b = dot(bz, wconv_ref[...])

    if ffn_on:
        hid = (jax.nn.silu(gate) * up).astype(bf16)
        ffn_lo = dot(hid, wd_ref[:, 0:D_MODEL // 2])
    if mixer_on:
        y_a = dot((g * jax.nn.sigmoid(glu)).astype(bf16), wssm_ref[...])
    if ffn_on:
        ffn_hi = dot(hid, wd_ref[:, D_MODEL // 2:D_MODEL])
        ffn = jnp.concatenate([ffn_lo, ffn_hi], axis=-1)
    out = pre1_next = None
    if mixer_on:
        merged = jax.nn.sigmoid(gate_a) * y_a + jax.nn.sigmoid(gate_b) * y_b
        pre1_next = ALPHA * x + dot(merged.astype(bf16), wo_ref[...])
    if ffn_on:
        out = _layer_norm(ALPHA * x1 + ffn, full_row(ROW_LN2_G), full_row(ROW_LN2_B))
    return out, pre1_next


N_MIXER_PARAMS = 9
N_FFN_PARAMS = 3
CAST_PARAMS = (0, 5, 6, 7, 8, 9, 10, 11)
SMALL_COLS = D_MODEL
ROW_DSKIP_GLUB = IN_COLS // SMALL_COLS
ROW_CONV = ROW_DSKIP_GLUB + 1
ROW_LN1_G, ROW_LN1_B, ROW_LN2_G, ROW_LN2_B = (ROW_CONV + 2 + r for r in range(4))
SMALL_ROWS = 16
OUT_DMA_PRIORITY = 1
CAST_ROWS = 128


def _load_weights_bf16(hbm_refs, vmem_refs, stage, sem):
    width = stage.shape[-1]
    fills = []
    for w_hbm, w_vmem in zip(hbm_refs, vmem_refs):
        rows, cols = w_hbm.shape
        assert rows % CAST_ROWS == 0 and cols <= width
        per_fill = width // cols
        row_chunks = list(range(0, rows, CAST_ROWS))
        for f0 in range(0, len(row_chunks), per_fill):
            fills.append([(w_hbm, w_vmem, r0, cols, q * cols)
                          for q, r0 in enumerate(row_chunks[f0:f0 + per_fill])])

    def copies(f):
        return [pltpu.make_async_copy(w_hbm.at[pl.ds(r0, CAST_ROWS), :],
                                      stage.at[f % 2, :, pl.ds(lane0, cols)], sem.at[f % 2])
                for w_hbm, _, r0, cols, lane0 in fills[f]]

    for cp in copies(0):
        cp.start()
    for f in range(len(fills)):
        if f + 1 < len(fills):
            for cp in copies(f + 1):
                cp.start()
        for cp in copies(f):
            cp.wait()
        for _, w_vmem, r0, cols, lane0 in fills[f]:
            w_vmem[pl.ds(r0, CAST_ROWS), :] = stage[f % 2, :, lane0:lane0 + cols].astype(bf16)


def _layer_kernel(x_hbm, *refs):
    n_params = N_MIXER_PARAMS + N_FFN_PARAMS
    params = list(refs[:n_params])
    o_hbm, xbuf, xsem, obuf, osem, pre1_ref, xs_re, xs_im, ve_ext, vo_ext, stage, wsem = \
        refs[n_params:n_params + 12]
    ssm_refs = refs[n_params + 12:n_params + 17]
    w_vmem = refs[n_params + 17:]
    w_hbm = [params[p] for p in CAST_PARAMS]
    for p, w in zip(CAST_PARAMS, w_vmem):
        params[p] = w
    mixer_refs = params[:N_MIXER_PARAMS]
    ffn_refs = params[N_MIXER_PARAMS:]
    stage_refs = (mixer_refs, ssm_refs, ffn_refs, xs_re, xs_im, ve_ext, vo_ext)

    for cp in _tile_copies(x_hbm, xbuf, xsem, 0, 0, True):
        cp.start()
    zeros = jnp.zeros((SUBLANES, N_STATE), f32)
    xs_re[0:SUBLANES, :] = zeros
    xs_im[0:SUBLANES, :] = zeros
    ve_ext[0:SUBLANES, :] = jnp.zeros((SUBLANES, CONV_WIDTH), f32)
    vo_ext[0:SUBLANES, :] = jnp.zeros((SUBLANES, CONV_WIDTH), f32)
    _ssm_prep(*mixer_refs[2:5], *ssm_refs)
    _load_weights_bf16(w_hbm, w_vmem, stage, wsem)

    for cp in _tile_copies(x_hbm, xbuf, xsem, 1, 1, True):
        cp.start()
    _tile_wait(xbuf, xsem, 0)
    _, pre1 = _layer_tile(None, _split_phases(xbuf[0]), *stage_refs)
    pre1_ref[...] = pre1

    def step(i, carry):
        slot = i % 2
        ffn_slot = 1 - slot

        @pl.when(i + 1 < N_TILES)
        def _():
            for cp in _tile_copies(x_hbm, xbuf, xsem, i + 1, 1 - slot, True):
                cp.start()

        _tile_wait(xbuf, xsem, slot)

        @pl.when(i >= 3)
        def _():
            _tile_wait(obuf, osem, ffn_slot)

        y, pre1 = _layer_tile(pre1_ref[...], _split_phases(xbuf[slot]), *stage_refs)
        obuf[ffn_slot] = _merge_phases(y)
        pre1_ref[...] = pre1
        for cp in _tile_copies(o_hbm, obuf, osem, i - 1, ffn_slot, False):
            cp.start(priority=OUT_DMA_PRIORITY)
        return carry

    lax.fori_loop(1, N_TILES, step, 0)

    last_slot = (N_TILES - 1) % 2
    _tile_wait(obuf, osem, last_slot)
    y, _ = _layer_tile(pre1_ref[...], None, *stage_refs)
    obuf[last_slot] = _merge_phases(y)
    for cp in _tile_copies(o_hbm, obuf, osem, N_TILES - 1, last_slot, False):
        cp.start(priority=OUT_DMA_PRIORITY)
    _tile_wait(obuf, osem, 1 - last_slot)
    _tile_wait(obuf, osem, last_slot)


def _ssm_prep(vec_ref, b_ref, c_ref, bmat_ref, a2_ref, wre_ref, wim_ref, kd_ref):
    lam_re, lam_im = vec_ref[0:1, :], vec_ref[1:2, :]
    dt = jnp.exp(vec_ref[2:3, :])
    mag = jnp.exp(lam_re * dt)
    a_re = mag * jnp.cos(lam_im * dt)
    a_im = mag * jnp.sin(lam_im * dt)
    den = lam_re * lam_re + lam_im * lam_im
    num_re = a_re - 1.0
    fr = (num_re * lam_re + a_im * lam_im) / den
    fi = (a_im * lam_re - num_re * lam_im) / den
    a2_re = a_re * a_re - a_im * a_im
    a2_im = 2.0 * a_re * a_im
    a2_ref[0:1, :] = a2_re
    a2_ref[1:2, :] = a2_im

    def cmul(xr, xi, yr, yi):
        return xr * yr - xi * yi, xr * yi + xi * yr

    b_re, b_im, c_re, c_im = b_ref[0], b_ref[1], c_ref[0], c_ref[1]
    bb_re, bb_im = cmul(fr, fi, b_re, b_im)
    abb_re, abb_im = cmul(a_re, a_im, bb_re, bb_im)
    ca_re, ca_im = cmul(a_re, a_im, c_re, c_im)
    ca2_re, ca2_im = cmul(a2_re, a2_im, c_re, c_im)

    row_group = lax.broadcasted_iota(jnp.int32, (BLOCK_CH, BLOCK_ST), 0) // SSM_GROUP
    col_group = lax.broadcasted_iota(jnp.int32, (BLOCK_CH, BLOCK_ST), 1) // SSM_STATE
    diag = row_group == col_group

    for j in range(N_BLOCKS):
        def bd(m):
            blk = m[:, j * BLOCK_ST:(j + 1) * BLOCK_ST]
            return jnp.where(diag, jnp.concatenate([blk] * GROUPS_PER_BLOCK, axis=0), 0.0)

        def nt(x, y):
            return lax.dot_general(x, y, (((1,), (1,)), ((), ())), precision=lax.Precision.HIGHEST,
                                   preferred_element_type=f32)

        bbr, bbi, abr, abi = bd(bb_re), bd(bb_im), bd(abb_re), bd(abb_im)
        cr, ci = bd(c_re), bd(c_im)
        bmat_ref[j, 0:BLOCK_CH, 0:BLOCK_ST] = abr.astype(bf16)
        bmat_ref[j, 0:BLOCK_CH, BLOCK_ST:2 * BLOCK_ST] = abi.astype(bf16)
        bmat_ref[j, BLOCK_CH:2 * BLOCK_CH, 0:BLOCK_ST] = bbr.astype(bf16)
        bmat_ref[j, BLOCK_CH:2 * BLOCK_CH, BLOCK_ST:2 * BLOCK_ST] = bbi.astype(bf16)
        wre_ref[j, :, 0:BLOCK_CH] = bd(ca_re).T.astype(bf16)
        wre_ref[j, :, BLOCK_CH:2 * BLOCK_CH] = bd(ca2_re).T.astype(bf16)
        wim_ref[j, :, 0:BLOCK_CH] = (-bd(ca_im)).T.astype(bf16)
        wim_ref[j, :, BLOCK_CH:2 * BLOCK_CH] = (-bd(ca2_im)).T.astype(bf16)
        k0 = nt(bbr, cr) - nt(bbi, ci)
        k1 = nt(abr, cr) - nt(abi, ci)
        kd_ref[j, 0:BLOCK_CH, 0:BLOCK_CH] = k0.astype(bf16)
        kd_ref[j, 0:BLOCK_CH, BLOCK_CH:2 * BLOCK_CH] = k1.astype(bf16)
        kd_ref[j, BLOCK_CH:2 * BLOCK_CH, 0:BLOCK_CH] = jnp.zeros((BLOCK_CH, BLOCK_CH), bf16)
        kd_ref[j, BLOCK_CH:2 * BLOCK_CH, BLOCK_CH:2 * BLOCK_CH] = k0.astype(bf16)


def _ssm_inputs(lam_re, lam_im, log_dt, b_re, b_im, c_re, c_im):
    ldt = jnp.broadcast_to(log_dt[:, None], (SSM_GROUPS, SSM_STATE))
    vec = jnp.stack([lam_re, lam_im, ldt]).astype(f32).reshape(3, N_STATE)
    b_t = jnp.transpose(jnp.stack([b_re, b_im]).astype(f32), (0, 3, 1, 2))
    c_t = jnp.transpose(jnp.stack([c_re, c_im]).astype(f32), (0, 2, 1, 3))
    return vec, b_t.reshape(2, SSM_GROUP, N_STATE), c_t.reshape(2, SSM_GROUP, N_STATE)


def _layer(x, w_in, b_in, lam_re, lam_im, log_dt, b_re, b_im, c_re, c_im, d_skip, glu_w, glu_b,
           w_ssm_out, conv_w, w_conv_out, w_o, ln1_g, ln1_b, w_gate, w_up, w_down, ln2_g, ln2_b):
    w32 = lambda w: w.astype(f32)
    flat = lambda *vs: jnp.concatenate([v.astype(f32).reshape(-1) for v in vs])
    small = flat(b_in, d_skip, glu_b, conv_w, jnp.zeros((CONV_WIDTH,), f32), ln1_g, ln1_b, ln2_g, ln2_b)
    small = jnp.pad(small, (0, SMALL_ROWS * SMALL_COLS - small.size)).reshape(SMALL_ROWS, SMALL_COLS)
    mixer_in = (w32(w_in), small, *_ssm_inputs(lam_re, lam_im, log_dt, b_re, b_im, c_re, c_im),
                w32(glu_w), w32(w_ssm_out), w32(w_conv_out), w32(w_o))
    ffn_in = (w32(w_gate), w32(w_up), w32(w_down))
    assert len(mixer_in) == N_MIXER_PARAMS and len(ffn_in) == N_FFN_PARAMS
    params = mixer_in + ffn_in
    param_specs = [pl.BlockSpec(memory_space=pl.ANY if p in CAST_PARAMS else pltpu.VMEM)
                   for p, a in enumerate(params)]
    stage_cols = max(params[p].shape[1] for p in CAST_PARAMS)
    return pl.pallas_call(
        _layer_kernel,
        in_specs=[pl.BlockSpec(memory_space=pl.ANY)] + param_specs,
        out_specs=pl.BlockSpec(memory_space=pl.ANY),
        out_shape=jax.ShapeDtypeStruct((BATCH, SEQ, D_MODEL), f32),
        scratch_shapes=[pltpu.VMEM((2, T_TILE, BATCH, D_MODEL), f32),
                        pltpu.SemaphoreType.DMA((2,)),
                        pltpu.VMEM((2, T_TILE, BATCH, D_MODEL), f32),
                        pltpu.SemaphoreType.DMA((2,)),
                        pltpu.VMEM((R_TILE, D_MODEL), f32),
                        pltpu.VMEM((HALF + SUBLANES, N_STATE), f32),
                        pltpu.VMEM((HALF + SUBLANES, N_STATE), f32),
                        pltpu.VMEM((HALF + SUBLANES, CONV_WIDTH), f32),
                        pltpu.VMEM((HALF + SUBLANES, CONV_WIDTH), f32),
                        pltpu.VMEM((2, CAST_ROWS, stage_cols), f32),
                        pltpu.SemaphoreType.DMA((2,)),
                        pltpu.VMEM((N_BLOCKS, 2 * BLOCK_CH, 2 * BLOCK_ST), bf16),
                        pltpu.VMEM((2, N_STATE), f32),
                        pltpu.VMEM((N_BLOCKS, BLOCK_ST, 2 * BLOCK_CH), bf16),
                        pltpu.VMEM((N_BLOCKS, BLOCK_ST, 2 * BLOCK_CH), bf16),
                        pltpu.VMEM((N_BLOCKS, 2 * BLOCK_CH, 2 * BLOCK_CH), bf16)]
                       + [pltpu.VMEM(params[p].shape, bf16) for p in CAST_PARAMS],
        compiler_params=pltpu.CompilerParams(vmem_limit_bytes=VMEM_LIMIT_BYTES),
        name="layer",
    )(x, *params)


def kernel(x, w_in, b_in, ssm_lambda_re, ssm_lambda_im, ssm_log_dt, ssm_b_re, ssm_b_im, ssm_c_re, ssm_c_im, ssm_d, glu_w, glu_b, w_ssm_out, conv_w, w_conv_out, w_o, ln1_g, ln1_b, w_gate, w_up, w_down, ln2_g, ln2_b):
    assert x.shape == (BATCH, SEQ, D_MODEL) and BATCH == SUBLANES
    for l in range(w_in.shape[0]):
        x = _layer(x, w_in[l], b_in[l], ssm_lambda_re[l], ssm_lambda_im[l], ssm_log_dt[l],
                   ssm_b_re[l], ssm_b_im[l], ssm_c_re[l], ssm_c_im[l], ssm_d[l], glu_w[l],
                   glu_b[l], w_ssm_out[l], conv_w[l], w_conv_out[l], w_o[l], ln1_g[l], ln1_b[l],
                   w_gate[l], w_up[l], w_down[l], ln2_g[l], ln2_b[l])
    return x
```

```python
import jax
import jax.numpy as jnp
from jax import lax
from jax.experimental import pallas as pl
from jax.experimental.pallas import tpu as pltpu

D_MODEL = 1024
BATCH = 8
SEQ = 4096
SSM_WIDTH = D_MODEL // 2
SSM_GROUP = 16
SSM_GROUPS = SSM_WIDTH // SSM_GROUP
SSM_STATE = 64
CONV_WIDTH = D_MODEL // 2
CONV_K = 3
FFN_HIDDEN = 2816
IN_COLS = SSM_WIDTH + 3 * CONV_WIDTH + 2 * D_MODEL
DEPTH = 1
ALPHA = (2.0 * DEPTH) ** 0.25
LN_EPS = 1e-5

SUBLANES = 8
VMEM_LIMIT_BYTES = 60 * 1024 * 1024

GROUPS_PER_BLOCK = 8
N_BLOCKS = SSM_GROUPS // GROUPS_PER_BLOCK
BLOCK_CH = GROUPS_PER_BLOCK * SSM_GROUP
BLOCK_ST = GROUPS_PER_BLOCK * SSM_STATE
N_STATE = SSM_GROUPS * SSM_STATE

PHASES = 2
T_TILE = 32
R_TILE = T_TILE * BATCH
H_TILE = T_TILE // PHASES
HALF = H_TILE * SUBLANES
N_TILES = SEQ // T_TILE
assert N_TILES >= 4 and SEQ % T_TILE == 0 and T_TILE % PHASES == 0

f32 = jnp.float32
bf16 = jnp.bfloat16


def _sigmoid(v):
    return 0.5 * jnp.tanh(0.5 * v) + 0.5


def _layer_norm(y, g, b):
    mu = jnp.mean(y, axis=-1, keepdims=True)
    var = jnp.mean(jnp.square(y - mu), axis=-1, keepdims=True)
    return (y - mu) * lax.rsqrt(var + LN_EPS) * g + b


def _tile_copies(hbm, buf, sem, tile, slot, to_vmem):
    copies = []
    for b in range(BATCH):
        h = hbm.at[b, pl.ds(tile * T_TILE, T_TILE), :]
        v = buf.at[slot, :, b, :]
        src, dst = (h, v) if to_vmem else (v, h)
        copies.append(pltpu.make_async_copy(src, dst, sem.at[slot]))
    return copies


def _tile_wait(buf, sem, slot):
    pltpu.make_async_copy(buf.at[slot], buf.at[slot], sem.at[slot]).wait()


def _split_phases(x3):
    x4 = x3.reshape(H_TILE, PHASES, SUBLANES, x3.shape[-1])
    return jnp.concatenate([x4[:, ph].reshape(HALF, x3.shape[-1]) for ph in range(PHASES)], axis=0)


def _merge_phases(y):
    parts = [y[ph * HALF:(ph + 1) * HALF].reshape(H_TILE, SUBLANES, y.shape[-1])
             for ph in range(PHASES)]
    return jnp.stack(parts, axis=1).reshape(T_TILE, SUBLANES, y.shape[-1])


def _layer_tile(pre1, x, mixer_refs, ssm_refs, ffn_refs, xs_re, xs_im, ve_ext, vo_ext):
    win_ref, small_ref = mixer_refs[:2]
    gluw_ref, wssm_ref, wconv_ref, wo_ref = mixer_refs[5:]
    bmat_ref, a2_ref, wre_ref, wim_ref, kd_ref = ssm_refs
    wg_ref, wu_ref, wd_ref = ffn_refs
    half_row = lambda r, k: small_ref[r:r + 1, k * SMALL_COLS // 2:(k + 1) * SMALL_COLS // 2]
    full_row = lambda r: small_ref[r:r + 1, :]
    ffn_on, mixer_on = pre1 is not None, x is not None
    dot = lambda a, w: jnp.dot(a, w, preferred_element_type=f32)
    if ffn_on:
        x1 = _layer_norm(pre1, full_row(ROW_LN1_G), full_row(ROW_LN1_B))
        x1b = x1.astype(bf16)
    if mixer_on:
        xb = x.astype(bf16)

        def proj(lo, hi):
            c0 = lo % SMALL_COLS
            return dot(xb, win_ref[:, lo:hi]) + small_ref[lo // SMALL_COLS:lo // SMALL_COLS + 1,
                                                          c0:c0 + hi - lo]

        u = proj(0, SSM_WIDTH)
        h = proj(SSM_WIDTH, SSM_WIDTH + CONV_WIDTH)
        ub = u.astype(bf16)
        u2 = []
        for j in range(N_BLOCKS):
            cs = slice(j * BLOCK_CH, (j + 1) * BLOCK_CH)
            u2.append(jnp.concatenate([ub[:HALF, cs], ub[HALF:, cs]], axis=-1))
            bu = dot(u2[j], bmat_ref[j])
            xs_re[SUBLANES:SUBLANES + HALF, j * BLOCK_ST:(j + 1) * BLOCK_ST] = bu[:, :BLOCK_ST]
            xs_im[SUBLANES:SUBLANES + HALF, j * BLOCK_ST:(j + 1) * BLOCK_ST] = bu[:, BLOCK_ST:]

        o4 = SSM_WIDTH + 3 * CONV_WIDTH
        c_gate = proj(SSM_WIDTH + CONV_WIDTH, SSM_WIDTH + 2 * CONV_WIDTH)
        b_gate = proj(SSM_WIDTH + 2 * CONV_WIDTH, o4)
        gate_a = proj(o4, o4 + D_MODEL)
        gate_b = proj(o4 + D_MODEL, o4 + 2 * D_MODEL)

        for j in range(N_BLOCKS):
            sl = slice(j * BLOCK_ST, (j + 1) * BLOCK_ST)
            ar = jnp.broadcast_to(a2_ref[0:1, sl], (SUBLANES, BLOCK_ST))
            ai = jnp.broadcast_to(a2_ref[1:2, sl], (SUBLANES, BLOCK_ST))
            sr, si = xs_re[0:SUBLANES, sl], xs_im[0:SUBLANES, sl]
            for k in range(H_TILE):
                r = slice((k + 1) * SUBLANES, (k + 2) * SUBLANES)
                sr, si = (ar * sr - ai * si + xs_re[r, sl], ar * si + ai * sr + xs_im[r, sl])
                xs_re[r, sl] = sr
                xs_im[r, sl] = si

    if ffn_on:
        gate = dot(x1b, wg_ref[...])

    if mixer_on:
        y_e, y_o = [], []
        for j in range(N_BLOCKS):
            sl = slice(j * BLOCK_ST, (j + 1) * BLOCK_ST)
            y2 = (dot(xs_re[0:HALF, sl].astype(bf16), wre_ref[j])
                  + dot(xs_im[0:HALF, sl].astype(bf16), wim_ref[j]) + dot(u2[j], kd_ref[j]))
            y_e.append(y2[:, :BLOCK_CH])
            y_o.append(y2[:, BLOCK_CH:])
        xs_re[0:SUBLANES, :] = xs_re[HALF:HALF + SUBLANES, :]
        xs_im[0:SUBLANES, :] = xs_im[HALF:HALF + SUBLANES, :]
        y_a = jnp.concatenate([jnp.concatenate(y_e, axis=-1), jnp.concatenate(y_o, axis=-1)], axis=0)
        y_a = y_a + half_row(ROW_DSKIP_GLUB, 0) * u

    if ffn_on:
        up = dot(x1b, wu_ref[...])

    if mixer_on:
        v = c_gate * h
        ve_ext[SUBLANES:SUBLANES + HALF, :] = v[:HALF]
        vo_ext[SUBLANES:SUBLANES + HALF, :] = v[HALF:]
        w0, w1, w2 = half_row(ROW_CONV, 0), half_row(ROW_CONV, 1), half_row(ROW_CONV + 1, 0)
        z_e = w0 * ve_ext[0:HALF, :] + w1 * vo_ext[0:HALF, :] + w2 * v[:HALF]
        z_o = w0 * vo_ext[0:HALF, :] + w1 * v[:HALF] + w2 * v[HALF:]
        ve_ext[0:SUBLANES, :] = ve_ext[HALF:HALF + SUBLANES, :]
        vo_ext[0:SUBLANES, :] = vo_ext[HALF:HALF + SUBLANES, :]
        bz = (b_gate * jnp.concatenate([z_e, z_o], axis=0)).astype(bf16)

        g = jax.nn.gelu(y_a)
        glu = dot(g.astype(bf16), gluw_ref[...]) + half_row(ROW_DSKIP_GLUB, 1)
        y_b = dot(bz, wconv_ref[...])

    if ffn_on:
        hid = (gate * _sigmoid(gate) * up).astype(bf16)
        ffn_lo = dot(hid, wd_ref[:, 0:D_MODEL // 2])
    if mixer_on:
        y_a = dot((g * _sigmoid(glu)).astype(bf16), wssm_ref[...])
    if ffn_on:
        ffn_hi = dot(hid, wd_ref[:, D_MODEL // 2:D_MODEL])
        ffn = jnp.concatenate([ffn_lo, ffn_hi], axis=-1)
    out = pre1_next = None
    if mixer_on:
        merged = _sigmoid(gate_a) * y_a + _sigmoid(gate_b) * y_b
        pre1_next = ALPHA * x + dot(merged.astype(bf16), wo_ref[...])
    if ffn_on:
        out = _layer_norm(ALPHA * x1 + ffn, full_row(ROW_LN2_G), full_row(ROW_LN2_B))
    return out, pre1_next


N_MIXER_PARAMS = 9
N_FFN_PARAMS = 3
CAST_PARAMS = (0, 5, 6, 7, 8, 9, 10, 11)
SMALL_COLS = D_MODEL
ROW_DSKIP_GLUB = IN_COLS // SMALL_COLS
ROW_CONV = ROW_DSKIP_GLUB + 1
ROW_LN1_G, ROW_LN1_B, ROW_LN2_G, ROW_LN2_B = (ROW_CONV + 2 + r for r in range(4))
SMALL_ROWS = 16
OUT_DMA_PRIORITY = 1
CAST_ROWS = 128


def _load_weights_bf16(hbm_refs, vmem_refs, stage, sem):
    width = stage.shape[-1]
    fills = []
    for w_hbm, w_vmem in zip(hbm_refs, vmem_refs):
        rows, cols = w_hbm.shape
        assert rows % CAST_ROWS == 0 and cols <= width
        per_fill = width // cols
        row_chunks = list(range(0, rows, CAST_ROWS))
        for f0 in range(0, len(row_chunks), per_fill):
            fills.append([(w_hbm, w_vmem, r0, cols, q * cols)
                          for q, r0 in enumerate(row_chunks[f0:f0 + per_fill])])

    def copies(f):
        return [pltpu.make_async_copy(w_hbm.at[pl.ds(r0, CAST_ROWS), :],
                                      stage.at[f % 2, :, pl.ds(lane0, cols)], sem.at[f % 2])
                for w_hbm, _, r0, cols, lane0 in fills[f]]

    for cp in copies(0):
        cp.start()
    for f in range(len(fills)):
        if f + 1 < len(fills):
            for cp in copies(f + 1):
                cp.start()
        for cp in copies(f):
            cp.wait()
        for _, w_vmem, r0, cols, lane0 in fills[f]:
            w_vmem[pl.ds(r0, CAST_ROWS), :] = stage[f % 2, :, lane0:lane0 + cols].astype(bf16)


def _layer_kernel(x_hbm, *refs):
    n_params = N_MIXER_PARAMS + N_FFN_PARAMS
    params = list(refs[:n_params])
    o_hbm, xbuf, xsem, obuf, osem, pre1_ref, xs_re, xs_im, ve_ext, vo_ext, stage, wsem = \
        refs[n_params:n_params + 12]
    ssm_refs = refs[n_params + 12:n_params + 17]
    w_vmem = refs[n_params + 17:]
    w_hbm = [params[p] for p in CAST_PARAMS]
    for p, w in zip(CAST_PARAMS, w_vmem):
        params[p] = w
    mixer_refs = params[:N_MIXER_PARAMS]
    ffn_refs = params[N_MIXER_PARAMS:]
    stage_refs = (mixer_refs, ssm_refs, ffn_refs, xs_re, xs_im, ve_ext, vo_ext)

    for cp in _tile_copies(x_hbm, xbuf, xsem, 0, 0, True):
        cp.start()
    zeros = jnp.zeros((SUBLANES, N_STATE), f32)
    xs_re[0:SUBLANES, :] = zeros
    xs_im[0:SUBLANES, :] = zeros
    ve_ext[0:SUBLANES, :] = jnp.zeros((SUBLANES, CONV_WIDTH), f32)
    vo_ext[0:SUBLANES, :] = jnp.zeros((SUBLANES, CONV_WIDTH), f32)
    _ssm_prep(*mixer_refs[2:5], *ssm_refs)
    _load_weights_bf16(w_hbm, w_vmem, stage, wsem)

    for cp in _tile_copies(x_hbm, xbuf, xsem, 1, 1, True):
        cp.start()
    _tile_wait(xbuf, xsem, 0)
    _, pre1 = _layer_tile(None, _split_phases(xbuf[0]), *stage_refs)
    pre1_ref[...] = pre1

    def step(i, carry):
        slot = i % 2
        ffn_slot = 1 - slot

        @pl.when(i + 1 < N_TILES)
        def _():
            for cp in _tile_copies(x_hbm, xbuf, xsem, i + 1, 1 - slot, True):
                cp.start()

        _tile_wait(xbuf, xsem, slot)

        @pl.when(i >= 3)
        def _():
            _tile_wait(obuf, osem, ffn_slot)

        y, pre1 = _layer_tile(pre1_ref[...], _split_phases(xbuf[slot]), *stage_refs)
        obuf[ffn_slot] = _merge_phases(y)
        pre1_ref[...] = pre1
        for cp in _tile_copies(o_hbm, obuf, osem, i - 1, ffn_slot, False):
            cp.start(priority=OUT_DMA_PRIORITY)
        return carry

    lax.fori_loop(1, N_TILES, step, 0)

    last_slot = (N_TILES - 1) % 2
    _tile_wait(obuf, osem, last_slot)
    y, _ = _layer_tile(pre1_ref[...], None, *stage_refs)
    obuf[last_slot] = _merge_phases(y)
    for cp in _tile_copies(o_hbm, obuf, osem, N_TILES - 1, last_slot, False):
        cp.start(priority=OUT_DMA_PRIORITY)
    _tile_wait(obuf, osem, 1 - last_slot)
    _tile_wait(obuf, osem, last_slot)


def _ssm_prep(vec_ref, b_ref, c_ref, bmat_ref, a2_ref, wre_ref, wim_ref, kd_ref):
    lam_re, lam_im = vec_ref[0:1, :], vec_ref[1:2, :]
    dt = jnp.exp(vec_ref[2:3, :])
    mag = jnp.exp(lam_re * dt)
    a_re = mag * jnp.cos(lam_im * dt)
    a_im = mag * jnp.sin(lam_im * dt)
    den = lam_re * lam_re + lam_im * lam_im
    num_re = a_re - 1.0
    fr = (num_re * lam_re + a_im * lam_im) / den
    fi = (a_im * lam_re - num_re * lam_im) / den
    a2_re = a_re * a_re - a_im * a_im
    a2_im = 2.0 * a_re * a_im
    a2_ref[0:1, :] = a2_re
    a2_ref[1:2, :] = a2_im

    def cmul(xr, xi, yr, yi):
        return xr * yr - xi * yi, xr * yi + xi * yr

    b_re, b_im, c_re, c_im = b_ref[0], b_ref[1], c_ref[0], c_ref[1]
    bb_re, bb_im = cmul(fr, fi, b_re, b_im)
    abb_re, abb_im = cmul(a_re, a_im, bb_re, bb_im)
    ca_re, ca_im = cmul(a_re, a_im, c_re, c_im)
    ca2_re, ca2_im = cmul(a2_re, a2_im, c_re, c_im)

    row_group = lax.broadcasted_iota(jnp.int32, (BLOCK_CH, BLOCK_ST), 0) // SSM_GROUP
    col_group = lax.broadcasted_iota(jnp.int32, (BLOCK_CH, BLOCK_ST), 1) // SSM_STATE
    diag = row_group == col_group

    for j in range(N_BLOCKS):
        def bd(m):
            blk = m[:, j * BLOCK_ST:(j + 1) * BLOCK_ST]
            return jnp.where(diag, jnp.concatenate([blk] * GROUPS_PER_BLOCK, axis=0), 0.0)

        def nt(x, y):
            return lax.dot_general(x, y, (((1,), (1,)), ((), ())), precision=lax.Precision.HIGHEST,
                                   preferred_element_type=f32)

        bbr, bbi, abr, abi = bd(bb_re), bd(bb_im), bd(abb_re), bd(abb_im)
        cr, ci = bd(c_re), bd(c_im)
        bmat_ref[j, 0:BLOCK_CH, 0:BLOCK_ST] = abr.astype(bf16)
        bmat_ref[j, 0:BLOCK_CH, BLOCK_ST:2 * BLOCK_ST] = abi.astype(bf16)
        bmat_ref[j, BLOCK_CH:2 * BLOCK_CH, 0:BLOCK_ST] = bbr.astype(bf16)
        bmat_ref[j, BLOCK_CH:2 * BLOCK_CH, BLOCK_ST:2 * BLOCK_ST] = bbi.astype(bf16)
        wre_ref[j, :, 0:BLOCK_CH] = bd(ca_re).T.astype(bf16)
        wre_ref[j, :, BLOCK_CH:2 * BLOCK_CH] = bd(ca2_re).T.astype(bf16)
        wim_ref[j, :, 0:BLOCK_CH] = (-bd(ca_im)).T.astype(bf16)
        wim_ref[j, :, BLOCK_CH:2 * BLOCK_CH] = (-bd(ca2_im)).T.astype(bf16)
        k0 = nt(bbr, cr) - nt(bbi, ci)
        k1 = nt(abr, cr) - nt(abi, ci)
        kd_ref[j, 0:BLOCK_CH, 0:BLOCK_CH] = k0.astype(bf16)
        kd_ref[j, 0:BLOCK_CH, BLOCK_CH:2 * BLOCK_CH] = k1.astype(bf16)
        kd_ref[j, BLOCK_CH:2 * BLOCK_CH, 0:BLOCK_CH] = jnp.zeros((BLOCK_CH, BLOCK_CH), bf16)
        kd_ref[j, BLOCK_CH:2 * BLOCK_CH, BLOCK_CH:2 * BLOCK_CH] = k0.astype(bf16)


def _ssm_inputs(lam_re, lam_im, log_dt, b_re, b_im, c_re, c_im):
    ldt = jnp.broadcast_to(log_dt[:, None], (SSM_GROUPS, SSM_STATE))
    vec = jnp.stack([lam_re, lam_im, ldt]).astype(f32).reshape(3, N_STATE)
    b_t = jnp.transpose(jnp.stack([b_re, b_im]).astype(f32), (0, 3, 1, 2))
    c_t = jnp.transpose(jnp.stack([c_re, c_im]).astype(f32), (0, 2, 1, 3))
    return vec, b_t.reshape(2, SSM_GROUP, N_STATE), c_t.reshape(2, SSM_GROUP, N_STATE)


def _layer(x, w_in, b_in, lam_re, lam_im, log_dt, b_re, b_im, c_re, c_im, d_skip, glu_w, glu_b,
           w_ssm_out, conv_w, w_conv_out, w_o, ln1_g, ln1_b, w_gate, w_up, w_down, ln2_g, ln2_b):
    w32 = lambda w: w.astype(f32)
    flat = lambda *vs: jnp.concatenate([v.astype(f32).reshape(-1) for v in vs])
    small = flat(b_in, d_skip, glu_b, conv_w, jnp.zeros((CONV_WIDTH,), f32), ln1_g, ln1_b, ln2_g, ln2_b)
    small = jnp.pad(small, (0, SMALL_ROWS * SMALL_COLS - small.size)).reshape(SMALL_ROWS, SMALL_COLS)
    mixer_in = (w32(w_in), small, *_ssm_inputs(lam_re, lam_im, log_dt, b_re, b_im, c_re, c_im),
                w32(glu_w), w32(w_ssm_out), w32(w_conv_out), w32(w_o))
    ffn_in = (w32(w_gate), w32(w_up), w32(w_down))
    assert len(mixer_in) == N_MIXER_PARAMS and len(ffn_in) == N_FFN_PARAMS
    params = mixer_in + ffn_in
    param_specs = [pl.BlockSpec(memory_space=pl.ANY if p in CAST_PARAMS else pltpu.VMEM)
                   for p, a in enumerate(params)]
    stage_cols = max(params[p].shape[1] for p in CAST_PARAMS)
    return pl.pallas_call(
        _layer_kernel,
        in_specs=[pl.BlockSpec(memory_space=pl.ANY)] + param_specs,
        out_specs=pl.BlockSpec(memory_space=pl.ANY),
        out_shape=jax.ShapeDtypeStruct((BATCH, SEQ, D_MODEL), f32),
        scratch_shapes=[pltpu.VMEM((2, T_TILE, BATCH, D_MODEL), f32),
                        pltpu.SemaphoreType.DMA((2,)),
                        pltpu.VMEM((2, T_TILE, BATCH, D_MODEL), f32),
                        pltpu.SemaphoreType.DMA((2,)),
                        pltpu.VMEM((R_TILE, D_MODEL), f32),
                        pltpu.VMEM((HALF + SUBLANES, N_STATE), f32),
                        pltpu.VMEM((HALF + SUBLANES, N_STATE), f32),
                        pltpu.VMEM((HALF + SUBLANES, CONV_WIDTH), f32),
                        pltpu.VMEM((HALF + SUBLANES, CONV_WIDTH), f32),
                        pltpu.VMEM((2, CAST_ROWS, stage_cols), f32),
                        pltpu.SemaphoreType.DMA((2,)),
                        pltpu.VMEM((N_BLOCKS, 2 * BLOCK_CH, 2 * BLOCK_ST), bf16),
                        pltpu.VMEM((2, N_STATE), f32),
                        pltpu.VMEM((N_BLOCKS, BLOCK_ST, 2 * BLOCK_CH), bf16),
                        pltpu.VMEM((N_BLOCKS, BLOCK_ST, 2 * BLOCK_CH), bf16),
                        pltpu.VMEM((N_BLOCKS, 2 * BLOCK_CH, 2 * BLOCK_CH), bf16)]
                       + [pltpu.VMEM(params[p].shape, bf16) for p in CAST_PARAMS],
        compiler_params=pltpu.CompilerParams(vmem_limit_bytes=VMEM_LIMIT_BYTES),
        name="layer",
    )(x, *params)


def kernel(x, w_in, b_in, ssm_lambda_re, ssm_lambda_im, ssm_log_dt, ssm_b_re, ssm_b_im, ssm_c_re, ssm_c_im, ssm_d, glu_w, glu_b, w_ssm_out, conv_w, w_conv_out, w_o, ln1_g, ln1_b, w_gate, w_up, w_down, ln2_g, ln2_b):
    assert x.shape == (BATCH, SEQ, D_MODEL) and BATCH == SUBLANES
    for l in range(w_in.shape[0]):
        x = _layer(x, w_in[l], b_in[l], ssm_lambda_re[l], ssm_lambda_im[l], ssm_log_dt[l],
                   ssm_b_re[l], ssm_b_im[l], ssm_c_re[l], ssm_c_im[l], ssm_d[l], glu_w[l],
                   glu_b[l], w_ssm_out[l], conv_w[l], w_conv_out[l], w_o[l], ln1_g[l], ln1_b[l],
                   w_gate[l], w_up[l], w_down[l], ln2_g[l], ln2_b[l])
    return x
```

```python
import jax
import jax.numpy as jnp
from jax import lax
from jax.experimental import pallas as pl
from jax.experimental.pallas import tpu as pltpu

D_MODEL = 1024
BATCH = 8
SEQ = 4096
SSM_WIDTH = D_MODEL // 2
SSM_GROUP = 16
SSM_GROUPS = SSM_WIDTH // SSM_GROUP
SSM_STATE = 64
CONV_WIDTH = D_MODEL // 2
CONV_K = 3
FFN_HIDDEN = 2816
IN_COLS = SSM_WIDTH + 3 * CONV_WIDTH + 2 * D_MODEL
DEPTH = 1
ALPHA = (2.0 * DEPTH) ** 0.25
LN_EPS = 1e-5

SUBLANES = 8
VMEM_LIMIT_BYTES = 60 * 1024 * 1024

GROUPS_PER_BLOCK = 8
N_BLOCKS = SSM_GROUPS // GROUPS_PER_BLOCK
BLOCK_CH = GROUPS_PER_BLOCK * SSM_GROUP
BLOCK_ST = GROUPS_PER_BLOCK * SSM_STATE
N_STATE = SSM_GROUPS * SSM_STATE

PHASES = 2
T_TILE = 32
R_TILE = T_TILE * BATCH
H_TILE = T_TILE // PHASES
HALF = H_TILE * SUBLANES
N_TILES = SEQ // T_TILE
assert N_TILES >= 4 and SEQ % T_TILE == 0 and T_TILE % PHASES == 0

f32 = jnp.float32
bf16 = jnp.bfloat16


def _sigmoid(v):
    return 0.5 * jnp.tanh((0.5 * v).astype(bf16)).astype(f32) + 0.5


def _layer_norm(y, g, b):
    mu = jnp.mean(y, axis=-1, keepdims=True)
    var = jnp.mean(jnp.square(y - mu), axis=-1, keepdims=True)
    return (y - mu) * lax.rsqrt(var + LN_EPS) * g + b


def _tile_copies(hbm, buf, sem, tile, slot, to_vmem):
    copies = []
    for b in range(BATCH):
        h = hbm.at[b, pl.ds(tile * T_TILE, T_TILE), :]
        v = buf.at[slot, :, b, :]
        src, dst = (h, v) if to_vmem else (v, h)
        copies.append(pltpu.make_async_copy(src, dst, sem.at[slot]))
    return copies


def _tile_wait(buf, sem, slot):
    pltpu.make_async_copy(buf.at[slot], buf.at[slot], sem.at[slot]).wait()


def _split_phases(x3):
    x4 = x3.reshape(H_TILE, PHASES, SUBLANES, x3.shape[-1])
    return jnp.concatenate([x4[:, ph].reshape(HALF, x3.shape[-1]) for ph in range(PHASES)], axis=0)


def _merge_phases(y):
    parts = [y[ph * HALF:(ph + 1) * HALF].reshape(H_TILE, SUBLANES, y.shape[-1])
             for ph in range(PHASES)]
    return jnp.stack(parts, axis=1).reshape(T_TILE, SUBLANES, y.shape[-1])


def _layer_tile(pre1, x, mixer_refs, ssm_refs, ffn_refs, xs_re, xs_im, ve_ext, vo_ext):
    win_ref, small_ref = mixer_refs[:2]
    gluw_ref, wssm_ref, wconv_ref, wo_ref = mixer_refs[5:]
    bmat_ref, a2_ref, wre_ref, wim_ref, kd_ref = ssm_refs
    wg_ref, wu_ref, wd_ref = ffn_refs
    half_row = lambda r, k: small_ref[r:r + 1, k * SMALL_COLS // 2:(k + 1) * SMALL_COLS // 2]
    full_row = lambda r: small_ref[r:r + 1, :]
    ffn_on, mixer_on = pre1 is not None, x is not None
    dot = lambda a, w: jnp.dot(a, w, preferred_element_type=f32)
    if ffn_on:
        x1 = _layer_norm(pre1, full_row(ROW_LN1_G), full_row(ROW_LN1_B))
        x1b = x1.astype(bf16)
    if mixer_on:
        xb = x.astype(bf16)

        def proj(lo, hi):
            c0 = lo % SMALL_COLS
            return dot(xb, win_ref[:, lo:hi]) + small_ref[lo // SMALL_COLS:lo // SMALL_COLS + 1,
                                                          c0:c0 + hi - lo]

        u = proj(0, SSM_WIDTH)
        h = proj(SSM_WIDTH, SSM_WIDTH + CONV_WIDTH)
        ub = u.astype(bf16)
        u2 = []
        for j in range(N_BLOCKS):
            cs = slice(j * BLOCK_CH, (j + 1) * BLOCK_CH)
            u2.append(jnp.concatenate([ub[:HALF, cs], ub[HALF:, cs]], axis=-1))
            bu = dot(u2[j], bmat_ref[j])
            xs_re[SUBLANES:SUBLANES + HALF, j * BLOCK_ST:(j + 1) * BLOCK_ST] = bu[:, :BLOCK_ST]
            xs_im[SUBLANES:SUBLANES + HALF, j * BLOCK_ST:(j + 1) * BLOCK_ST] = bu[:, BLOCK_ST:]

        o4 = SSM_WIDTH + 3 * CONV_WIDTH
        c_gate = proj(SSM_WIDTH + CONV_WIDTH, SSM_WIDTH + 2 * CONV_WIDTH)
        b_gate = proj(SSM_WIDTH + 2 * CONV_WIDTH, o4)
        gate_a = proj(o4, o4 + D_MODEL)
        gate_b = proj(o4 + D_MODEL, o4 + 2 * D_MODEL)

        for j in range(N_BLOCKS):
            sl = slice(j * BLOCK_ST, (j + 1) * BLOCK_ST)
            ar = jnp.broadcast_to(a2_ref[0:1, sl], (SUBLANES, BLOCK_ST))
            ai = jnp.broadcast_to(a2_ref[1:2, sl], (SUBLANES, BLOCK_ST))
            sr, si = xs_re[0:SUBLANES, sl], xs_im[0:SUBLANES, sl]
            for k in range(H_TILE):
                r = slice((k + 1) * SUBLANES, (k + 2) * SUBLANES)
                sr, si = (ar * sr - ai * si + xs_re[r, sl], ar * si + ai * sr + xs_im[r, sl])
                xs_re[r, sl] = sr
                xs_im[r, sl] = si

    if ffn_on:
        gate = dot(x1b, wg_ref[...])

    if mixer_on:
        y_e, y_o = [], []
        for j in range(N_BLOCKS):
            sl = slice(j * BLOCK_ST, (j + 1) * BLOCK_ST)
            y2 = (dot(xs_re[0:HALF, sl].astype(bf16), wre_ref[j])
                  + dot(xs_im[0:HALF, sl].astype(bf16), wim_ref[j]) + dot(u2[j], kd_ref[j]))
            y_e.append(y2[:, :BLOCK_CH])
            y_o.append(y2[:, BLOCK_CH:])
        xs_re[0:SUBLANES, :] = xs_re[HALF:HALF + SUBLANES, :]
        xs_im[0:SUBLANES, :] = xs_im[HALF:HALF + SUBLANES, :]
        y_a = jnp.concatenate([jnp.concatenate(y_e, axis=-1), jnp.concatenate(y_o, axis=-1)], axis=0)
        y_a = y_a + half_row(ROW_DSKIP_GLUB, 0) * u

    if ffn_on:
        up = dot(x1b, wu_ref[...])

    if mixer_on:
        v = c_gate * h
        ve_ext[SUBLANES:SUBLANES + HALF, :] = v[:HALF]
        vo_ext[SUBLANES:SUBLANES + HALF, :] = v[HALF:]
        w0, w1, w2 = half_row(ROW_CONV, 0), half_row(ROW_CONV, 1), half_row(ROW_CONV + 1, 0)
        z_e = w0 * ve_ext[0:HALF, :] + w1 * vo_ext[0:HALF, :] + w2 * v[:HALF]
        z_o = w0 * vo_ext[0:HALF, :] + w1 * v[:HALF] + w2 * v[HALF:]
        ve_ext[0:SUBLANES, :] = ve_ext[HALF:HALF + SUBLANES, :]
        vo_ext[0:SUBLANES, :] = vo_ext[HALF:HALF + SUBLANES, :]
        bz = (b_gate * jnp.concatenate([z_e, z_o], axis=0)).astype(bf16)

        g = jax.nn.gelu(y_a)
        glu = dot(g.astype(bf16), gluw_ref[...]) + half_row(ROW_DSKIP_GLUB, 1)
        y_b = dot(bz, wconv_ref[...])

    if ffn_on:
        hid = (gate * _sigmoid(gate) * up).astype(bf16)
        ffn_lo = dot(hid, wd_ref[:, 0:D_MODEL // 2])
    if mixer_on:
        y_a = dot((g * _sigmoid(glu)).astype(bf16), wssm_ref[...])
    if ffn_on:
        ffn_hi = dot(hid, wd_ref[:, D_MODEL // 2:D_MODEL])
        ffn = jnp.concatenate([ffn_lo, ffn_hi], axis=-1)
    out = pre1_next = None
    if mixer_on:
        merged = _sigmoid(gate_a) * y_a + _sigmoid(gate_b) * y_b
        pre1_next = ALPHA * x + dot(merged.astype(bf16), wo_ref[...])
    if ffn_on:
        out = _layer_norm(ALPHA * x1 + ffn, full_row(ROW_LN2_G), full_row(ROW_LN2_B))
    return out, pre1_next


N_MIXER_PARAMS = 9
N_FFN_PARAMS = 3
CAST_PARAMS = (0, 5, 6, 7, 8, 9, 10, 11)
SMALL_COLS = D_MODEL
ROW_DSKIP_GLUB = IN_COLS // SMALL_COLS
ROW_CONV = ROW_DSKIP_GLUB + 1
ROW_LN1_G, ROW_LN1_B, ROW_LN2_G, ROW_LN2_B = (ROW_CONV + 2 + r for r in range(4))
SMALL_ROWS = 16
OUT_DMA_PRIORITY = 1
CAST_ROWS = 128


def _load_weights_bf16(hbm_refs, vmem_refs, stage, sem):
    width = stage.shape[-1]
    fills = []
    for w_hbm, w_vmem in zip(hbm_refs, vmem_refs):
        rows, cols = w_hbm.shape
        assert rows % CAST_ROWS == 0 and cols <= width
        per_fill = width // cols
        row_chunks = list(range(0, rows, CAST_ROWS))
        for f0 in range(0, len(row_chunks), per_fill):
            fills.append([(w_hbm, w_vmem, r0, cols, q * cols)
                          for q, r0 in enumerate(row_chunks[f0:f0 + per_fill])])

    def copies(f):
        return [pltpu.make_async_copy(w_hbm.at[pl.ds(r0, CAST_ROWS), :],
                                      stage.at[f % 2, :, pl.ds(lane0, cols)], sem.at[f % 2])
                for w_hbm, _, r0, cols, lane0 in fills[f]]

    for cp in copies(0):
        cp.start()
    for f in range(len(fills)):
        if f + 1 < len(fills):
            for cp in copies(f + 1):
                cp.start()
        for cp in copies(f):
            cp.wait()
        for _, w_vmem, r0, cols, lane0 in fills[f]:
            w_vmem[pl.ds(r0, CAST_ROWS), :] = stage[f % 2, :, lane0:lane0 + cols].astype(bf16)


def _layer_kernel(x_hbm, *refs):
    n_params = N_MIXER_PARAMS + N_FFN_PARAMS
    params = list(refs[:n_params])
    o_hbm, xbuf, xsem, obuf, osem, pre1_ref, xs_re, xs_im, ve_ext, vo_ext, stage, wsem = \
        refs[n_params:n_params + 12]
    ssm_refs = refs[n_params + 12:n_params + 17]
    w_vmem = refs[n_params + 17:]
    w_hbm = [params[p] for p in CAST_PARAMS]
    for p, w in zip(CAST_PARAMS, w_vmem):
        params[p] = w
    mixer_refs = params[:N_MIXER_PARAMS]
    ffn_refs = params[N_MIXER_PARAMS:]
    stage_refs = (mixer_refs, ssm_refs, ffn_refs, xs_re, xs_im, ve_ext, vo_ext)

    for cp in _tile_copies(x_hbm, xbuf, xsem, 0, 0, True):
        cp.start()
    zeros = jnp.zeros((SUBLANES, N_STATE), f32)
    xs_re[0:SUBLANES, :] = zeros
    xs_im[0:SUBLANES, :] = zeros
    ve_ext[0:SUBLANES, :] = jnp.zeros((SUBLANES, CONV_WIDTH), f32)
    vo_ext[0:SUBLANES, :] = jnp.zeros((SUBLANES, CONV_WIDTH), f32)
    _ssm_prep(*mixer_refs[2:5], *ssm_refs)
    _load_weights_bf16(w_hbm, w_vmem, stage, wsem)

    for cp in _tile_copies(x_hbm, xbuf, xsem, 1, 1, True):
        cp.start()
    _tile_wait(xbuf, xsem, 0)
    _, pre1 = _layer_tile(None, _split_phases(xbuf[0]), *stage_refs)
    pre1_ref[...] = pre1

    def step(i, carry):
        slot = i % 2
        ffn_slot = 1 - slot

        @pl.when(i + 1 < N_TILES)
        def _():
            for cp in _tile_copies(x_hbm, xbuf, xsem, i + 1, 1 - slot, True):
                cp.start()

        _tile_wait(xbuf, xsem, slot)

        @pl.when(i >= 3)
        def _():
            _tile_wait(obuf, osem, ffn_slot)

        y, pre1 = _layer_tile(pre1_ref[...], _split_phases(xbuf[slot]), *stage_refs)
        obuf[ffn_slot] = _merge_phases(y)
        pre1_ref[...] = pre1
        for cp in _tile_copies(o_hbm, obuf, osem, i - 1, ffn_slot, False):
            cp.start(priority=OUT_DMA_PRIORITY)
        return carry

    lax.fori_loop(1, N_TILES, step, 0)

    last_slot = (N_TILES - 1) % 2
    _tile_wait(obuf, osem, last_slot)
    y, _ = _layer_tile(pre1_ref[...], None, *stage_refs)
    obuf[last_slot] = _merge_phases(y)
    for cp in _tile_copies(o_hbm, obuf, osem, N_TILES - 1, last_slot, False):
        cp.start(priority=OUT_DMA_PRIORITY)
    _tile_wait(obuf, osem, 1 - last_slot)
    _tile_wait(obuf, osem, last_slot)


def _ssm_prep(vec_ref, b_ref, c_ref, bmat_ref, a2_ref, wre_ref, wim_ref, kd_ref):
    lam_re, lam_im = vec_ref[0:1, :], vec_ref[1:2, :]
    dt = jnp.exp(vec_ref[2:3, :])
    mag = jnp.exp(lam_re * dt)
    a_re = mag * jnp.cos(lam_im * dt)
    a_im = mag * jnp.sin(lam_im * dt)
    den = lam_re * lam_re + lam_im * lam_im
    num_re = a_re - 1.0
    fr = (num_re * lam_re + a_im * lam_im) / den
    fi = (a_im * lam_re - num_re * lam_im) / den
    a2_re = a_re * a_re - a_im * a_im
    a2_im = 2.0 * a_re * a_im
    a2_ref[0:1, :] = a2_re
    a2_ref[1:2, :] = a2_im

    def cmul(xr, xi, yr, yi):
        return xr * yr - xi * yi, xr * yi + xi * yr

    b_re, b_im, c_re, c_im = b_ref[0], b_ref[1], c_ref[0], c_ref[1]
    bb_re, bb_im = cmul(fr, fi, b_re, b_im)
    abb_re, abb_im = cmul(a_re, a_im, bb_re, bb_im)
    ca_re, ca_im = cmul(a_re, a_im, c_re, c_im)
    ca2_re, ca2_im = cmul(a2_re, a2_im, c_re, c_im)

    row_group = lax.broadcasted_iota(jnp.int32, (BLOCK_CH, BLOCK_ST), 0) // SSM_GROUP
    col_group = lax.broadcasted_iota(jnp.int32, (BLOCK_CH, BLOCK_ST), 1) // SSM_STATE
    diag = row_group == col_group

    for j in range(N_BLOCKS):
        def bd(m):
            blk = m[:, j * BLOCK_ST:(j + 1) * BLOCK_ST]
            return jnp.where(diag, jnp.concatenate([blk] * GROUPS_PER_BLOCK, axis=0), 0.0)

        def nt(x, y):
            return lax.dot_general(x, y, (((1,), (1,)), ((), ())), precision=lax.Precision.HIGHEST,
                                   preferred_element_type=f32)

        bbr, bbi, abr, abi = bd(bb_re), bd(bb_im), bd(abb_re), bd(abb_im)
        cr, ci = bd(c_re), bd(c_im)
        bmat_ref[j, 0:BLOCK_CH, 0:BLOCK_ST] = abr.astype(bf16)
        bmat_ref[j, 0:BLOCK_CH, BLOCK_ST:2 * BLOCK_ST] = abi.astype(bf16)
        bmat_ref[j, BLOCK_CH:2 * BLOCK_CH, 0:BLOCK_ST] = bbr.astype(bf16)
        bmat_ref[j, BLOCK_CH:2 * BLOCK_CH, BLOCK_ST:2 * BLOCK_ST] = bbi.astype(bf16)
        wre_ref[j, :, 0:BLOCK_CH] = bd(ca_re).T.astype(bf16)
        wre_ref[j, :, BLOCK_CH:2 * BLOCK_CH] = bd(ca2_re).T.astype(bf16)
        wim_ref[j, :, 0:BLOCK_CH] = (-bd(ca_im)).T.astype(bf16)
        wim_ref[j, :, BLOCK_CH:2 * BLOCK_CH] = (-bd(ca2_im)).T.astype(bf16)
        k0 = nt(bbr, cr) - nt(bbi, ci)
        k1 = nt(abr, cr) - nt(abi, ci)
        kd_ref[j, 0:BLOCK_CH, 0:BLOCK_CH] = k0.astype(bf16)
        kd_ref[j, 0:BLOCK_CH, BLOCK_CH:2 * BLOCK_CH] = k1.astype(bf16)
        kd_ref[j, BLOCK_CH:2 * BLOCK_CH, 0:BLOCK_CH] = jnp.zeros((BLOCK_CH, BLOCK_CH), bf16)
        kd_ref[j, BLOCK_CH:2 * BLOCK_CH, BLOCK_CH:2 * BLOCK_CH] = k0.astype(bf16)


def _ssm_inputs(lam_re, lam_im, log_dt, b_re, b_im, c_re, c_im):
    ldt = jnp.broadcast_to(log_dt[:, None], (SSM_GROUPS, SSM_STATE))
    vec = jnp.stack([lam_re, lam_im, ldt]).astype(f32).reshape(3, N_STATE)
    b_t = jnp.transpose(jnp.stack([b_re, b_im]).astype(f32), (0, 3, 1, 2))
    c_t = jnp.transpose(jnp.stack([c_re, c_im]).astype(f32), (0, 2, 1, 3))
    return vec, b_t.reshape(2, SSM_GROUP, N_STATE), c_t.reshape(2, SSM_GROUP, N_STATE)


def _layer(x, w_in, b_in, lam_re, lam_im, log_dt, b_re, b_im, c_re, c_im, d_skip, glu_w, glu_b,
           w_ssm_out, conv_w, w_conv_out, w_o, ln1_g, ln1_b, w_gate, w_up, w_down, ln2_g, ln2_b):
    w32 = lambda w: w.astype(f32)
    flat = lambda *vs: jnp.concatenate([v.astype(f32).reshape(-1) for v in vs])
    small = flat(b_in, d_skip, glu_b, conv_w, jnp.zeros((CONV_WIDTH,), f32), ln1_g, ln1_b, ln2_g, ln2_b)
    small = jnp.pad(small, (0, SMALL_ROWS * SMALL_COLS - small.size)).reshape(SMALL_ROWS, SMALL_COLS)
    mixer_in = (w32(w_in), small, *_ssm_inputs(lam_re, lam_im, log_dt, b_re, b_im, c_re, c_im),
                w32(glu_w), w32(w_ssm_out), w32(w_conv_out), w32(w_o))
    ffn_in = (w32(w_gate), w32(w_up), w32(w_down))
    assert len(mixer_in) == N_MIXER_PARAMS and len(ffn_in) == N_FFN_PARAMS
    params = mixer_in + ffn_in
    param_specs = [pl.BlockSpec(memory_space=pl.ANY if p in CAST_PARAMS else pltpu.VMEM)
                   for p, a in enumerate(params)]
    stage_cols = max(params[p].shape[1] for p in CAST_PARAMS)
    return pl.pallas_call(
        _layer_kernel,
        in_specs=[pl.BlockSpec(memory_space=pl.ANY)] + param_specs,
        out_specs=pl.BlockSpec(memory_space=pl.ANY),
        out_shape=jax.ShapeDtypeStruct((BATCH, SEQ, D_MODEL), f32),
        scratch_shapes=[pltpu.VMEM((2, T_TILE, BATCH, D_MODEL), f32),
                        pltpu.SemaphoreType.DMA((2,)),
                        pltpu.VMEM((2, T_TILE, BATCH, D_MODEL), f32),
                        pltpu.SemaphoreType.DMA((2,)),
                        pltpu.VMEM((R_TILE, D_MODEL), f32),
                        pltpu.VMEM((HALF + SUBLANES, N_STATE), f32),
                        pltpu.VMEM((HALF + SUBLANES, N_STATE), f32),
                        pltpu.VMEM((HALF + SUBLANES, CONV_WIDTH), f32),
                        pltpu.VMEM((HALF + SUBLANES, CONV_WIDTH), f32),
                        pltpu.VMEM((2, CAST_ROWS, stage_cols), f32),
                        pltpu.SemaphoreType.DMA((2,)),
                        pltpu.VMEM((N_BLOCKS, 2 * BLOCK_CH, 2 * BLOCK_ST), bf16),
                        pltpu.VMEM((2, N_STATE), f32),
                        pltpu.VMEM((N_BLOCKS, BLOCK_ST, 2 * BLOCK_CH), bf16),
                        pltpu.VMEM((N_BLOCKS, BLOCK_ST, 2 * BLOCK_CH), bf16),
                        pltpu.VMEM((N_BLOCKS, 2 * BLOCK_CH, 2 * BLOCK_CH), bf16)]
                       + [pltpu.VMEM(params[p].shape, bf16) for p in CAST_PARAMS],
        compiler_params=pltpu.CompilerParams(vmem_limit_bytes=VMEM_LIMIT_BYTES),
        name="layer",
    )(x, *params)


def kernel(x, w_in, b_in, ssm_lambda_re, ssm_lambda_im, ssm_log_dt, ssm_b_re, ssm_b_im, ssm_c_re, ssm_c_im, ssm_d, glu_w, glu_b, w_ssm_out, conv_w, w_conv_out, w_o, ln1_g, ln1_b, w_gate, w_up, w_down, ln2_g, ln2_b):
    assert x.shape == (BATCH, SEQ, D_MODEL) and BATCH == SUBLANES
    for l in range(w_in.shape[0]):
        x = _layer(x, w_in[l], b_in[l], ssm_lambda_re[l], ssm_lambda_im[l], ssm_log_dt[l],
                   ssm_b_re[l], ssm_b_im[l], ssm_c_re[l], ssm_c_im[l], ssm_d[l], glu_w[l],
                   glu_b[l], w_ssm_out[l], conv_w[l], w_conv_out[l], w_o[l], ln1_g[l], ln1_b[l],
                   w_gate[l], w_up[l], w_down[l], ln2_g[l], ln2_b[l])
    return x
```

```python
import jax
import jax.numpy as jnp
from jax import lax
from jax.experimental import pallas as pl
from jax.experimental.pallas import tpu as pltpu

D_MODEL = 1024
BATCH = 8
SEQ = 4096
SSM_WIDTH = D_MODEL // 2
SSM_GROUP = 16
SSM_GROUPS = SSM_WIDTH // SSM_GROUP
SSM_STATE = 64
CONV_WIDTH = D_MODEL // 2
CONV_K = 3
FFN_HIDDEN = 2816
F_SPLIT = 1536
IN_COLS = SSM_WIDTH + 3 * CONV_WIDTH + 2 * D_MODEL
DEPTH = 1
ALPHA = (2.0 * DEPTH) ** 0.25
LN_EPS = 1e-5

SUBLANES = 8
VMEM_LIMIT_BYTES = 60 * 1024 * 1024

GROUPS_PER_BLOCK = 8
N_BLOCKS = SSM_GROUPS // GROUPS_PER_BLOCK
BLOCK_CH = GROUPS_PER_BLOCK * SSM_GROUP
BLOCK_ST = GROUPS_PER_BLOCK * SSM_STATE
N_STATE = SSM_GROUPS * SSM_STATE

PHASES = 2
T_TILE = 32
R_TILE = T_TILE * BATCH
H_TILE = T_TILE // PHASES
HALF = H_TILE * SUBLANES
N_TILES = SEQ // T_TILE
assert N_TILES >= 4 and SEQ % T_TILE == 0 and T_TILE % PHASES == 0

f32 = jnp.float32
bf16 = jnp.bfloat16


def _sigmoid(v):
    return 0.5 * jnp.tanh(0.5 * v) + 0.5


def _layer_norm(y, g, b):
    mu = jnp.mean(y, axis=-1, keepdims=True)
    var = jnp.mean(jnp.square(y - mu), axis=-1, keepdims=True)
    return (y - mu) * lax.rsqrt(var + LN_EPS) * g + b


def _tile_copies(hbm, buf, sem, tile, slot, to_vmem):
    copies = []
    for b in range(BATCH):
        h = hbm.at[b, pl.ds(tile * T_TILE, T_TILE), :]
        v = buf.at[slot, :, b, :]
        src, dst = (h, v) if to_vmem else (v, h)
        copies.append(pltpu.make_async_copy(src, dst, sem.at[slot]))
    return copies


def _tile_wait(buf, sem, slot):
    pltpu.make_async_copy(buf.at[slot], buf.at[slot], sem.at[slot]).wait()


def _split_phases(x3):
    x4 = x3.reshape(H_TILE, PHASES, SUBLANES, x3.shape[-1])
    return jnp.concatenate([x4[:, ph].reshape(HALF, x3.shape[-1]) for ph in range(PHASES)], axis=0)


def _merge_phases(y):
    parts = [y[ph * HALF:(ph + 1) * HALF].reshape(H_TILE, SUBLANES, y.shape[-1])
             for ph in range(PHASES)]
    return jnp.stack(parts, axis=1).reshape(T_TILE, SUBLANES, y.shape[-1])


def _layer_tile(pre1, x, mixer_refs, ssm_refs, ffn_refs, xs_re, xs_im, ve_ext, vo_ext):
    win_ref, small_ref = mixer_refs[:2]
    gluw_ref, wssm_ref, wconv_ref, wo_ref = mixer_refs[5:]
    bmat_ref, a2_ref, wre_ref, wim_ref, kd_ref = ssm_refs
    wg_ref, wu_ref, wd_ref = ffn_refs
    half_row = lambda r, k: small_ref[r:r + 1, k * SMALL_COLS // 2:(k + 1) * SMALL_COLS // 2]
    full_row = lambda r: small_ref[r:r + 1, :]
    ffn_on, mixer_on = pre1 is not None, x is not None
    dot = lambda a, w: jnp.dot(a, w, preferred_element_type=f32)
    if ffn_on:
        x1 = _layer_norm(pre1, full_row(ROW_LN1_G), full_row(ROW_LN1_B))
        x1b = x1.astype(bf16)
    if mixer_on:
        xb = x.astype(bf16)

        def proj(lo, hi):
            c0 = lo % SMALL_COLS
            return dot(xb, win_ref[:, lo:hi]) + small_ref[lo // SMALL_COLS:lo // SMALL_COLS + 1,
                                                          c0:c0 + hi - lo]

        u = proj(0, SSM_WIDTH)
        h = proj(SSM_WIDTH, SSM_WIDTH + CONV_WIDTH)
        ub = u.astype(bf16)
        u2 = []
        for j in range(N_BLOCKS):
            cs = slice(j * BLOCK_CH, (j + 1) * BLOCK_CH)
            u2.append(jnp.concatenate([ub[:HALF, cs], ub[HALF:, cs]], axis=-1))
            bu = dot(u2[j], bmat_ref[j])
            xs_re[SUBLANES:SUBLANES + HALF, j * BLOCK_ST:(j + 1) * BLOCK_ST] = bu[:, :BLOCK_ST]
            xs_im[SUBLANES:SUBLANES + HALF, j * BLOCK_ST:(j + 1) * BLOCK_ST] = bu[:, BLOCK_ST:]

        o4 = SSM_WIDTH + 3 * CONV_WIDTH
        c_gate = proj(SSM_WIDTH + CONV_WIDTH, SSM_WIDTH + 2 * CONV_WIDTH)
        b_gate = proj(SSM_WIDTH + 2 * CONV_WIDTH, o4)
        gate_a = proj(o4, o4 + D_MODEL)
        gate_b = proj(o4 + D_MODEL, o4 + 2 * D_MODEL)

        for j in range(N_BLOCKS):
            sl = slice(j * BLOCK_ST, (j + 1) * BLOCK_ST)
            ar = jnp.broadcast_to(a2_ref[0:1, sl], (SUBLANES, BLOCK_ST))
            ai = jnp.broadcast_to(a2_ref[1:2, sl], (SUBLANES, BLOCK_ST))
            sr, si = xs_re[0:SUBLANES, sl], xs_im[0:SUBLANES, sl]
            for k in range(H_TILE):
                r = slice((k + 1) * SUBLANES, (k + 2) * SUBLANES)
                sr, si = (ar * sr - ai * si + xs_re[r, sl], ar * si + ai * sr + xs_im[r, sl])
                xs_re[r, sl] = sr
                xs_im[r, sl] = si

    if ffn_on:
        gate_a_ = dot(x1b, wg_ref[:, 0:F_SPLIT])
        up_a_ = dot(x1b, wu_ref[:, 0:F_SPLIT])

    if mixer_on:
        y_e, y_o = [], []
        for j in range(N_BLOCKS):
            sl = slice(j * BLOCK_ST, (j + 1) * BLOCK_ST)
            y2 = (dot(xs_re[0:HALF, sl].astype(bf16), wre_ref[j])
                  + dot(xs_im[0:HALF, sl].astype(bf16), wim_ref[j]) + dot(u2[j], kd_ref[j]))
            y_e.append(y2[:, :BLOCK_CH])
            y_o.append(y2[:, BLOCK_CH:])
        xs_re[0:SUBLANES, :] = xs_re[HALF:HALF + SUBLANES, :]
        xs_im[0:SUBLANES, :] = xs_im[HALF:HALF + SUBLANES, :]
        y_a = jnp.concatenate([jnp.concatenate(y_e, axis=-1), jnp.concatenate(y_o, axis=-1)], axis=0)
        y_a = y_a + half_row(ROW_DSKIP_GLUB, 0) * u

    if ffn_on:
        gate_b_ = dot(x1b, wg_ref[:, F_SPLIT:FFN_HIDDEN])
        up_b_ = dot(x1b, wu_ref[:, F_SPLIT:FFN_HIDDEN])
        hid_a = (gate_a_ * _sigmoid(gate_a_) * up_a_).astype(bf16)

    if mixer_on:
        v = c_gate * h
        ve_ext[SUBLANES:SUBLANES + HALF, :] = v[:HALF]
        vo_ext[SUBLANES:SUBLANES + HALF, :] = v[HALF:]
        w0, w1, w2 = half_row(ROW_CONV, 0), half_row(ROW_CONV, 1), half_row(ROW_CONV + 1, 0)
        z_e = w0 * ve_ext[0:HALF, :] + w1 * vo_ext[0:HALF, :] + w2 * v[:HALF]
        z_o = w0 * vo_ext[0:HALF, :] + w1 * v[:HALF] + w2 * v[HALF:]
        ve_ext[0:SUBLANES, :] = ve_ext[HALF:HALF + SUBLANES, :]
        vo_ext[0:SUBLANES, :] = vo_ext[HALF:HALF + SUBLANES, :]
        bz = (b_gate * jnp.concatenate([z_e, z_o], axis=0)).astype(bf16)

        g = jax.nn.gelu(y_a)
        glu = dot(g.astype(bf16), gluw_ref[...]) + half_row(ROW_DSKIP_GLUB, 1)
        y_b = dot(bz, wconv_ref[...])

    if ffn_on:
        hid_b = (gate_b_ * _sigmoid(gate_b_) * up_b_).astype(bf16)
        ffn_lo = dot(hid_a, wd_ref[0:F_SPLIT, :])
    if mixer_on:
        y_a = dot((g * _sigmoid(glu)).astype(bf16), wssm_ref[...])
    if ffn_on:
        ffn = ffn_lo + dot(hid_b, wd_ref[F_SPLIT:FFN_HIDDEN, :])
    out = pre1_next = None
    if mixer_on:
        merged = _sigmoid(gate_a) * y_a + _sigmoid(gate_b) * y_b
        pre1_next = ALPHA * x + dot(merged.astype(bf16), wo_ref[...])
    if ffn_on:
        out = _layer_norm(ALPHA * x1 + ffn, full_row(ROW_LN2_G), full_row(ROW_LN2_B))
    return out, pre1_next


N_MIXER_PARAMS = 9
N_FFN_PARAMS = 3
CAST_PARAMS = (0, 5, 6, 7, 8, 9, 10, 11)
SMALL_COLS = D_MODEL
ROW_DSKIP_GLUB = IN_COLS // SMALL_COLS
ROW_CONV = ROW_DSKIP_GLUB + 1
ROW_LN1_G, ROW_LN1_B, ROW_LN2_G, ROW_LN2_B = (ROW_CONV + 2 + r for r in range(4))
SMALL_ROWS = 16
OUT_DMA_PRIORITY = 1
CAST_ROWS = 128


def _load_weights_bf16(hbm_refs, vmem_refs, stage, sem):
    width = stage.shape[-1]
    fills = []
    for w_hbm, w_vmem in zip(hbm_refs, vmem_refs):
        rows, cols = w_hbm.shape
        assert rows % CAST_ROWS == 0 and cols <= width
        per_fill = width // cols
        row_chunks = list(range(0, rows, CAST_ROWS))
        for f0 in range(0, len(row_chunks), per_fill):
            fills.append([(w_hbm, w_vmem, r0, cols, q * cols)
                          for q, r0 in enumerate(row_chunks[f0:f0 + per_fill])])

    def copies(f):
        return [pltpu.make_async_copy(w_hbm.at[pl.ds(r0, CAST_ROWS), :],
                                      stage.at[f % 2, :, pl.ds(lane0, cols)], sem.at[f % 2])
                for w_hbm, _, r0, cols, lane0 in fills[f]]

    for cp in copies(0):
        cp.start()
    for f in range(len(fills)):
        if f + 1 < len(fills):
            for cp in copies(f + 1):
                cp.start()
        for cp in copies(f):
            cp.wait()
        for _, w_vmem, r0, cols, lane0 in fills[f]:
            w_vmem[pl.ds(r0, CAST_ROWS), :] = stage[f % 2, :, lane0:lane0 + cols].astype(bf16)


def _layer_kernel(x_hbm, *refs):
    n_params = N_MIXER_PARAMS + N_FFN_PARAMS
    params = list(refs[:n_params])
    o_hbm, xbuf, xsem, obuf, osem, pre1_ref, xs_re, xs_im, ve_ext, vo_ext, stage, wsem = \
        refs[n_params:n_params + 12]
    ssm_refs = refs[n_params + 12:n_params + 17]
    w_vmem = refs[n_params + 17:]
    w_hbm = [params[p] for p in CAST_PARAMS]
    for p, w in zip(CAST_PARAMS, w_vmem):
        params[p] = w
    mixer_refs = params[:N_MIXER_PARAMS]
    ffn_refs = params[N_MIXER_PARAMS:]
    stage_refs = (mixer_refs, ssm_refs, ffn_refs, xs_re, xs_im, ve_ext, vo_ext)

    for cp in _tile_copies(x_hbm, xbuf, xsem, 0, 0, True):
        cp.start()
    zeros = jnp.zeros((SUBLANES, N_STATE), f32)
    xs_re[0:SUBLANES, :] = zeros
    xs_im[0:SUBLANES, :] = zeros
    ve_ext[0:SUBLANES, :] = jnp.zeros((SUBLANES, CONV_WIDTH), f32)
    vo_ext[0:SUBLANES, :] = jnp.zeros((SUBLANES, CONV_WIDTH), f32)
    _ssm_prep(*mixer_refs[2:5], *ssm_refs)
    _load_weights_bf16(w_hbm, w_vmem, stage, wsem)

    for cp in _tile_copies(x_hbm, xbuf, xsem, 1, 1, True):
        cp.start()
    _tile_wait(xbuf, xsem, 0)
    _, pre1 = _layer_tile(None, _split_phases(xbuf[0]), *stage_refs)
    pre1_ref[...] = pre1

    def step(i, carry):
        slot = i % 2
        ffn_slot = 1 - slot

        @pl.when(i + 1 < N_TILES)
        def _():
            for cp in _tile_copies(x_hbm, xbuf, xsem, i + 1, 1 - slot, True):
                cp.start()

        _tile_wait(xbuf, xsem, slot)

        @pl.when(i >= 3)
        def _():
            _tile_wait(obuf, osem, ffn_slot)

        y, pre1 = _layer_tile(pre1_ref[...], _split_phases(xbuf[slot]), *stage_refs)
        obuf[ffn_slot] = _merge_phases(y)
        pre1_ref[...] = pre1
        for cp in _tile_copies(o_hbm, obuf, osem, i - 1, ffn_slot, False):
            cp.start(priority=OUT_DMA_PRIORITY)
        return carry

    lax.fori_loop(1, N_TILES, step, 0)

    last_slot = (N_TILES - 1) % 2
    _tile_wait(obuf, osem, last_slot)
    y, _ = _layer_tile(pre1_ref[...], None, *stage_refs)
    obuf[last_slot] = _merge_phases(y)
    for cp in _tile_copies(o_hbm, obuf, osem, N_TILES - 1, last_slot, False):
        cp.start(priority=OUT_DMA_PRIORITY)
    _tile_wait(obuf, osem, 1 - last_slot)
    _tile_wait(obuf, osem, last_slot)


def _ssm_prep(vec_ref, b_ref, c_ref, bmat_ref, a2_ref, wre_ref, wim_ref, kd_ref):
    lam_re, lam_im = vec_ref[0:1, :], vec_ref[1:2, :]
    dt = jnp.exp(vec_ref[2:3, :])
    mag = jnp.exp(lam_re * dt)
    a_re = mag * jnp.cos(lam_im * dt)
    a_im = mag * jnp.sin(lam_im * dt)
    den = lam_re * lam_re + lam_im * lam_im
    num_re = a_re - 1.0
    fr = (num_re * lam_re + a_im * lam_im) / den
    fi = (a_im * lam_re - num_re * lam_im) / den
    a2_re = a_re * a_re - a_im * a_im
    a2_im = 2.0 * a_re * a_im
    a2_ref[0:1, :] = a2_re
    a2_ref[1:2, :] = a2_im

    def cmul(xr, xi, yr, yi):
        return xr * yr - xi * yi, xr * yi + xi * yr

    b_re, b_im, c_re, c_im = b_ref[0], b_ref[1], c_ref[0], c_ref[1]
    bb_re, bb_im = cmul(fr, fi, b_re, b_im)
    abb_re, abb_im = cmul(a_re, a_im, bb_re, bb_im)
    ca_re, ca_im = cmul(a_re, a_im, c_re, c_im)
    ca2_re, ca2_im = cmul(a2_re, a2_im, c_re, c_im)

    row_group = lax.broadcasted_iota(jnp.int32, (BLOCK_CH, BLOCK_ST), 0) // SSM_GROUP
    col_group = lax.broadcasted_iota(jnp.int32, (BLOCK_CH, BLOCK_ST), 1) // SSM_STATE
    diag = row_group == col_group

    for j in range(N_BLOCKS):
        def bd(m):
            blk = m[:, j * BLOCK_ST:(j + 1) * BLOCK_ST]
            return jnp.where(diag, jnp.concatenate([blk] * GROUPS_PER_BLOCK, axis=0), 0.0)

        def nt(x, y):
            return lax.dot_general(x, y, (((1,), (1,)), ((), ())), precision=lax.Precision.HIGHEST,
                                   preferred_element_type=f32)

        bbr, bbi, abr, abi = bd(bb_re), bd(bb_im), bd(abb_re), bd(abb_im)
        cr, ci = bd(c_re), bd(c_im)
        bmat_ref[j, 0:BLOCK_CH, 0:BLOCK_ST] = abr.astype(bf16)
        bmat_ref[j, 0:BLOCK_CH, BLOCK_ST:2 * BLOCK_ST] = abi.astype(bf16)
        bmat_ref[j, BLOCK_CH:2 * BLOCK_CH, 0:BLOCK_ST] = bbr.astype(bf16)
        bmat_ref[j, BLOCK_CH:2 * BLOCK_CH, BLOCK_ST:2 * BLOCK_ST] = bbi.astype(bf16)
        wre_ref[j, :, 0:BLOCK_CH] = bd(ca_re).T.astype(bf16)
        wre_ref[j, :, BLOCK_CH:2 * BLOCK_CH] = bd(ca2_re).T.astype(bf16)
        wim_ref[j, :, 0:BLOCK_CH] = (-bd(ca_im)).T.astype(bf16)
        wim_ref[j, :, BLOCK_CH:2 * BLOCK_CH] = (-bd(ca2_im)).T.astype(bf16)
        k0 = nt(bbr, cr) - nt(bbi, ci)
        k1 = nt(abr, cr) - nt(abi, ci)
        kd_ref[j, 0:BLOCK_CH, 0:BLOCK_CH] = k0.astype(bf16)
        kd_ref[j, 0:BLOCK_CH, BLOCK_CH:2 * BLOCK_CH] = k1.astype(bf16)
        kd_ref[j, BLOCK_CH:2 * BLOCK_CH, 0:BLOCK_CH] = jnp.zeros((BLOCK_CH, BLOCK_CH), bf16)
        kd_ref[j, BLOCK_CH:2 * BLOCK_CH, BLOCK_CH:2 * BLOCK_CH] = k0.astype(bf16)


def _ssm_inputs(lam_re, lam_im, log_dt, b_re, b_im, c_re, c_im):
    ldt = jnp.broadcast_to(log_dt[:, None], (SSM_GROUPS, SSM_STATE))
    vec = jnp.stack([lam_re, lam_im, ldt]).astype(f32).reshape(3, N_STATE)
    b_t = jnp.transpose(jnp.stack([b_re, b_im]).astype(f32), (0, 3, 1, 2))
    c_t = jnp.transpose(jnp.stack([c_re, c_im]).astype(f32), (0, 2, 1, 3))
    return vec, b_t.reshape(2, SSM_GROUP, N_STATE), c_t.reshape(2, SSM_GROUP, N_STATE)


def _layer(x, w_in, b_in, lam_re, lam_im, log_dt, b_re, b_im, c_re, c_im, d_skip, glu_w, glu_b,
           w_ssm_out, conv_w, w_conv_out, w_o, ln1_g, ln1_b, w_gate, w_up, w_down, ln2_g, ln2_b):
    w32 = lambda w: w.astype(f32)
    flat = lambda *vs: jnp.concatenate([v.astype(f32).reshape(-1) for v in vs])
    small = flat(b_in, d_skip, glu_b, conv_w, jnp.zeros((CONV_WIDTH,), f32), ln1_g, ln1_b, ln2_g, ln2_b)
    small = jnp.pad(small, (0, SMALL_ROWS * SMALL_COLS - small.size)).reshape(SMALL_ROWS, SMALL_COLS)
    mixer_in = (w32(w_in), small, *_ssm_inputs(lam_re, lam_im, log_dt, b_re, b_im, c_re, c_im),
                w32(glu_w), w32(w_ssm_out), w32(w_conv_out), w32(w_o))
    ffn_in = (w32(w_gate), w32(w_up), w32(w_down))
    assert len(mixer_in) == N_MIXER_PARAMS and len(ffn_in) == N_FFN_PARAMS
    params = mixer_in + ffn_in
    param_specs = [pl.BlockSpec(memory_space=pl.ANY if p in CAST_PARAMS else pltpu.VMEM)
                   for p, a in enumerate(params)]
    stage_cols = max(params[p].shape[1] for p in CAST_PARAMS)
    return pl.pallas_call(
        _layer_kernel,
        in_specs=[pl.BlockSpec(memory_space=pl.ANY)] + param_specs,
        out_specs=pl.BlockSpec(memory_space=pl.ANY),
        out_shape=jax.ShapeDtypeStruct((BATCH, SEQ, D_MODEL), f32),
        scratch_shapes=[pltpu.VMEM((2, T_TILE, BATCH, D_MODEL), f32),
                        pltpu.SemaphoreType.DMA((2,)),
                        pltpu.VMEM((2, T_TILE, BATCH, D_MODEL), f32),
                        pltpu.SemaphoreType.DMA((2,)),
                        pltpu.VMEM((R_TILE, D_MODEL), f32),
                        pltpu.VMEM((HALF + SUBLANES, N_STATE), f32),
                        pltpu.VMEM((HALF + SUBLANES, N_STATE), f32),
                        pltpu.VMEM((HALF + SUBLANES, CONV_WIDTH), f32),
                        pltpu.VMEM((HALF + SUBLANES, CONV_WIDTH), f32),
                        pltpu.VMEM((2, CAST_ROWS, stage_cols), f32),
                        pltpu.SemaphoreType.DMA((2,)),
                        pltpu.VMEM((N_BLOCKS, 2 * BLOCK_CH, 2 * BLOCK_ST), bf16),
                        pltpu.VMEM((2, N_STATE), f32),
                        pltpu.VMEM((N_BLOCKS, BLOCK_ST, 2 * BLOCK_CH), bf16),
                        pltpu.VMEM((N_BLOCKS, BLOCK_ST, 2 * BLOCK_CH), bf16),
                        pltpu.VMEM((N_BLOCKS, 2 * BLOCK_CH, 2 * BLOCK_CH), bf16)]
                       + [pltpu.VMEM(params[p].shape, bf16) for p in CAST_PARAMS],
        compiler_params=pltpu.CompilerParams(vmem_limit_bytes=VMEM_LIMIT_BYTES),
        name="layer",
    )(x, *params)


def kernel(x, w_in, b_in, ssm_lambda_re, ssm_lambda_im, ssm_log_dt, ssm_b_re, ssm_b_im, ssm_c_re, ssm_c_im, ssm_d, glu_w, glu_b, w_ssm_out, conv_w, w_conv_out, w_o, ln1_g, ln1_b, w_gate, w_up, w_down, ln2_g, ln2_b):
    assert x.shape == (BATCH, SEQ, D_MODEL) and BATCH == SUBLANES
    for l in range(w_in.shape[0]):
        x = _layer(x, w_in[l], b_in[l], ssm_lambda_re[l], ssm_lambda_im[l], ssm_log_dt[l],
                   ssm_b_re[l], ssm_b_im[l], ssm_c_re[l], ssm_c_im[l], ssm_d[l], glu_w[l],
                   glu_b[l], w_ssm_out[l], conv_w[l], w_conv_out[l], w_o[l], ln1_g[l], ln1_b[l],
                   w_gate[l], w_up[l], w_down[l], ln2_g[l], ln2_b[l])
    return x
```

```python
import jax
import jax.numpy as jnp
from jax import lax
from jax.experimental import pallas as pl
from jax.experimental.pallas import tpu as pltpu

D_MODEL = 1024
BATCH = 8
SEQ = 4096
SSM_WIDTH = D_MODEL // 2
SSM_GROUP = 16
SSM_GROUPS = SSM_WIDTH // SSM_GROUP
SSM_STATE = 64
CONV_WIDTH = D_MODEL // 2
CONV_K = 3
FFN_HIDDEN = 2816
IN_COLS = SSM_WIDTH + 3 * CONV_WIDTH + 2 * D_MODEL
DEPTH = 1
ALPHA = (2.0 * DEPTH) ** 0.25
LN_EPS = 1e-5

SUBLANES = 8
VMEM_LIMIT_BYTES = 60 * 1024 * 1024

GROUPS_PER_BLOCK = 8
N_BLOCKS = SSM_GROUPS // GROUPS_PER_BLOCK
BLOCK_CH = GROUPS_PER_BLOCK * SSM_GROUP
BLOCK_ST = GROUPS_PER_BLOCK * SSM_STATE
N_STATE = SSM_GROUPS * SSM_STATE

PHASES = 2
T_TILE = 32
R_TILE = T_TILE * BATCH
H_TILE = T_TILE // PHASES
HALF = H_TILE * SUBLANES
N_TILES = SEQ // T_TILE
assert N_TILES >= 4 and SEQ % T_TILE == 0 and T_TILE % PHASES == 0

f32 = jnp.float32
bf16 = jnp.bfloat16


def _sigmoid(v):
    return 0.5 * jnp.tanh(0.5 * v) + 0.5


def _layer_norm(y, g, b):
    mu = jnp.mean(y, axis=-1, keepdims=True)
    var = jnp.mean(jnp.square(y - mu), axis=-1, keepdims=True)
    return (y - mu) * lax.rsqrt(var + LN_EPS) * g + b


def _tile_copies(hbm, buf, sem, tile, slot, to_vmem):
    copies = []
    for b in range(BATCH):
        h = hbm.at[b, pl.ds(tile * T_TILE, T_TILE), :]
        v = buf.at[slot, :, b, :]
        src, dst = (h, v) if to_vmem else (v, h)
        copies.append(pltpu.make_async_copy(src, dst, sem.at[slot]))
    return copies


def _tile_wait(buf, sem, slot):
    pltpu.make_async_copy(buf.at[slot], buf.at[slot], sem.at[slot]).wait()


def _split_phases(x3):
    x4 = x3.reshape(H_TILE, PHASES, SUBLANES, x3.shape[-1])
    return jnp.concatenate([x4[:, ph].reshape(HALF, x3.shape[-1]) for ph in range(PHASES)], axis=0)


def _merge_phases(y):
    parts = [y[ph * HALF:(ph + 1) * HALF].reshape(H_TILE, SUBLANES, y.shape[-1])
             for ph in range(PHASES)]
    return jnp.stack(parts, axis=1).reshape(T_TILE, SUBLANES, y.shape[-1])


def _layer_tile(pre1, x, mixer_refs, ssm_refs, ffn_refs, xs_re, xs_im, ve_ext, vo_ext):
    win_ref, small_ref = mixer_refs[:2]
    gluw_ref, wssm_ref, wconv_ref, wo_ref = mixer_refs[5:]
    bmat_ref, a2_ref, wre_ref, wim_ref, kd_ref = ssm_refs
    wg_ref, wu_ref, wd_ref = ffn_refs
    half_row = lambda r, k: small_ref[r:r + 1, k * SMALL_COLS // 2:(k + 1) * SMALL_COLS // 2]
    full_row = lambda r: small_ref[r:r + 1, :]
    ffn_on, mixer_on = pre1 is not None, x is not None
    dot = lambda a, w: jnp.dot(a, w, preferred_element_type=f32)
    if ffn_on:
        x1 = _layer_norm(pre1, full_row(ROW_LN1_G), full_row(ROW_LN1_B))
        x1b = x1.astype(bf16)
    if mixer_on:
        xb = x.astype(bf16)

        def proj(lo, hi):
            c0 = lo % SMALL_COLS
            return dot(xb, win_ref[:, lo:hi]) + small_ref[lo // SMALL_COLS:lo // SMALL_COLS + 1,
                                                          c0:c0 + hi - lo]

        u = proj(0, SSM_WIDTH)
        h = proj(SSM_WIDTH, SSM_WIDTH + CONV_WIDTH)
        ub = u.astype(bf16)
        u2 = []
        for j in range(N_BLOCKS):
            cs = slice(j * BLOCK_CH, (j + 1) * BLOCK_CH)
            u2.append(jnp.concatenate([ub[:HALF, cs], ub[HALF:, cs]], axis=-1))
            bu = dot(u2[j], bmat_ref[j])
            xs_re[SUBLANES:SUBLANES + HALF, j * BLOCK_ST:(j + 1) * BLOCK_ST] = bu[:, :BLOCK_ST]
            xs_im[SUBLANES:SUBLANES + HALF, j * BLOCK_ST:(j + 1) * BLOCK_ST] = bu[:, BLOCK_ST:]

        o4 = SSM_WIDTH + 3 * CONV_WIDTH
        c_gate = proj(SSM_WIDTH + CONV_WIDTH, SSM_WIDTH + 2 * CONV_WIDTH)
        b_gate = proj(SSM_WIDTH + 2 * CONV_WIDTH, o4)
        gate_a = proj(o4, o4 + D_MODEL)
        gate_b = proj(o4 + D_MODEL, o4 + 2 * D_MODEL)

        for j in range(N_BLOCKS):
            sl = slice(j * BLOCK_ST, (j + 1) * BLOCK_ST)
            ar = jnp.broadcast_to(a2_ref[0:1, sl], (SUBLANES, BLOCK_ST))
            ai = jnp.broadcast_to(a2_ref[1:2, sl], (SUBLANES, BLOCK_ST))
            sr, si = xs_re[0:SUBLANES, sl], xs_im[0:SUBLANES, sl]
            for k in range(H_TILE):
                r = slice((k + 1) * SUBLANES, (k + 2) * SUBLANES)
                sr, si = (ar * sr - ai * si + xs_re[r, sl], ar * si + ai * sr + xs_im[r, sl])
                xs_re[r, sl] = sr
                xs_im[r, sl] = si

    if ffn_on:
        gate = dot(x1b, wg_ref[...])

    if mixer_on:
        y_e, y_o = [], []
        for j in range(N_BLOCKS):
            sl = slice(j * BLOCK_ST, (j + 1) * BLOCK_ST)
            y2 = (dot(xs_re[0:HALF, sl].astype(bf16), wre_ref[j])
                  + dot(xs_im[0:HALF, sl].astype(bf16), wim_ref[j]) + dot(u2[j], kd_ref[j]))
            y_e.append(y2[:, :BLOCK_CH])
            y_o.append(y2[:, BLOCK_CH:])
        xs_re[0:SUBLANES, :] = xs_re[HALF:HALF + SUBLANES, :]
        xs_im[0:SUBLANES, :] = xs_im[HALF:HALF + SUBLANES, :]
        y_a = jnp.concatenate([jnp.concatenate(y_e, axis=-1), jnp.concatenate(y_o, axis=-1)], axis=0)
        y_a = y_a + half_row(ROW_DSKIP_GLUB, 0) * u

    if ffn_on:
        up = dot(x1b, wu_ref[...])

    if mixer_on:
        v = c_gate * h
        ve_ext[SUBLANES:SUBLANES + HALF, :] = v[:HALF]
        vo_ext[SUBLANES:SUBLANES + HALF, :] = v[HALF:]
        w0, w1, w2 = half_row(ROW_CONV, 0), half_row(ROW_CONV, 1), half_row(ROW_CONV + 1, 0)
        z_e = w0 * ve_ext[0:HALF, :] + w1 * vo_ext[0:HALF, :] + w2 * v[:HALF]
        z_o = w0 * vo_ext[0:HALF, :] + w1 * v[:HALF] + w2 * v[HALF:]
        ve_ext[0:SUBLANES, :] = ve_ext[HALF:HALF + SUBLANES, :]
        vo_ext[0:SUBLANES, :] = vo_ext[HALF:HALF + SUBLANES, :]
        bz = (b_gate * jnp.concatenate([z_e, z_o], axis=0)).astype(bf16)

        g = jax.nn.gelu(y_a)
        glu = dot(g.astype(bf16), gluw_ref[...]) + half_row(ROW_DSKIP_GLUB, 1)
        y_b = dot(bz, wconv_ref[...])

    if ffn_on:
        hid = (gate * _sigmoid(gate) * up).astype(bf16)
        ffn_lo = dot(hid, wd_ref[:, 0:D_MODEL // 2])
    if mixer_on:
        y_a = dot((g * _sigmoid(glu)).astype(bf16), wssm_ref[...])
    if ffn_on:
        ffn_hi = dot(hid, wd_ref[:, D_MODEL // 2:D_MODEL])
        ffn = jnp.concatenate([ffn_lo, ffn_hi], axis=-1)
    out = pre1_next = None
    if mixer_on:
        merged = _sigmoid(gate_a) * y_a + _sigmoid(gate_b) * y_b
        pre1_next = ALPHA * x + dot(merged.astype(bf16), wo_ref[...])
    if ffn_on:
        out = _layer_norm(ALPHA * x1 + ffn, full_row(ROW_LN2_G), full_row(ROW_LN2_B))
    return out, pre1_next


N_MIXER_PARAMS = 9
N_FFN_PARAMS = 3
CAST_PARAMS = (0, 5, 6, 7, 8, 9, 10, 11)
SMALL_COLS = D_MODEL
ROW_DSKIP_GLUB = IN_COLS // SMALL_COLS
ROW_CONV = ROW_DSKIP_GLUB + 1
ROW_LN1_G, ROW_LN1_B, ROW_LN2_G, ROW_LN2_B = (ROW_CONV + 2 + r for r in range(4))
SMALL_ROWS = 16
OUT_DMA_PRIORITY = 0
CAST_ROWS = 128


def _load_weights_bf16(hbm_refs, vmem_refs, stage, sem):
    width = stage.shape[-1]
    fills = []
    for w_hbm, w_vmem in zip(hbm_refs, vmem_refs):
        rows, cols = w_hbm.shape
        assert rows % CAST_ROWS == 0 and cols <= width
        per_fill = width // cols
        row_chunks = list(range(0, rows, CAST_ROWS))
        for f0 in range(0, len(row_chunks), per_fill):
            fills.append([(w_hbm, w_vmem, r0, cols, q * cols)
                          for q, r0 in enumerate(row_chunks[f0:f0 + per_fill])])

    def copies(f):
        return [pltpu.make_async_copy(w_hbm.at[pl.ds(r0, CAST_ROWS), :],
                                      stage.at[f % 2, :, pl.ds(lane0, cols)], sem.at[f % 2])
                for w_hbm, _, r0, cols, lane0 in fills[f]]

    for cp in copies(0):
        cp.start()
    for f in range(len(fills)):
        if f + 1 < len(fills):
            for cp in copies(f + 1):
                cp.start()
        for cp in copies(f):
            cp.wait()
        for _, w_vmem, r0, cols, lane0 in fills[f]:
            w_vmem[pl.ds(r0, CAST_ROWS), :] = stage[f % 2, :, lane0:lane0 + cols].astype(bf16)


def _layer_kernel(x_hbm, *refs):
    n_params = N_MIXER_PARAMS + N_FFN_PARAMS
    params = list(refs[:n_params])
    o_hbm, xbuf, xsem, obuf, osem, pre1_ref, xs_re, xs_im, ve_ext, vo_ext, stage, wsem = \
        refs[n_params:n_params + 12]
    ssm_refs = refs[n_params + 12:n_params + 17]
    w_vmem = refs[n_params + 17:]
    w_hbm = [params[p] for p in CAST_PARAMS]
    for p, w in zip(CAST_PARAMS, w_vmem):
        params[p] = w
    mixer_refs = params[:N_MIXER_PARAMS]
    ffn_refs = params[N_MIXER_PARAMS:]
    stage_refs = (mixer_refs, ssm_refs, ffn_refs, xs_re, xs_im, ve_ext, vo_ext)

    for cp in _tile_copies(x_hbm, xbuf, xsem, 0, 0, True):
        cp.start()
    zeros = jnp.zeros((SUBLANES, N_STATE), f32)
    xs_re[0:SUBLANES, :] = zeros
    xs_im[0:SUBLANES, :] = zeros
    ve_ext[0:SUBLANES, :] = jnp.zeros((SUBLANES, CONV_WIDTH), f32)
    vo_ext[0:SUBLANES, :] = jnp.zeros((SUBLANES, CONV_WIDTH), f32)
    _ssm_prep(*mixer_refs[2:5], *ssm_refs)
    _load_weights_bf16(w_hbm, w_vmem, stage, wsem)

    for cp in _tile_copies(x_hbm, xbuf, xsem, 1, 1, True):
        cp.start()
    _tile_wait(xbuf, xsem, 0)
    _, pre1 = _layer_tile(None, _split_phases(xbuf[0]), *stage_refs)
    pre1_ref[...] = pre1

    def step(i, carry):
        slot = i % 2
        ffn_slot = 1 - slot

        @pl.when(i + 1 < N_TILES)
        def _():
            for cp in _tile_copies(x_hbm, xbuf, xsem, i + 1, 1 - slot, True):
                cp.start(priority=1 - OUT_DMA_PRIORITY)

        _tile_wait(xbuf, xsem, slot)

        @pl.when(i >= 3)
        def _():
            _tile_wait(obuf, osem, ffn_slot)

        y, pre1 = _layer_tile(pre1_ref[...], _split_phases(xbuf[slot]), *stage_refs)
        obuf[ffn_slot] = _merge_phases(y)
        pre1_ref[...] = pre1
        for cp in _tile_copies(o_hbm, obuf, osem, i - 1, ffn_slot, False):
            cp.start(priority=OUT_DMA_PRIORITY)
        return carry

    lax.fori_loop(1, N_TILES, step, 0)

    last_slot = (N_TILES - 1) % 2
    _tile_wait(obuf, osem, last_slot)
    y, _ = _layer_tile(pre1_ref[...], None, *stage_refs)
    obuf[last_slot] = _merge_phases(y)
    for cp in _tile_copies(o_hbm, obuf, osem, N_TILES - 1, last_slot, False):
        cp.start(priority=OUT_DMA_PRIORITY)
    _tile_wait(obuf, osem, 1 - last_slot)
    _tile_wait(obuf, osem, last_slot)


def _ssm_prep(vec_ref, b_ref, c_ref, bmat_ref, a2_ref, wre_ref, wim_ref, kd_ref):
    lam_re, lam_im = vec_ref[0:1, :], vec_ref[1:2, :]
    dt = jnp.exp(vec_ref[2:3, :])
    mag = jnp.exp(lam_re * dt)
    a_re = mag * jnp.cos(lam_im * dt)
    a_im = mag * jnp.sin(lam_im * dt)
    den = lam_re * lam_re + lam_im * lam_im
    num_re = a_re - 1.0
    fr = (num_re * lam_re + a_im * lam_im) / den
    fi = (a_im * lam_re - num_re * lam_im) / den
    a2_re = a_re * a_re - a_im * a_im
    a2_im = 2.0 * a_re * a_im
    a2_ref[0:1, :] = a2_re
    a2_ref[1:2, :] = a2_im

    def cmul(xr, xi, yr, yi):
        return xr * yr - xi * yi, xr * yi + xi * yr

    b_re, b_im, c_re, c_im = b_ref[0], b_ref[1], c_ref[0], c_ref[1]
    bb_re, bb_im = cmul(fr, fi, b_re, b_im)
    abb_re, abb_im = cmul(a_re, a_im, bb_re, bb_im)
    ca_re, ca_im = cmul(a_re, a_im, c_re, c_im)
    ca2_re, ca2_im = cmul(a2_re, a2_im, c_re, c_im)

    row_group = lax.broadcasted_iota(jnp.int32, (BLOCK_CH, BLOCK_ST), 0) // SSM_GROUP
    col_group = lax.broadcasted_iota(jnp.int32, (BLOCK_CH, BLOCK_ST), 1) // SSM_STATE
    diag = row_group == col_group

    for j in range(N_BLOCKS):
        def bd(m):
            blk = m[:, j * BLOCK_ST:(j + 1) * BLOCK_ST]
            return jnp.where(diag, jnp.concatenate([blk] * GROUPS_PER_BLOCK, axis=0), 0.0)

        def nt(x, y):
            return lax.dot_general(x, y, (((1,), (1,)), ((), ())), precision=lax.Precision.HIGHEST,
                                   preferred_element_type=f32)

        bbr, bbi, abr, abi = bd(bb_re), bd(bb_im), bd(abb_re), bd(abb_im)
        cr, ci = bd(c_re), bd(c_im)
        bmat_ref[j, 0:BLOCK_CH, 0:BLOCK_ST] = abr.astype(bf16)
        bmat_ref[j, 0:BLOCK_CH, BLOCK_ST:2 * BLOCK_ST] = abi.astype(bf16)
        bmat_ref[j, BLOCK_CH:2 * BLOCK_CH, 0:BLOCK_ST] = bbr.astype(bf16)
        bmat_ref[j, BLOCK_CH:2 * BLOCK_CH, BLOCK_ST:2 * BLOCK_ST] = bbi.astype(bf16)
        wre_ref[j, :, 0:BLOCK_CH] = bd(ca_re).T.astype(bf16)
        wre_ref[j, :, BLOCK_CH:2 * BLOCK_CH] = bd(ca2_re).T.astype(bf16)
        wim_ref[j, :, 0:BLOCK_CH] = (-bd(ca_im)).T.astype(bf16)
        wim_ref[j, :, BLOCK_CH:2 * BLOCK_CH] = (-bd(ca2_im)).T.astype(bf16)
        k0 = nt(bbr, cr) - nt(bbi, ci)
        k1 = nt(abr, cr) - nt(abi, ci)
        kd_ref[j, 0:BLOCK_CH, 0:BLOCK_CH] = k0.astype(bf16)
        kd_ref[j, 0:BLOCK_CH, BLOCK_CH:2 * BLOCK_CH] = k1.astype(bf16)
        kd_ref[j, BLOCK_CH:2 * BLOCK_CH, 0:BLOCK_CH] = jnp.zeros((BLOCK_CH, BLOCK_CH), bf16)
        kd_ref[j, BLOCK_CH:2 * BLOCK_CH, BLOCK_CH:2 * BLOCK_CH] = k0.astype(bf16)


def _ssm_inputs(lam_re, lam_im, log_dt, b_re, b_im, c_re, c_im):
    ldt = jnp.broadcast_to(log_dt[:, None], (SSM_GROUPS, SSM_STATE))
    vec = jnp.stack([lam_re, lam_im, ldt]).astype(f32).reshape(3, N_STATE)
    b_t = jnp.transpose(jnp.stack([b_re, b_im]).astype(f32), (0, 3, 1, 2))
    c_t = jnp.transpose(jnp.stack([c_re, c_im]).astype(f32), (0, 2, 1, 3))
    return vec, b_t.reshape(2, SSM_GROUP, N_STATE), c_t.reshape(2, SSM_GROUP, N_STATE)


def _layer(x, w_in, b_in, lam_re, lam_im, log_dt, b_re, b_im, c_re, c_im, d_skip, glu_w, glu_b,
           w_ssm_out, conv_w, w_conv_out, w_o, ln1_g, ln1_b, w_gate, w_up, w_down, ln2_g, ln2_b):
    w32 = lambda w: w.astype(f32)
    flat = lambda *vs: jnp.concatenate([v.astype(f32).reshape(-1) for v in vs])
    small = flat(b_in, d_skip, glu_b, conv_w, jnp.zeros((CONV_WIDTH,), f32), ln1_g, ln1_b, ln2_g, ln2_b)
    small = jnp.pad(small, (0, SMALL_ROWS * SMALL_COLS - small.size)).reshape(SMALL_ROWS, SMALL_COLS)
    mixer_in = (w32(w_in), small, *_ssm_inputs(lam_re, lam_im, log_dt, b_re, b_im, c_re, c_im),
                w32(glu_w), w32(w_ssm_out), w32(w_conv_out), w32(w_o))
    ffn_in = (w32(w_gate), w32(w_up), w32(w_down))
    assert len(mixer_in) == N_MIXER_PARAMS and len(ffn_in) == N_FFN_PARAMS
    params = mixer_in + ffn_in
    param_specs = [pl.BlockSpec(memory_space=pl.ANY if p in CAST_PARAMS else pltpu.VMEM)
                   for p, a in enumerate(params)]
    stage_cols = max(params[p].shape[1] for p in CAST_PARAMS)
    return pl.pallas_call(
        _layer_kernel,
        in_specs=[pl.BlockSpec(memory_space=pl.ANY)] + param_specs,
        out_specs=pl.BlockSpec(memory_space=pl.ANY),
        out_shape=jax.ShapeDtypeStruct((BATCH, SEQ, D_MODEL), f32),
        scratch_shapes=[pltpu.VMEM((2, T_TILE, BATCH, D_MODEL), f32),
                        pltpu.SemaphoreType.DMA((2,)),
                        pltpu.VMEM((2, T_TILE, BATCH, D_MODEL), f32),
                        pltpu.SemaphoreType.DMA((2,)),
                        pltpu.VMEM((R_TILE, D_MODEL), f32),
                        pltpu.VMEM((HALF + SUBLANES, N_STATE), f32),
                        pltpu.VMEM((HALF + SUBLANES, N_STATE), f32),
                        pltpu.VMEM((HALF + SUBLANES, CONV_WIDTH), f32),
                        pltpu.VMEM((HALF + SUBLANES, CONV_WIDTH), f32),
                        pltpu.VMEM((2, CAST_ROWS, stage_cols), f32),
                        pltpu.SemaphoreType.DMA((2,)),
                        pltpu.VMEM((N_BLOCKS, 2 * BLOCK_CH, 2 * BLOCK_ST), bf16),
                        pltpu.VMEM((2, N_STATE), f32),
                        pltpu.VMEM((N_BLOCKS, BLOCK_ST, 2 * BLOCK_CH), bf16),
                        pltpu.VMEM((N_BLOCKS, BLOCK_ST, 2 * BLOCK_CH), bf16),
                        pltpu.VMEM((N_BLOCKS, 2 * BLOCK_CH, 2 * BLOCK_CH), bf16)]
                       + [pltpu.VMEM(params[p].shape, bf16) for p in CAST_PARAMS],
        compiler_params=pltpu.CompilerParams(vmem_limit_bytes=VMEM_LIMIT_BYTES),
        name="layer",
    )(x, *params)


def kernel(x, w_in, b_in, ssm_lambda_re, ssm_lambda_im, ssm_log_dt, ssm_b_re, ssm_b_im, ssm_c_re, ssm_c_im, ssm_d, glu_w, glu_b, w_ssm_out, conv_w, w_conv_out, w_o, ln1_g, ln1_b, w_gate, w_up, w_down, ln2_g, ln2_b):
    assert x.shape == (BATCH, SEQ, D_MODEL) and BATCH == SUBLANES
    for l in range(w_in.shape[0]):
        x = _layer(x, w_in[l], b_in[l], ssm_lambda_re[l], ssm_lambda_im[l], ssm_log_dt[l],
                   ssm_b_re[l], ssm_b_im[l], ssm_c_re[l], ssm_c_im[l], ssm_d[l], glu_w[l],
                   glu_b[l], w_ssm_out[l], conv_w[l], w_conv_out[l], w_o[l], ln1_g[l], ln1_b[l],
                   w_gate[l], w_up[l], w_down[l], ln2_g[l], ln2_b[l])
    return x
```
